```python
import math
import jax, jax.numpy as jnp
from jax import lax
import numpy as np

D_MODEL = 1024
BATCH = 4
SEQ = 4096
DEPTH = 1
DEC_BATCH = 16
DEC_SEQ = 32
PAST_LEN = 1024

CHUNK = 64
N_META = 16
D_MIX = D_MODEL
RET_HEADS = 4
RET_DK = 128
RET_DV = 128
MLA_HEADS = 4
MLA_NOPE = 128
MLA_ROPE = 64
MLA_V = 128
MLA_Q_LORA = 256
MLA_KV_LORA = 128
D_FF = 2816
ROPE_BASE = 10000.0
LN_EPS = 1e-5
RMS_EPS = 1e-6
Q_BLOCK = 128
DN_ALPHA = (2 * DEPTH) ** 0.25
DN_BETA = (8 * DEPTH) ** -0.25

OFF_KR = RET_HEADS * RET_DK
OFF_VR = 2 * RET_HEADS * RET_DK
OFF_GR = OFF_VR + RET_HEADS * RET_DV
OFF_CQ = OFF_GR + RET_HEADS * RET_DV
OFF_CKV = OFF_CQ + MLA_Q_LORA
OFF_KPE = OFF_CKV + MLA_KV_LORA
D_IN = OFF_KPE + MLA_ROPE

kernel_name = "hymba_retnet_mla_macaron_deepnorm_stream_step"


def layer_norm(x, g, b):
    xf = x.astype(jnp.float32)
    mu = xf.mean(-1, keepdims=True)
    var = jnp.square(xf - mu).mean(-1, keepdims=True)
    return ((xf - mu) * lax.rsqrt(var + LN_EPS) * g.astype(jnp.float32) + b.astype(jnp.float32)).astype(x.dtype)


def rms_norm(x, g):
    xf = x.astype(jnp.float32)
    return (xf * lax.rsqrt(jnp.square(xf).mean(-1, keepdims=True) + RMS_EPS) * g.astype(jnp.float32)).astype(x.dtype)


def rope(x, pos):
    d = x.shape[-1]
    inv = ROPE_BASE ** (-jnp.arange(0, d, 2, dtype=jnp.float32) / d)
    ang = pos.astype(jnp.float32)[:, None] * inv[None, :]
    cos = jnp.cos(ang)[:, None, :]
    sin = jnp.sin(ang)[:, None, :]
    xf = x.astype(jnp.float32)
    x1, x2 = xf[..., : d // 2], xf[..., d // 2:]
    return jnp.concatenate([x1 * cos - x2 * sin, x1 * sin + x2 * cos], -1).astype(x.dtype)


def swiglu(x, w_in, w_out):
    g, u = jnp.split(x @ w_in, 2, axis=-1)
    return (jax.nn.silu(g) * u) @ w_out


def ffn_block(x, w_in, w_out, g, b):
    return layer_norm(DN_ALPHA * x + 0.5 * swiglu(x, w_in, w_out), g, b)


def ret_log_gamma():
    return jnp.log(1.0 - jnp.exp2(-5.0 - jnp.arange(RET_HEADS, dtype=jnp.float32)))


def retention(q, k, v, s0, chunk):
    B, L, H, dk = q.shape
    dv = v.shape[-1]
    n = L // chunk
    f32 = jnp.float32
    lg = ret_log_gamma()
    qc = q.astype(f32).reshape(B, n, chunk, H, dk)
    kc = k.astype(f32).reshape(B, n, chunk, H, dk)
    vc = v.astype(f32).reshape(B, n, chunk, H, dv)
    idx = jnp.arange(chunk, dtype=f32)
    rel = idx[:, None] - idx[None, :]
    dmask = jnp.where(rel >= 0, jnp.exp(lg[:, None, None] * jnp.maximum(rel, 0.0)), 0.0)
    scores = jnp.einsum("bnqhd,bnkhd->bnhqk", qc, kc) * dmask[None, None]
    inner = jnp.einsum("bnhqk,bnkhe->bnqhe", scores, vc)
    kdec = jnp.exp(lg[None, :] * (chunk - 1.0 - idx)[:, None])
    kv = jnp.einsum("bnkhd,kh,bnkhe->bnhde", kc, kdec, vc)
    chunk_decay = jnp.exp(lg * chunk)[:, None, None]

    def step(s, kv_i):
        return chunk_decay * s + kv_i, s

    s_last, s_prev = lax.scan(step, s0.astype(f32), jnp.moveaxis(kv, 1, 0))
    qdec = jnp.exp(lg[None, :] * (idx + 1.0)[:, None])
    cross = jnp.einsum("bnqhd,nbhde,qh->bnqhe", qc, s_prev, qdec)
    return (inner + cross).reshape(B, L, H, dv), s_last


def retention_out(o, g_r, gn_g):
    B, L = o.shape[:2]
    mu = o.mean(-1, keepdims=True)
    var = jnp.square(o - mu).mean(-1, keepdims=True)
    on = ((o - mu) * lax.rsqrt(var + LN_EPS)).reshape(B, L, RET_HEADS * RET_DV) * gn_g.astype(jnp.float32)
    return (on * jax.nn.silu(g_r.astype(jnp.float32))).astype(g_r.dtype)


def mla_latent(ckv_raw, kpe_raw, pos, kv_norm_g):
    ckv = rms_norm(ckv_raw, kv_norm_g)
    kpe = rope(kpe_raw[:, :, None, :], pos)[:, :, 0, :]
    return ckv, kpe


def mla_query(cq, pos, q_norm_g, w_uq):
    B, L, _ = cq.shape
    q = (rms_norm(cq, q_norm_g) @ w_uq).reshape(B, L, MLA_HEADS, MLA_NOPE + MLA_ROPE)
    return jnp.concatenate([q[..., :MLA_NOPE], rope(q[..., MLA_NOPE:], pos)], -1)


def mla_keys(ckv, kpe, w_ukv):
    B, L, _ = ckv.shape
    kv = (ckv @ w_ukv).reshape(B, L, MLA_HEADS, MLA_NOPE + MLA_V)
    k = jnp.concatenate([kv[..., :MLA_NOPE], jnp.broadcast_to(kpe[:, :, None, :], (B, L, MLA_HEADS, MLA_ROPE))], -1)
    return k, kv[..., MLA_NOPE:]


def attend(q, k, v, mask):
    s = jnp.einsum("bqhd,bkhd->bhqk", q, k).astype(jnp.float32) * (MLA_NOPE + MLA_ROPE) ** -0.5
    if mask is not None:
        s = jnp.where(mask, s, -jnp.inf)
    p = jax.nn.softmax(s, axis=-1).astype(v.dtype)
    return jnp.einsum("bhqk,bkhe->bqhe", p, v)


def mixer_front(h, pos, w_mix_in, mla_q_norm_g, mla_w_uq, mla_kv_norm_g):
    B, L, _ = h.shape
    p = h @ w_mix_in
    q_r, k_r, v_r, g_r, cq, ckv, kpe = jnp.split(p, [OFF_KR, OFF_VR, OFF_GR, OFF_CQ, OFF_CKV, OFF_KPE], axis=-1)
    q_r = rope(q_r.reshape(B, L, RET_HEADS, RET_DK), pos)
    k_r = rope(k_r.reshape(B, L, RET_HEADS, RET_DK), pos) * RET_DK ** -0.5
    v_r = v_r.reshape(B, L, RET_HEADS, RET_DV)
    q_m = mla_query(cq, pos, mla_q_norm_g, mla_w_uq)
    ckv, kpe = mla_latent(ckv, kpe, pos, mla_kv_norm_g)
    return q_r, k_r, v_r, g_r, q_m, ckv, kpe


def trunk_tail(h, ret_y, mla_y, w_mix_out, ln2_g, ln2_b, ffn2_w_in, ffn2_w_out, ln3_g, ln3_b):
    mix = jnp.concatenate([ret_y, mla_y.astype(ret_y.dtype)], -1) @ w_mix_out
    h = layer_norm(DN_ALPHA * h + mix, ln2_g, ln2_b)
    return ffn_block(h, ffn2_w_in, ffn2_w_out, ln3_g, ln3_b)


def setup_inputs(seed: int = 0) -> dict:
    key = jax.random.key(seed)
    ks = jax.random.split(key, 23)
    nrm = lambda k, shape, s=1.0: jax.random.normal(k, shape, jnp.float32) * s
    gain = lambda k, n: 1.0 + 0.02 * jax.random.normal(k, (n,), jnp.float32)
    return {
        "x_prompt": nrm(ks[0], (BATCH, SEQ, D_MODEL)),
        "x_sample": nrm(ks[1], (DEC_BATCH, DEC_SEQ, D_MODEL)),
        "cache_mla_ckv": nrm(ks[2], (DEC_BATCH, PAST_LEN, MLA_KV_LORA)),
        "cache_mla_kpe": nrm(ks[3], (DEC_BATCH, PAST_LEN, MLA_ROPE)),
        "state_ret": nrm(ks[4], (DEC_BATCH, RET_HEADS, RET_DK, RET_DV), 0.3),
        "meta_tokens": nrm(ks[5], (N_META, D_MODEL)),
        "ffn1_w_in": nrm(ks[6], (D_MODEL, 2 * D_FF), D_MODEL ** -0.5),
        "ffn1_w_out": nrm(ks[7], (D_FF, D_MODEL), DN_BETA * D_FF ** -0.5),
        "ln1_g": gain(ks[8], D_MODEL),
        "ln1_b": nrm(ks[9], (D_MODEL,), 0.02),
        "w_mix_in": nrm(ks[10], (D_MODEL, D_IN), D_MODEL ** -0.5),
        "ret_gn_g": gain(ks[11], RET_HEADS * RET_DV),
        "mla_q_norm_g": gain(ks[12], MLA_Q_LORA),
        "mla_w_uq": nrm(ks[13], (MLA_Q_LORA, MLA_HEADS * (MLA_NOPE + MLA_ROPE)), MLA_Q_LORA ** -0.5),
        "mla_kv_norm_g": gain(ks[14], MLA_KV_LORA),
        "mla_w_ukv": nrm(ks[15], (MLA_KV_LORA, MLA_HEADS * (MLA_NOPE + MLA_V)), MLA_KV_LORA ** -0.5),
        "w_mix_out": nrm(ks[16], (D_MIX, D_MODEL), DN_BETA * D_MIX ** -0.5),
        "ln2_g": gain(ks[17], D_MODEL),
        "ln2_b": nrm(ks[18], (D_MODEL,), 0.02),
        "ffn2_w_in": nrm(ks[19], (D_MODEL, 2 * D_FF), D_MODEL ** -0.5),
        "ffn2_w_out": nrm(ks[20], (D_FF, D_MODEL), DN_BETA * D_FF ** -0.5),
        "ln3_g": gain(ks[21], D_MODEL),
        "ln3_b": nrm(ks[22], (D_MODEL,), 0.02),
    }


def reference(x_prompt, x_sample, cache_mla_ckv, cache_mla_kpe, state_ret, meta_tokens,
              ffn1_w_in, ffn1_w_out, ln1_g, ln1_b, w_mix_in, ret_gn_g, mla_q_norm_g, mla_w_uq,
              mla_kv_norm_g, mla_w_ukv, w_mix_out, ln2_g, ln2_b, ffn2_w_in, ffn2_w_out, ln3_g, ln3_b):
    B, S, D = x_prompt.shape
    L = N_META + S
    x = jnp.concatenate([jnp.broadcast_to(meta_tokens.astype(x_prompt.dtype)[None], (B, N_META, D)), x_prompt], 1)
    h = x
    for _ in range(DEPTH):
        h = ffn_block(h, ffn1_w_in, ffn1_w_out, ln1_g, ln1_b)
        pos = jnp.arange(L)
        q_r, k_r, v_r, g_r, q_m, p_ckv, p_kpe = mixer_front(h, pos, w_mix_in, mla_q_norm_g, mla_w_uq, mla_kv_norm_g)
        pad = (-L) % CHUNK
        padf = lambda a: jnp.pad(a, ((0, 0), (pad, 0), (0, 0), (0, 0)))
        s0 = jnp.zeros((B, RET_HEADS, RET_DK, RET_DV), jnp.float32)
        o_r, p_state = retention(padf(q_r), padf(k_r), padf(v_r), s0, CHUNK)
        ret_y = retention_out(o_r[:, pad:], g_r, ret_gn_g)
        k_m, v_m = mla_keys(p_ckv, p_kpe, mla_w_ukv)
        meta_o = attend(q_m[:, :N_META], k_m[:, :N_META], v_m[:, :N_META], None)
        key_chunk = jnp.concatenate([jnp.full((N_META,), -1, jnp.int32), jnp.arange(S, dtype=jnp.int32) // CHUNK])
        nb = S // Q_BLOCK
        q_blocks = jnp.moveaxis(q_m[:, N_META:].reshape(B, nb, Q_BLOCK, MLA_HEADS, MLA_NOPE + MLA_ROPE), 1, 0)

        def block(args):
            qb, bi = args
            q_chunk = (bi * Q_BLOCK + jnp.arange(Q_BLOCK, dtype=jnp.int32)) // CHUNK
            mask = key_chunk[None, :] <= q_chunk[:, None]
            return attend(qb, k_m, v_m, mask)

        frame_o = lax.map(block, (q_blocks, jnp.arange(nb, dtype=jnp.int32)))
        frame_o = jnp.moveaxis(frame_o, 0, 1).reshape(B, S, MLA_HEADS, MLA_V)
        mla_y = jnp.concatenate([meta_o, frame_o], 1).reshape(B, L, MLA_HEADS * MLA_V)
        h = trunk_tail(h, ret_y, mla_y, w_mix_out, ln2_g, ln2_b, ffn2_w_in, ffn2_w_out, ln3_g, ln3_b)
    y_prompt = h[:, N_META:]

    DB, T, _ = x_sample.shape
    P = cache_mla_ckv.shape[1]
    hs = x_sample
    for _ in range(DEPTH):
        meta_h = ffn_block(meta_tokens.astype(x_sample.dtype), ffn1_w_in, ffn1_w_out, ln1_g, ln1_b)
        meta_lat = (meta_h @ w_mix_in[:, OFF_CKV:D_IN])[None]
        meta_ckv, meta_kpe = mla_latent(meta_lat[..., :MLA_KV_LORA], meta_lat[..., MLA_KV_LORA:], jnp.arange(N_META), mla_kv_norm_g)
        hs = ffn_block(hs, ffn1_w_in, ffn1_w_out, ln1_g, ln1_b)
        pos_s = N_META + P + jnp.arange(T)
        q_r, k_r, v_r, g_r, q_m, s_ckv, s_kpe = mixer_front(hs, pos_s, w_mix_in, mla_q_norm_g, mla_w_uq, mla_kv_norm_g)
        o_r, s_state = retention(q_r, k_r, v_r, state_ret, T)
        ret_y = retention_out(o_r, g_r, ret_gn_g)
        ckv_all = jnp.concatenate([jnp.broadcast_to(meta_ckv, (DB, N_META, MLA_KV_LORA)).astype(s_ckv.dtype),
                                   cache_mla_ckv.astype(s_ckv.dtype), s_ckv], 1)
        kpe_all = jnp.concatenate([jnp.broadcast_to(meta_kpe, (DB, N_META, MLA_ROPE)).astype(s_kpe.dtype),
                                   cache_mla_kpe.astype(s_kpe.dtype), s_kpe], 1)
        k_s, v_s = mla_keys(ckv_all, kpe_all, mla_w_ukv)
        mla_y = attend(q_m, k_s, v_s, None).reshape(DB, T, MLA_HEADS * MLA_V)
        hs = trunk_tail(hs, ret_y, mla_y, w_mix_out, ln2_g, ln2_b, ffn2_w_in, ffn2_w_out, ln3_g, ln3_b)
    y_sample = hs

    return (y_prompt, y_sample, p_ckv, p_kpe, p_state.astype(x_prompt.dtype),
            s_ckv, s_kpe, s_state.astype(state_ret.dtype))
```

```python
import functools
import math

import jax
import jax.numpy as jnp
from jax import lax
from jax.experimental import pallas as pl
from jax.experimental.pallas import tpu as pltpu

D_MODEL = 1024
DEPTH = 1
CHUNK = 64
N_META = 16
RET_HEADS = 4
RET_DK = 128
RET_DV = 128
MLA_HEADS = 4
MLA_NOPE = 128
MLA_ROPE = 64
MLA_V = 128
MLA_Q_LORA = 256
MLA_KV_LORA = 128
D_FF = 2816
ROPE_BASE = 10000.0
LN_EPS = 1e-5
RMS_EPS = 1e-6
DN_ALPHA = (2 * DEPTH) ** 0.25

OFF_KR = RET_HEADS * RET_DK
OFF_VR = 2 * RET_HEADS * RET_DK
OFF_GR = OFF_VR + RET_HEADS * RET_DV
OFF_CQ = OFF_GR + RET_HEADS * RET_DV
OFF_CKV = OFF_CQ + MLA_Q_LORA
OFF_KPE = OFF_CKV + MLA_KV_LORA
D_IN = OFF_KPE + MLA_ROPE

LANES = 128
D_IN_PAD = D_IN + (LANES - MLA_ROPE)
MLA_QK_PAD = 2 * LANES
RET_W = RET_HEADS * RET_DK
MLA_QW = MLA_HEADS * MLA_QK_PAD
MLA_VW = MLA_HEADS * MLA_V
TAB_W = 5 * LANES

FRONT_TM = 256
ATT_TQ = 512
RET_C = 256
VMEM_LIMIT = 56 * 1024 * 1024

MLA_SCALE = (MLA_NOPE + MLA_ROPE) ** -0.5
RET_LOG_GAMMA = tuple(math.log(1.0 - 2.0 ** (-5.0 - h)) for h in range(RET_HEADS))

f32 = jnp.float32
bf16 = jnp.bfloat16


def _dot(a, b):
    return jnp.dot(a, b, preferred_element_type=f32)


def _dot_nt(a, b):
    return lax.dot_general(a, b, (((1,), (1,)), ((), ())), preferred_element_type=f32)


def _dot_tn(a, b):
    return lax.dot_general(a, b, (((0,), (0,)), ((), ())), preferred_element_type=f32)


def _layer_norm(x, g, b):
    mu = jnp.mean(x, axis=-1, keepdims=True)
    xc = x - mu
    var = jnp.mean(xc * xc, axis=-1, keepdims=True)
    return xc * lax.rsqrt(var + LN_EPS) * g + b


def _rms_norm(x, g):
    return x * lax.rsqrt(jnp.mean(x * x, axis=-1, keepdims=True) + RMS_EPS) * g


def _silu(x):
    return x / (1.0 + jnp.exp(-x))


def _ffn_ln(x, w_in_ref, w_out_ref, g_ref, b_ref):
    xb = x.astype(bf16)
    hg = _dot(xb, w_in_ref[:, :D_FF])
    hu = _dot(xb, w_in_ref[:, D_FF:])
    a = (_silu(hg) * hu).astype(bf16)
    y = _dot(a, w_out_ref[...])
    return _layer_norm(DN_ALPHA * x + 0.5 * y, g_ref[...], b_ref[...])


def _front_kernel(x_ref, tab_ref, w_in_ref, w_out_ref, ln_g_ref, ln_b_ref, w_mix_ref,
                  qn_g_ref, w_uq_ref, kvn_g_ref, w_ukv_ref,
                  h_ref, qr_ref, kr_ref, vr_ref, gr_ref, qm_ref, km_ref, vm_ref, ckv_ref, kpe_ref):
    h = _ffn_ln(x_ref[...], w_in_ref, w_out_ref, ln_g_ref, ln_b_ref)
    h_ref[...] = h
    p = _dot(h.astype(bf16), w_mix_ref[...])

    c_r = tab_ref[:, 0 * LANES:1 * LANES]
    s_r = tab_ref[:, 1 * LANES:2 * LANES]
    c_m = tab_ref[:, 2 * LANES:3 * LANES]
    s_ma = tab_ref[:, 3 * LANES:4 * LANES]
    s_mb = tab_ref[:, 4 * LANES:5 * LANES]

    def rope_ret(xh):
        return xh * c_r + pltpu.roll(xh, 64, 1) * s_r

    def rope_mla(xh):
        return xh * c_m + pltpu.roll(xh, 96, 1) * s_ma + pltpu.roll(xh, 32, 1) * s_mb

    for hd in range(RET_HEADS):
        lo = hd * RET_DK
        qr_ref[:, lo:lo + RET_DK] = rope_ret(p[:, lo:lo + RET_DK]).astype(bf16)
        kr_ref[:, lo:lo + RET_DK] = (rope_ret(p[:, OFF_KR + lo:OFF_KR + lo + RET_DK])
                                     * RET_DK ** -0.5).astype(bf16)
    vr_ref[...] = p[:, OFF_VR:OFF_GR].astype(bf16)
    gr_ref[...] = p[:, OFF_GR:OFF_CQ]

    cq = _rms_norm(p[:, OFF_CQ:OFF_CKV], qn_g_ref[...])
    q = _dot(cq.astype(bf16), w_uq_ref[...])
    ckv = _rms_norm(p[:, OFF_CKV:OFF_KPE], kvn_g_ref[...])
    ckv_ref[...] = ckv
    kpe = rope_mla(p[:, OFF_KPE:D_IN_PAD])
    kpe_ref[...] = kpe
    kpe_b = kpe.astype(bf16)
    kv = _dot(ckv.astype(bf16), w_ukv_ref[...])
    for hd in range(MLA_HEADS):
        lo = hd * MLA_QK_PAD
        qm_ref[:, lo:lo + LANES] = q[:, lo:lo + LANES].astype(bf16)
        qm_ref[:, lo + LANES:lo + 2 * LANES] = rope_mla(q[:, lo + LANES:lo + 2 * LANES]).astype(bf16)
        km_ref[:, lo:lo + LANES] = kv[:, hd * MLA_NOPE:(hd + 1) * MLA_NOPE].astype(bf16)
        km_ref[:, lo + LANES:lo + 2 * LANES] = kpe_b
    vm_ref[...] = kv[:, MLA_HEADS * MLA_NOPE:].astype(bf16)


def _const_spec(shape):
    nd = len(shape)
    return pl.BlockSpec(shape, lambda *_: (0,) * nd, pipeline_mode=pl.Buffered(1))


def _front_call(x, tab, wts, tm, tab_blocks):
    n = x.shape[0]
    grid = (n // tm,)
    row = lambda w: pl.BlockSpec((tm, w), lambda i: (i, 0))
    in_specs = [row(D_MODEL), pl.BlockSpec((tm, TAB_W), lambda i: (i % tab_blocks, 0))]
    in_specs += [_const_spec(w.shape) for w in wts]
    widths = [(D_MODEL, f32), (RET_W, bf16), (RET_W, bf16), (RET_W, bf16), (RET_W, f32),
              (MLA_QW, bf16), (MLA_QW, bf16), (MLA_VW, bf16), (MLA_KV_LORA, f32), (LANES, f32)]
    return pl.pallas_call(
        _front_kernel,
        grid=grid,
        in_specs=in_specs,
        out_specs=[row(w) for w, _ in widths],
        out_shape=[jax.ShapeDtypeStruct((n, w), dt) for w, dt in widths],
        compiler_params=pltpu.CompilerParams(dimension_semantics=("arbitrary",),
                                             vmem_limit_bytes=VMEM_LIMIT),
        name="front",
    )(x, tab, *wts)


def _decay_mask(c, lg):
    r = lax.broadcasted_iota(jnp.int32, (c, c), 0)
    k = lax.broadcasted_iota(jnp.int32, (c, c), 1)
    rel = (r - k).astype(f32)
    return jnp.where(rel >= 0.0, jnp.exp(lg * jnp.maximum(rel, 0.0)), 0.0)


def _row_pow(c, lg, offset, sign):
    i = lax.broadcasted_iota(jnp.int32, (c, LANES), 0).astype(f32)
    return jnp.exp(lg * (offset + sign * i))


def _retention_chunk(q, k, v, s_prev, lg):
    c = q.shape[0]
    scores = _dot_nt(q, k) * _decay_mask(c, lg)
    inner = _dot(scores.astype(bf16), v)
    cross = _dot(q, s_prev.astype(bf16)) * _row_pow(c, lg, 1.0, 1.0)
    kd = (k.astype(f32) * _row_pow(c, lg, c - 1.0, -1.0)).astype(bf16)
    s_new = math.exp(lg * c) * s_prev + _dot_tn(kd, v)
    return inner + cross, s_new


def _retention_out(o, g, gn):
    mu = jnp.mean(o, axis=-1, keepdims=True)
    oc = o - mu
    var = jnp.mean(oc * oc, axis=-1, keepdims=True)
    return oc * lax.rsqrt(var + LN_EPS) * gn * _silu(g)


def _softmax_block(s, m_prev):
    m_new = jnp.maximum(m_prev, jnp.max(s, axis=-1, keepdims=True))
    p = jnp.exp(s - m_new)
    return m_new, p, jnp.sum(p, axis=-1, keepdims=True)


def _mixer_kernel(qm_ref, km_ref, vm_ref, kmeta_ref, vmeta_ref, qr_ref, kr_ref, vr_ref, gr_ref,
                  krmeta_ref, vrmeta_ref, gn_ref, mix_ref, state_ref):
    qi = pl.program_id(1)

    @pl.when(qi == 0)
    def _():
        for hd in range(RET_HEADS):
            lo = hd * RET_DK
            lg = RET_LOG_GAMMA[hd]
            kd = (krmeta_ref[:, lo:lo + RET_DK].astype(f32)
                  * _row_pow(N_META, lg, N_META - 1.0, -1.0)).astype(bf16)
            state_ref[0, hd] = _dot_tn(kd, vrmeta_ref[:, lo:lo + RET_DV])

    for hd in range(RET_HEADS):
        lo = hd * RET_DK
        lg = RET_LOG_GAMMA[hd]
        s = state_ref[0, hd]
        for c in range(ATT_TQ // RET_C):
            rows = slice(c * RET_C, (c + 1) * RET_C)
            o, s = _retention_chunk(qr_ref[rows, lo:lo + RET_DK], kr_ref[rows, lo:lo + RET_DK],
                                    vr_ref[rows, lo:lo + RET_DV], s, lg)
            y = _retention_out(o, gr_ref[rows, lo:lo + RET_DV], gn_ref[:, lo:lo + RET_DV])
            mix_ref[rows, lo:lo + RET_DV] = y.astype(bf16)
        state_ref[0, hd] = s

    r_chunk = lax.broadcasted_iota(jnp.int32, (ATT_TQ, ATT_TQ), 0) // CHUNK
    c_chunk = lax.broadcasted_iota(jnp.int32, (ATT_TQ, ATT_TQ), 1) // CHUNK
    diag_ok = c_chunk <= r_chunk
    for hd in range(MLA_HEADS):
        klo = hd * MLA_QK_PAD
        vlo = hd * MLA_V
        q = qm_ref[:, klo:klo + MLA_QK_PAD]
        s0 = _dot_nt(q, kmeta_ref[:, klo:klo + MLA_QK_PAD]) * MLA_SCALE
        m = jnp.max(s0, axis=-1, keepdims=True)
        p0 = jnp.exp(s0 - m)
        l = jnp.sum(p0, axis=-1, keepdims=True)
        acc = _dot(p0.astype(bf16), vmeta_ref[:, vlo:vlo + MLA_V])

        def step(kt, carry, masked):
            m, l, acc = carry
            start = pl.multiple_of(kt * ATT_TQ, ATT_TQ)
            k = km_ref[0, pl.ds(start, ATT_TQ), klo:klo + MLA_QK_PAD]
            v = vm_ref[0, pl.ds(start, ATT_TQ), vlo:vlo + MLA_V]
            s = _dot_nt(q, k) * MLA_SCALE
            if masked:
                s = jnp.where(diag_ok, s, -jnp.inf)
            m_new, p, psum = _softmax_block(s, m)
            alpha = jnp.exp(m - m_new)
            return m_new, alpha * l + psum, alpha * acc + _dot(p.astype(bf16), v)

        m, l, acc = lax.fori_loop(0, qi, functools.partial(step, masked=False), (m, l, acc))
        m, l, acc = step(qi, (m, l, acc), True)
        mix_ref[:, RET_W + vlo:RET_W + vlo + MLA_V] = (acc / l).astype(bf16)


def _mixer_call(qm, km, vm, kmeta, vmeta, qr, kr, vr, gr, krmeta, vrmeta, gn, batch, seq):
    nq = seq // ATT_TQ
    tile = lambda w: pl.BlockSpec((ATT_TQ, w), lambda b, i: (b * nq + i, 0))
    whole = lambda w: pl.BlockSpec((1, seq, w), lambda b, i: (b, 0, 0))
    const = lambda a: pl.BlockSpec(a.shape, lambda b, i: (0,) * a.ndim)
    return pl.pallas_call(
        _mixer_kernel,
        grid=(batch, nq),
        in_specs=[tile(MLA_QW), whole(MLA_QW), whole(MLA_VW), const(kmeta), const(vmeta),
                  tile(RET_W), tile(RET_W), tile(RET_W), tile(RET_W),
                  const(krmeta), const(vrmeta), const(gn)],
        out_specs=[tile(D_MODEL),
                   pl.BlockSpec((1, RET_HEADS, RET_DK, RET_DV), lambda b, i: (b, 0, 0, 0))],
        out_shape=[jax.ShapeDtypeStruct((batch * seq, D_MODEL), bf16),
                   jax.ShapeDtypeStruct((batch, RET_HEADS, RET_DK, RET_DV), f32)],
        compiler_params=pltpu.CompilerParams(dimension_semantics=("arbitrary", "arbitrary"),
                                             vmem_limit_bytes=VMEM_LIMIT),
        name="mixer",
    )(qm, km.reshape(batch, seq, MLA_QW), vm.reshape(batch, seq, MLA_VW), kmeta, vmeta,
      qr, kr, vr, gr, krmeta, vrmeta, gn)


def _smixer_kernel(qm_ref, knew_ref, vnew_ref, kmeta_ref, vmeta_ref, cckv_ref, ckpe_ref, w_ukv_ref,
                   qr_ref, kr_ref, vr_ref, gr_ref, gn_ref, s0_ref, mix_ref, state_ref):
    for hd in range(RET_HEADS):
        lo = hd * RET_DK
        o, s = _retention_chunk(qr_ref[:, lo:lo + RET_DK], kr_ref[:, lo:lo + RET_DK],
                                vr_ref[:, lo:lo + RET_DV], s0_ref[0, hd], RET_LOG_GAMMA[hd])
        state_ref[0, hd] = s
        y = _retention_out(o, gr_ref[:, lo:lo + RET_DV], gn_ref[:, lo:lo + RET_DV])
        mix_ref[:, lo:lo + RET_DV] = y.astype(bf16)

    kv_c = _dot(cckv_ref[0].astype(bf16), w_ukv_ref[...])
    kpe_c = ckpe_ref[0].astype(bf16)
    for hd in range(MLA_HEADS):
        klo = hd * MLA_QK_PAD
        vlo = hd * MLA_V
        q = qm_ref[:, klo:klo + MLA_QK_PAD]
        s_m = _dot_nt(q, kmeta_ref[:, klo:klo + MLA_QK_PAD]) * MLA_SCALE
        s_c = (_dot_nt(q[:, :LANES], kv_c[:, hd * MLA_NOPE:(hd + 1) * MLA_NOPE].astype(bf16))
               + _dot_nt(q[:, LANES:], kpe_c)) * MLA_SCALE
        s_n = _dot_nt(q, knew_ref[:, klo:klo + MLA_QK_PAD]) * MLA_SCALE
        m = jnp.maximum(jnp.maximum(jnp.max(s_m, axis=-1, keepdims=True),
                                    jnp.max(s_c, axis=-1, keepdims=True)),
                        jnp.max(s_n, axis=-1, keepdims=True))
        p_m = jnp.exp(s_m - m)
        p_c = jnp.exp(s_c - m)
        p_n = jnp.exp(s_n - m)
        l = (jnp.sum(p_m, axis=-1, keepdims=True) + jnp.sum(p_c, axis=-1, keepdims=True)
             + jnp.sum(p_n, axis=-1, keepdims=True))
        v_c = kv_c[:, MLA_HEADS * MLA_NOPE + vlo:MLA_HEADS * MLA_NOPE + vlo + MLA_V].astype(bf16)
        acc = (_dot(p_m.astype(bf16), vmeta_ref[:, vlo:vlo + MLA_V]) + _dot(p_c.astype(bf16), v_c)
               + _dot(p_n.astype(bf16), vnew_ref[:, vlo:vlo + MLA_V]))
        mix_ref[:, RET_W + vlo:RET_W + vlo + MLA_V] = (acc / l).astype(bf16)


def _smixer_call(qm, knew, vnew, kmeta, vmeta, cckv, ckpe, w_ukv, qr, kr, vr, gr, gn, s0, t):
    db, past = cckv.shape[0], cckv.shape[1]
    tile = lambda w: pl.BlockSpec((t, w), lambda b: (b, 0))
    const = lambda a: pl.BlockSpec(a.shape, lambda b: (0,) * a.ndim)
    per = lambda *tail: pl.BlockSpec((1,) + tail, lambda b: (b,) + (0,) * len(tail))
    return pl.pallas_call(
        _smixer_kernel,
        grid=(db,),
        in_specs=[tile(MLA_QW), tile(MLA_QW), tile(MLA_VW), const(kmeta), const(vmeta),
                  per(past, MLA_KV_LORA), per(past, LANES), const(w_ukv),
                  tile(RET_W), tile(RET_W), tile(RET_W), tile(RET_W), const(gn),
                  per(RET_HEADS, RET_DK, RET_DV)],
        out_specs=[tile(D_MODEL), per(RET_HEADS, RET_DK, RET_DV)],
        out_shape=[jax.ShapeDtypeStruct((db * t, D_MODEL), bf16),
                   jax.ShapeDtypeStruct((db, RET_HEADS, RET_DK, RET_DV), f32)],
        compiler_params=pltpu.CompilerParams(dimension_semantics=("arbitrary",),
                                             vmem_limit_bytes=VMEM_LIMIT),
        name="smixer",
    )(qm, knew, vnew, kmeta, vmeta, cckv, ckpe, w_ukv, qr, kr, vr, gr, gn, s0)


def _tail_kernel(mix_ref, h_ref, w_mo_ref, ln2_g_ref, ln2_b_ref, w_in_ref, w_out_ref, ln3_g_ref, ln3_b_ref,
                 y_ref):
    mix = _dot(mix_ref[...], w_mo_ref[...])
    h2 = _layer_norm(DN_ALPHA * h_ref[...] + mix, ln2_g_ref[...], ln2_b_ref[...])
    y_ref[...] = _ffn_ln(h2, w_in_ref, w_out_ref, ln3_g_ref, ln3_b_ref)


def _tail_call(mix, h, wts, tm):
    n = mix.shape[0]
    row = lambda w: pl.BlockSpec((tm, w), lambda i: (i, 0))
    return pl.pallas_call(
        _tail_kernel,
        grid=(n // tm,),
        in_specs=[row(D_MODEL), row(D_MODEL)] + [_const_spec(w.shape) for w in wts],
        out_specs=row(D_MODEL),
        out_shape=jax.ShapeDtypeStruct((n, D_MODEL), f32),
        compiler_params=pltpu.CompilerParams(dimension_semantics=("arbitrary",),
                                             vmem_limit_bytes=VMEM_LIMIT),
        name="tail",
    )(mix, h, *wts)


def _rope_tables(pos):
    posf = pos.astype(f32)[:, None]
    inv_r = ROPE_BASE ** (-jnp.arange(0, RET_DK, 2, dtype=f32) / RET_DK)
    ang = posf * inv_r[None, :]
    c, s = jnp.cos(ang), jnp.sin(ang)
    inv_m = ROPE_BASE ** (-jnp.arange(0, MLA_ROPE, 2, dtype=f32) / MLA_ROPE)
    angm = posf * inv_m[None, :]
    cm, sm = jnp.cos(angm), jnp.sin(angm)
    z = jnp.zeros_like(cm)
    z2 = jnp.zeros_like(c)
    return jnp.concatenate([c, c, -s, s, cm, cm, z2, -sm, z, z2, z, sm, z2], axis=1)


def kernel(x_prompt, x_sample, cache_mla_ckv, cache_mla_kpe, state_ret, meta_tokens,
           ffn1_w_in, ffn1_w_out, ln1_g, ln1_b, w_mix_in, ret_gn_g, mla_q_norm_g, mla_w_uq,
           mla_kv_norm_g, mla_w_ukv, w_mix_out, ln2_g, ln2_b, ffn2_w_in, ffn2_w_out, ln3_g, ln3_b):
    B, S, D = x_prompt.shape
    DB, T, _ = x_sample.shape
    P = cache_mla_ckv.shape[1]
    assert S % ATT_TQ == 0 and (B * S) % FRONT_TM == 0 and S % FRONT_TM == 0 and ATT_TQ % RET_C == 0
    row = lambda a: a.reshape(1, -1).astype(f32)

    w_mix = jnp.pad(w_mix_in, ((0, 0), (0, D_IN_PAD - D_IN))).astype(bf16)
    w_uq = mla_w_uq.reshape(MLA_Q_LORA, MLA_HEADS, MLA_NOPE + MLA_ROPE)
    w_uq = jnp.pad(w_uq, ((0, 0), (0, 0), (0, MLA_QK_PAD - MLA_NOPE - MLA_ROPE)))
    w_uq = w_uq.reshape(MLA_Q_LORA, MLA_QW).astype(bf16)
    w_ukv = mla_w_ukv.reshape(MLA_KV_LORA, MLA_HEADS, MLA_NOPE + MLA_V)
    w_ukv = jnp.concatenate([w_ukv[:, :, :MLA_NOPE].reshape(MLA_KV_LORA, -1),
                             w_ukv[:, :, MLA_NOPE:].reshape(MLA_KV_LORA, -1)], axis=1).astype(bf16)
    front_w = [ffn1_w_in.astype(bf16), ffn1_w_out.astype(bf16), row(ln1_g), row(ln1_b), w_mix,
               row(mla_q_norm_g), w_uq, row(mla_kv_norm_g), w_ukv]
    tail_w = [w_mix_out.astype(bf16), row(ln2_g), row(ln2_b), ffn2_w_in.astype(bf16),
              ffn2_w_out.astype(bf16), row(ln3_g), row(ln3_b)]
    gn = row(ret_gn_g)

    tab_p = _rope_tables(N_META + jnp.arange(S))
    (h_p, qr_p, kr_p, vr_p, gr_p, qm_p, km_p, vm_p, ckv_p, kpe_p) = _front_call(
        x_prompt.reshape(B * S, D), tab_p, front_w, FRONT_TM, S // FRONT_TM)
    n_s = DB * T
    x_small = jnp.concatenate([x_sample.reshape(n_s, D), meta_tokens.astype(x_sample.dtype)], axis=0)
    pos_small = jnp.concatenate([jnp.tile(N_META + P + jnp.arange(T), DB), jnp.arange(N_META)])
    (h_s, qr_s, kr_s, vr_s, gr_s, qm_s, km_s, vm_s, ckv_s, kpe_s) = _front_call(
        x_small, _rope_tables(pos_small), front_w, n_s + N_META, 1)
    kmeta, vmeta = km_s[n_s:], vm_s[n_s:]
    krmeta, vrmeta = kr_s[n_s:], vr_s[n_s:]

    mix_p, p_state = _mixer_call(qm_p, km_p, vm_p, kmeta, vmeta, qr_p, kr_p, vr_p, gr_p,
                                 krmeta, vrmeta, gn, B, S)
    ckpe = jnp.pad(cache_mla_kpe, ((0, 0), (0, 0), (0, LANES - MLA_ROPE)))
    mix_s, s_state = _smixer_call(qm_s[:n_s], km_s[:n_s], vm_s[:n_s], kmeta, vmeta, cache_mla_ckv, ckpe,
                                  w_ukv, qr_s[:n_s], kr_s[:n_s], vr_s[:n_s], gr_s[:n_s], gn, state_ret, T)

    y_p = _tail_call(mix_p, h_p, tail_w, FRONT_TM)
    y_s = _tail_call(mix_s, h_s[:n_s], tail_w, n_s)

    meta_ckv = jnp.broadcast_to(ckv_s[n_s:][None], (B, N_META, MLA_KV_LORA))
    meta_kpe = jnp.broadcast_to(kpe_s[n_s:, :MLA_ROPE][None], (B, N_META, MLA_ROPE))
    p_ckv = jnp.concatenate([meta_ckv, ckv_p.reshape(B, S, MLA_KV_LORA)], axis=1)
    p_kpe = jnp.concatenate([meta_kpe, kpe_p[:, :MLA_ROPE].reshape(B, S, MLA_ROPE)], axis=1)
    return (y_p.reshape(B, S, D), y_s.reshape(DB, T, D), p_ckv, p_kpe, p_state.astype(x_prompt.dtype),
            ckv_s[:n_s].reshape(DB, T, MLA_KV_LORA), kpe_s[:n_s, :MLA_ROPE].reshape(DB, T, MLA_ROPE),
            s_state.astype(state_ret.dtype))
```

```python
import functools
import math

import jax
import jax.numpy as jnp
from jax import lax
from jax.experimental import pallas as pl
from jax.experimental.pallas import tpu as pltpu

D_MODEL = 1024
DEPTH = 1
CHUNK = 64
N_META = 16
RET_HEADS = 4
RET_DK = 128
RET_DV = 128
MLA_HEADS = 4
MLA_NOPE = 128
MLA_ROPE = 64
MLA_V = 128
MLA_Q_LORA = 256
MLA_KV_LORA = 128
D_FF = 2816
ROPE_BASE = 10000.0
LN_EPS = 1e-5
RMS_EPS = 1e-6
DN_ALPHA = (2 * DEPTH) ** 0.25

OFF_KR = RET_HEADS * RET_DK
OFF_VR = 2 * RET_HEADS * RET_DK
OFF_GR = OFF_VR + RET_HEADS * RET_DV
OFF_CQ = OFF_GR + RET_HEADS * RET_DV
OFF_CKV = OFF_CQ + MLA_Q_LORA
OFF_KPE = OFF_CKV + MLA_KV_LORA
D_IN = OFF_KPE + MLA_ROPE

LANES = 128
D_IN_PAD = D_IN + (LANES - MLA_ROPE)
MLA_QK_PAD = 2 * LANES
RET_W = RET_HEADS * RET_DK
MLA_QW = MLA_HEADS * MLA_QK_PAD
MLA_VW = MLA_HEADS * MLA_V
TAB_W = 5 * LANES

FRONT_TM = 256
ATT_TQ = 512
RET_C = 256
VMEM_LIMIT = 56 * 1024 * 1024

MLA_SCALE = (MLA_NOPE + MLA_ROPE) ** -0.5
MLA_SCALE_LOG2E = MLA_SCALE * math.log2(math.e)
RET_LOG_GAMMA = tuple(math.log(1.0 - 2.0 ** (-5.0 - h)) for h in range(RET_HEADS))

f32 = jnp.float32
bf16 = jnp.bfloat16


def _dot(a, b):
    return jnp.dot(a, b, preferred_element_type=f32)


def _dot_nt(a, b):
    return lax.dot_general(a, b, (((1,), (1,)), ((), ())), preferred_element_type=f32)


def _dot_tn(a, b):
    return lax.dot_general(a, b, (((0,), (0,)), ((), ())), preferred_element_type=f32)


def _layer_norm(x, g, b):
    mu = jnp.mean(x, axis=-1, keepdims=True)
    xc = x - mu
    var = jnp.mean(xc * xc, axis=-1, keepdims=True)
    return xc * lax.rsqrt(var + LN_EPS) * g + b


def _rms_norm(x, g):
    return x * lax.rsqrt(jnp.mean(x * x, axis=-1, keepdims=True) + RMS_EPS) * g


def _silu(x):
    return x / (1.0 + jnp.exp(-x))


def _ffn_ln(x, w_in_ref, w_out_ref, g_ref, b_ref):
    xb = x.astype(bf16)
    hg = _dot(xb, w_in_ref[:, :D_FF])
    hu = _dot(xb, w_in_ref[:, D_FF:])
    a = (_silu(hg) * hu).astype(bf16)
    y = _dot(a, w_out_ref[...])
    return _layer_norm(DN_ALPHA * x + 0.5 * y, g_ref[...], b_ref[...])


def _front_kernel(x_ref, tab_ref, w_in_ref, w_out_ref, ln_g_ref, ln_b_ref, w_mix_ref,
                  qn_g_ref, w_uq_ref, kvn_g_ref, w_ukv_ref,
                  h_ref, qr_ref, kr_ref, vr_ref, gr_ref, qm_ref, km_ref, vm_ref, ckv_ref, kpe_ref):
    h = _ffn_ln(x_ref[...], w_in_ref, w_out_ref, ln_g_ref, ln_b_ref)
    h_ref[...] = h
    p = _dot(h.astype(bf16), w_mix_ref[...])

    c_r = tab_ref[:, 0 * LANES:1 * LANES]
    s_r = tab_ref[:, 1 * LANES:2 * LANES]
    c_m = tab_ref[:, 2 * LANES:3 * LANES]
    s_ma = tab_ref[:, 3 * LANES:4 * LANES]
    s_mb = tab_ref[:, 4 * LANES:5 * LANES]

    def rope_ret(xh):
        return xh * c_r + pltpu.roll(xh, 64, 1) * s_r

    def rope_mla(xh):
        return xh * c_m + pltpu.roll(xh, 96, 1) * s_ma + pltpu.roll(xh, 32, 1) * s_mb

    for hd in range(RET_HEADS):
        lo = hd * RET_DK
        qr_ref[:, lo:lo + RET_DK] = rope_ret(p[:, lo:lo + RET_DK]).astype(bf16)
        kr_ref[:, lo:lo + RET_DK] = (rope_ret(p[:, OFF_KR + lo:OFF_KR + lo + RET_DK])
                                     * RET_DK ** -0.5).astype(bf16)
    vr_ref[...] = p[:, OFF_VR:OFF_GR].astype(bf16)
    gr_ref[...] = p[:, OFF_GR:OFF_CQ]

    cq = _rms_norm(p[:, OFF_CQ:OFF_CKV], qn_g_ref[...])
    q = _dot(cq.astype(bf16), w_uq_ref[...])
    ckv = _rms_norm(p[:, OFF_CKV:OFF_KPE], kvn_g_ref[...])
    ckv_ref[...] = ckv
    kpe = rope_mla(p[:, OFF_KPE:D_IN_PAD])
    kpe_ref[...] = kpe
    kpe_b = kpe.astype(bf16)
    kv = _dot(ckv.astype(bf16), w_ukv_ref[...])
    for hd in range(MLA_HEADS):
        lo = hd * MLA_QK_PAD
        qm_ref[:, lo:lo + LANES] = q[:, lo:lo + LANES].astype(bf16)
        qm_ref[:, lo + LANES:lo + 2 * LANES] = rope_mla(q[:, lo + LANES:lo + 2 * LANES]).astype(bf16)
        km_ref[:, lo:lo + LANES] = kv[:, hd * MLA_NOPE:(hd + 1) * MLA_NOPE].astype(bf16)
        km_ref[:, lo + LANES:lo + 2 * LANES] = kpe_b
    vm_ref[...] = kv[:, MLA_HEADS * MLA_NOPE:].astype(bf16)


def _const_spec(shape):
    nd = len(shape)
    return pl.BlockSpec(shape, lambda *_: (0,) * nd, pipeline_mode=pl.Buffered(1))


def _front_call(x, tab, wts, tm, tab_blocks):
    n = x.shape[0]
    grid = (n // tm,)
    row = lambda w: pl.BlockSpec((tm, w), lambda i: (i, 0))
    in_specs = [row(D_MODEL), pl.BlockSpec((tm, TAB_W), lambda i: (i % tab_blocks, 0))]
    in_specs += [_const_spec(w.shape) for w in wts]
    widths = [(D_MODEL, f32), (RET_W, bf16), (RET_W, bf16), (RET_W, bf16), (RET_W, f32),
              (MLA_QW, bf16), (MLA_QW, bf16), (MLA_VW, bf16), (MLA_KV_LORA, f32), (LANES, f32)]
    return pl.pallas_call(
        _front_kernel,
        grid=grid,
        in_specs=in_specs,
        out_specs=[row(w) for w, _ in widths],
        out_shape=[jax.ShapeDtypeStruct((n, w), dt) for w, dt in widths],
        compiler_params=pltpu.CompilerParams(dimension_semantics=("arbitrary",),
                                             vmem_limit_bytes=VMEM_LIMIT),
        name="front",
    )(x, tab, *wts)


def _decay_mask(c, lg):
    r = lax.broadcasted_iota(jnp.int32, (c, c), 0)
    k = lax.broadcasted_iota(jnp.int32, (c, c), 1)
    rel = (r - k).astype(f32)
    return jnp.where(rel >= 0.0, jnp.exp(lg * jnp.maximum(rel, 0.0)), 0.0)


def _row_pow(c, lg, offset, sign):
    i = lax.broadcasted_iota(jnp.int32, (c, LANES), 0).astype(f32)
    return jnp.exp(lg * (offset + sign * i))


def _retention_chunk(q, k, v, s_prev, lg):
    c = q.shape[0]
    scores = _dot_nt(q, k) * _decay_mask(c, lg)
    inner = _dot(scores.astype(bf16), v)
    cross = _dot(q, s_prev.astype(bf16)) * _row_pow(c, lg, 1.0, 1.0)
    kd = (k.astype(f32) * _row_pow(c, lg, c - 1.0, -1.0)).astype(bf16)
    s_new = math.exp(lg * c) * s_prev + _dot_tn(kd, v)
    return inner + cross, s_new


def _retention_out(o, g, gn):
    mu = jnp.mean(o, axis=-1, keepdims=True)
    oc = o - mu
    var = jnp.mean(oc * oc, axis=-1, keepdims=True)
    return oc * lax.rsqrt(var + LN_EPS) * gn * _silu(g)


def _softmax_block(s, m_prev):
    m_new = jnp.maximum(m_prev, jnp.max(s, axis=-1, keepdims=True))
    p = jnp.exp(s - m_new)
    return m_new, p, jnp.sum(p, axis=-1, keepdims=True)


def _mixer_kernel(qm_ref, km_ref, vm_ref, kmeta_ref, vmeta_ref, qr_ref, kr_ref, vr_ref, gr_ref,
                  krmeta_ref, vrmeta_ref, gn_ref, mix_ref, state_ref):
    qi = pl.program_id(1)

    @pl.when(qi == 0)
    def _():
        for hd in range(RET_HEADS):
            lo = hd * RET_DK
            lg = RET_LOG_GAMMA[hd]
            kd = (krmeta_ref[:, lo:lo + RET_DK].astype(f32)
                  * _row_pow(N_META, lg, N_META - 1.0, -1.0)).astype(bf16)
            state_ref[0, hd] = _dot_tn(kd, vrmeta_ref[:, lo:lo + RET_DV])

    for hd in range(RET_HEADS):
        lo = hd * RET_DK
        lg = RET_LOG_GAMMA[hd]
        s = state_ref[0, hd]
        for c in range(ATT_TQ // RET_C):
            rows = slice(c * RET_C, (c + 1) * RET_C)
            o, s = _retention_chunk(qr_ref[rows, lo:lo + RET_DK], kr_ref[rows, lo:lo + RET_DK],
                                    vr_ref[rows, lo:lo + RET_DV], s, lg)
            y = _retention_out(o, gr_ref[rows, lo:lo + RET_DV], gn_ref[:, lo:lo + RET_DV])
            mix_ref[rows, lo:lo + RET_DV] = y.astype(bf16)
        state_ref[0, hd] = s

    r_chunk = lax.broadcasted_iota(jnp.int32, (ATT_TQ, ATT_TQ), 0) // CHUNK
    c_chunk = lax.broadcasted_iota(jnp.int32, (ATT_TQ, ATT_TQ), 1) // CHUNK
    diag_ok = c_chunk <= r_chunk

    carry = []
    for hd in range(MLA_HEADS):
        klo = hd * MLA_QK_PAD
        vlo = hd * MLA_V
        s0 = _dot_nt(qm_ref[:, klo:klo + MLA_QK_PAD], kmeta_ref[:, klo:klo + MLA_QK_PAD]) * MLA_SCALE_LOG2E
        m = jnp.max(s0, axis=-1, keepdims=True)
        p0 = jnp.exp2(s0 - m)
        carry += [m, jnp.sum(p0, axis=-1, keepdims=True),
                  _dot(p0.astype(bf16), vmeta_ref[:, vlo:vlo + MLA_V])]

    def step(kt, carry, masked):
        start = pl.multiple_of(kt * ATT_TQ, ATT_TQ)
        out = []
        for hd in range(MLA_HEADS):
            klo = hd * MLA_QK_PAD
            vlo = hd * MLA_V
            m, l, acc = carry[3 * hd:3 * hd + 3]
            k = km_ref[0, pl.ds(start, ATT_TQ), klo:klo + MLA_QK_PAD]
            v = vm_ref[0, pl.ds(start, ATT_TQ), vlo:vlo + MLA_V]
            s = _dot_nt(qm_ref[:, klo:klo + MLA_QK_PAD], k) * MLA_SCALE_LOG2E
            if masked:
                s = jnp.where(diag_ok, s, -jnp.inf)
            m_new = jnp.maximum(m, jnp.max(s, axis=-1, keepdims=True))
            p = jnp.exp2(s - m_new)
            alpha = jnp.exp2(m - m_new)
            out += [m_new, alpha * l + jnp.sum(p, axis=-1, keepdims=True),
                    alpha * acc + _dot(p.astype(bf16), v)]
        return tuple(out)

    carry = lax.fori_loop(0, qi, functools.partial(step, masked=False), tuple(carry))
    carry = step(qi, carry, True)
    for hd in range(MLA_HEADS):
        vlo = hd * MLA_V
        m, l, acc = carry[3 * hd:3 * hd + 3]
        mix_ref[:, RET_W + vlo:RET_W + vlo + MLA_V] = (acc / l).astype(bf16)


def _mixer_call(qm, km, vm, kmeta, vmeta, qr, kr, vr, gr, krmeta, vrmeta, gn, batch, seq):
    nq = seq // ATT_TQ
    tile = lambda w: pl.BlockSpec((ATT_TQ, w), lambda b, i: (b * nq + i, 0))
    whole = lambda w: pl.BlockSpec((1, seq, w), lambda b, i: (b, 0, 0))
    const = lambda a: pl.BlockSpec(a.shape, lambda b, i: (0,) * a.ndim)
    return pl.pallas_call(
        _mixer_kernel,
        grid=(batch, nq),
        in_specs=[tile(MLA_QW), whole(MLA_QW), whole(MLA_VW), const(kmeta), const(vmeta),
                  tile(RET_W), tile(RET_W), tile(RET_W), tile(RET_W),
                  const(krmeta), const(vrmeta), const(gn)],
        out_specs=[tile(D_MODEL),
                   pl.BlockSpec((1, RET_HEADS, RET_DK, RET_DV), lambda b, i: (b, 0, 0, 0))],
        out_shape=[jax.ShapeDtypeStruct((batch * seq, D_MODEL), bf16),
                   jax.ShapeDtypeStruct((batch, RET_HEADS, RET_DK, RET_DV), f32)],
        compiler_params=pltpu.CompilerParams(dimension_semantics=("arbitrary", "arbitrary"),
                                             vmem_limit_bytes=VMEM_LIMIT),
        name="mixer",
    )(qm, km.reshape(batch, seq, MLA_QW), vm.reshape(batch, seq, MLA_VW), kmeta, vmeta,
      qr, kr, vr, gr, krmeta, vrmeta, gn)


def _smixer_kernel(qm_ref, knew_ref, vnew_ref, kmeta_ref, vmeta_ref, cckv_ref, ckpe_ref, w_ukv_ref,
                   qr_ref, kr_ref, vr_ref, gr_ref, gn_ref, s0_ref, mix_ref, state_ref):
    for hd in range(RET_HEADS):
        lo = hd * RET_DK
        o, s = _retention_chunk(qr_ref[:, lo:lo + RET_DK], kr_ref[:, lo:lo + RET_DK],
                                vr_ref[:, lo:lo + RET_DV], s0_ref[0, hd], RET_LOG_GAMMA[hd])
        state_ref[0, hd] = s
        y = _retention_out(o, gr_ref[:, lo:lo + RET_DV], gn_ref[:, lo:lo + RET_DV])
        mix_ref[:, lo:lo + RET_DV] = y.astype(bf16)

    kv_c = _dot(cckv_ref[0].astype(bf16), w_ukv_ref[...])
    kpe_c = ckpe_ref[0].astype(bf16)
    for hd in range(MLA_HEADS):
        klo = hd * MLA_QK_PAD
        vlo = hd * MLA_V
        q = qm_ref[:, klo:klo + MLA_QK_PAD]
        s_m = _dot_nt(q, kmeta_ref[:, klo:klo + MLA_QK_PAD]) * MLA_SCALE
        s_c = (_dot_nt(q[:, :LANES], kv_c[:, hd * MLA_NOPE:(hd + 1) * MLA_NOPE].astype(bf16))
               + _dot_nt(q[:, LANES:], kpe_c)) * MLA_SCALE
        s_n = _dot_nt(q, knew_ref[:, klo:klo + MLA_QK_PAD]) * MLA_SCALE
        m = jnp.maximum(jnp.maximum(jnp.max(s_m, axis=-1, keepdims=True),
                                    jnp.max(s_c, axis=-1, keepdims=True)),
                        jnp.max(s_n, axis=-1, keepdims=True))
        p_m = jnp.exp(s_m - m)
        p_c = jnp.exp(s_c - m)
        p_n = jnp.exp(s_n - m)
        l = (jnp.sum(p_m, axis=-1, keepdims=True) + jnp.sum(p_c, axis=-1, keepdims=True)
             + jnp.sum(p_n, axis=-1, keepdims=True))
        v_c = kv_c[:, MLA_HEADS * MLA_NOPE + vlo:MLA_HEADS * MLA_NOPE + vlo + MLA_V].astype(bf16)
        acc = (_dot(p_m.astype(bf16), vmeta_ref[:, vlo:vlo + MLA_V]) + _dot(p_c.astype(bf16), v_c)
               + _dot(p_n.astype(bf16), vnew_ref[:, vlo:vlo + MLA_V]))
        mix_ref[:, RET_W + vlo:RET_W + vlo + MLA_V] = (acc / l).astype(bf16)


def _smixer_call(qm, knew, vnew, kmeta, vmeta, cckv, ckpe, w_ukv, qr, kr, vr, gr, gn, s0, t):
    db, past = cckv.shape[0], cckv.shape[1]
    tile = lambda w: pl.BlockSpec((t, w), lambda b: (b, 0))
    const = lambda a: pl.BlockSpec(a.shape, lambda b: (0,) * a.ndim)
    per = lambda *tail: pl.BlockSpec((1,) + tail, lambda b: (b,) + (0,) * len(tail))
    return pl.pallas_call(
        _smixer_kernel,
        grid=(db,),
        in_specs=[tile(MLA_QW), tile(MLA_QW), tile(MLA_VW), const(kmeta), const(vmeta),
                  per(past, MLA_KV_LORA), per(past, LANES), const(w_ukv),
                  tile(RET_W), tile(RET_W), tile(RET_W), tile(RET_W), const(gn),
                  per(RET_HEADS, RET_DK, RET_DV)],
        out_specs=[tile(D_MODEL), per(RET_HEADS, RET_DK, RET_DV)],
        out_shape=[jax.ShapeDtypeStruct((db * t, D_MODEL), bf16),
                   jax.ShapeDtypeStruct((db, RET_HEADS, RET_DK, RET_DV), f32)],
        compiler_params=pltpu.CompilerParams(dimension_semantics=("arbitrary",),
                                             vmem_limit_bytes=VMEM_LIMIT),
        name="smixer",
    )(qm, knew, vnew, kmeta, vmeta, cckv, ckpe, w_ukv, qr, kr, vr, gr, gn, s0)


def _tail_kernel(mix_ref, h_ref, w_mo_ref, ln2_g_ref, ln2_b_ref, w_in_ref, w_out_ref, ln3_g_ref, ln3_b_ref,
                 y_ref):
    mix = _dot(mix_ref[...], w_mo_ref[...])
    h2 = _layer_norm(DN_ALPHA * h_ref[...] + mix, ln2_g_ref[...], ln2_b_ref[...])
    y_ref[...] = _ffn_ln(h2, w_in_ref, w_out_ref, ln3_g_ref, ln3_b_ref)


def _tail_call(mix, h, wts, tm):
    n = mix.shape[0]
    row = lambda w: pl.BlockSpec((tm, w), lambda i: (i, 0))
    return pl.pallas_call(
        _tail_kernel,
        grid=(n // tm,),
        in_specs=[row(D_MODEL), row(D_MODEL)] + [_const_spec(w.shape) for w in wts],
        out_specs=row(D_MODEL),
        out_shape=jax.ShapeDtypeStruct((n, D_MODEL), f32),
        compiler_params=pltpu.CompilerParams(dimension_semantics=("arbitrary",),
                                             vmem_limit_bytes=VMEM_LIMIT),
        name="tail",
    )(mix, h, *wts)


def _rope_tables(pos):
    posf = pos.astype(f32)[:, None]
    inv_r = ROPE_BASE ** (-jnp.arange(0, RET_DK, 2, dtype=f32) / RET_DK)
    ang = posf * inv_r[None, :]
    c, s = jnp.cos(ang), jnp.sin(ang)
    inv_m = ROPE_BASE ** (-jnp.arange(0, MLA_ROPE, 2, dtype=f32) / MLA_ROPE)
    angm = posf * inv_m[None, :]
    cm, sm = jnp.cos(angm), jnp.sin(angm)
    z = jnp.zeros_like(cm)
    z2 = jnp.zeros_like(c)
    return jnp.concatenate([c, c, -s, s, cm, cm, z2, -sm, z, z2, z, sm, z2], axis=1)


def kernel(x_prompt, x_sample, cache_mla_ckv, cache_mla_kpe, state_ret, meta_tokens,
           ffn1_w_in, ffn1_w_out, ln1_g, ln1_b, w_mix_in, ret_gn_g, mla_q_norm_g, mla_w_uq,
           mla_kv_norm_g, mla_w_ukv, w_mix_out, ln2_g, ln2_b, ffn2_w_in, ffn2_w_out, ln3_g, ln3_b):
    B, S, D = x_prompt.shape
    DB, T, _ = x_sample.shape
    P = cache_mla_ckv.shape[1]
    assert S % ATT_TQ == 0 and (B * S) % FRONT_TM == 0 and S % FRONT_TM == 0 and ATT_TQ % RET_C == 0
    row = lambda a: a.reshape(1, -1).astype(f32)

    w_mix = jnp.pad(w_mix_in, ((0, 0), (0, D_IN_PAD - D_IN))).astype(bf16)
    w_uq = mla_w_uq.reshape(MLA_Q_LORA, MLA_HEADS, MLA_NOPE + MLA_ROPE)
    w_uq = jnp.pad(w_uq, ((0, 0), (0, 0), (0, MLA_QK_PAD - MLA_NOPE - MLA_ROPE)))
    w_uq = w_uq.reshape(MLA_Q_LORA, MLA_QW).astype(bf16)
    w_ukv = mla_w_ukv.reshape(MLA_KV_LORA, MLA_HEADS, MLA_NOPE + MLA_V)
    w_ukv = jnp.concatenate([w_ukv[:, :, :MLA_NOPE].reshape(MLA_KV_LORA, -1),
                             w_ukv[:, :, MLA_NOPE:].reshape(MLA_KV_LORA, -1)], axis=1).astype(bf16)
    front_w = [ffn1_w_in.astype(bf16), ffn1_w_out.astype(bf16), row(ln1_g), row(ln1_b), w_mix,
               row(mla_q_norm_g), w_uq, row(mla_kv_norm_g), w_ukv]
    tail_w = [w_mix_out.astype(bf16), row(ln2_g), row(ln2_b), ffn2_w_in.astype(bf16),
              ffn2_w_out.astype(bf16), row(ln3_g), row(ln3_b)]
    gn = row(ret_gn_g)

    tab_p = _rope_tables(N_META + jnp.arange(S))
    (h_p, qr_p, kr_p, vr_p, gr_p, qm_p, km_p, vm_p, ckv_p, kpe_p) = _front_call(
        x_prompt.reshape(B * S, D), tab_p, front_w, FRONT_TM, S // FRONT_TM)
    n_s = DB * T
    x_small = jnp.concatenate([x_sample.reshape(n_s, D), meta_tokens.astype(x_sample.dtype)], axis=0)
    pos_small = jnp.concatenate([jnp.tile(N_META + P + jnp.arange(T), DB), jnp.arange(N_META)])
    (h_s, qr_s, kr_s, vr_s, gr_s, qm_s, km_s, vm_s, ckv_s, kpe_s) = _front_call(
        x_small, _rope_tables(pos_small), front_w, n_s + N_META, 1)
    kmeta, vmeta = km_s[n_s:], vm_s[n_s:]
    krmeta, vrmeta = kr_s[n_s:], vr_s[n_s:]

    mix_p, p_state = _mixer_call(qm_p, km_p, vm_p, kmeta, vmeta, qr_p, kr_p, vr_p, gr_p,
                                 krmeta, vrmeta, gn, B, S)
    ckpe = jnp.pad(cache_mla_kpe, ((0, 0), (0, 0), (0, LANES - MLA_ROPE)))
    mix_s, s_state = _smixer_call(qm_s[:n_s], km_s[:n_s], vm_s[:n_s], kmeta, vmeta, cache_mla_ckv, ckpe,
                                  w_ukv, qr_s[:n_s], kr_s[:n_s], vr_s[:n_s], gr_s[:n_s], gn, state_ret, T)

    y_p = _tail_call(mix_p, h_p, tail_w, FRONT_TM)
    y_s = _tail_call(mix_s, h_s[:n_s], tail_w, n_s)

    meta_ckv = jnp.broadcast_to(ckv_s[n_s:][None], (B, N_META, MLA_KV_LORA))
    meta_kpe = jnp.broadcast_to(kpe_s[n_s:, :MLA_ROPE][None], (B, N_META, MLA_ROPE))
    p_ckv = jnp.concatenate([meta_ckv, ckv_p.reshape(B, S, MLA_KV_LORA)], axis=1)
    p_kpe = jnp.concatenate([meta_kpe, kpe_p[:, :MLA_ROPE].reshape(B, S, MLA_ROPE)], axis=1)
    return (y_p.reshape(B, S, D), y_s.reshape(DB, T, D), p_ckv, p_kpe, p_state.astype(x_prompt.dtype),
            ckv_s[:n_s].reshape(DB, T, MLA_KV_LORA), kpe_s[:n_s, :MLA_ROPE].reshape(DB, T, MLA_ROPE),
            s_state.astype(state_ret.dtype))
```

```python
import functools
import math

import jax
import jax.numpy as jnp
from jax import lax
from jax.experimental import pallas as pl
from jax.experimental.pallas import tpu as pltpu

D_MODEL = 1024
DEPTH = 1
CHUNK = 64
N_META = 16
RET_HEADS = 4
RET_DK = 128
RET_DV = 128
MLA_HEADS = 4
MLA_NOPE = 128
MLA_ROPE = 64
MLA_V = 128
MLA_Q_LORA = 256
MLA_KV_LORA = 128
D_FF = 2816
ROPE_BASE = 10000.0
LN_EPS = 1e-5
RMS_EPS = 1e-6
DN_ALPHA = (2 * DEPTH) ** 0.25

OFF_KR = RET_HEADS * RET_DK
OFF_VR = 2 * RET_HEADS * RET_DK
OFF_GR = OFF_VR + RET_HEADS * RET_DV
OFF_CQ = OFF_GR + RET_HEADS * RET_DV
OFF_CKV = OFF_CQ + MLA_Q_LORA
OFF_KPE = OFF_CKV + MLA_KV_LORA
D_IN = OFF_KPE + MLA_ROPE

LANES = 128
D_IN_PAD = D_IN + (LANES - MLA_ROPE)
MLA_QK_PAD = 2 * LANES
RET_W = RET_HEADS * RET_DK
MLA_QW = MLA_HEADS * MLA_QK_PAD
MLA_VW = MLA_HEADS * MLA_V
TAB_W = 5 * LANES

FRONT_TM = 256
ATT_QB = FRONT_TM
ATT_TQ = 2 * ATT_QB
ATT_TK = 2 * ATT_QB
RET_C = 256
VMEM_LIMIT = 56 * 1024 * 1024

MLA_SCALE = (MLA_NOPE + MLA_ROPE) ** -0.5
MLA_SCALE_LOG2E = MLA_SCALE * math.log2(math.e)
RET_LOG_GAMMA = tuple(math.log(1.0 - 2.0 ** (-5.0 - h)) for h in range(RET_HEADS))

f32 = jnp.float32
bf16 = jnp.bfloat16


def _dot(a, b):
    return jnp.dot(a, b, preferred_element_type=f32)


def _dot_nt(a, b):
    return lax.dot_general(a, b, (((1,), (1,)), ((), ())), preferred_element_type=f32)


def _dot_tn(a, b):
    return lax.dot_general(a, b, (((0,), (0,)), ((), ())), preferred_element_type=f32)


def _layer_norm(x, g, b):
    mu = jnp.mean(x, axis=-1, keepdims=True)
    xc = x - mu
    var = jnp.mean(xc * xc, axis=-1, keepdims=True)
    return xc * lax.rsqrt(var + LN_EPS) * g + b


def _rms_norm(x, g):
    return x * lax.rsqrt(jnp.mean(x * x, axis=-1, keepdims=True) + RMS_EPS) * g


def _silu(x):
    return x / (1.0 + jnp.exp(-x))


def _ffn_ln(x, w_in_ref, w_out_ref, g_ref, b_ref):
    xb = x.astype(bf16)
    hg = _dot(xb, w_in_ref[:, :D_FF])
    hu = _dot(xb, w_in_ref[:, D_FF:])
    a = (_silu(hg) * hu).astype(bf16)
    y = _dot(a, w_out_ref[...])
    return _layer_norm(DN_ALPHA * x + 0.5 * y, g_ref[...], b_ref[...])


def _front_kernel(transposed, x_ref, tab_ref, *refs):
    if transposed:
        tab_t_ref, refs = refs[0], refs[1:]
    (w_in_ref, w_out_ref, ln_g_ref, ln_b_ref, w_mix_ref, qn_g_ref, w_uq_ref, kvn_g_ref, w_uk_ref, w_uv_ref,
     h_ref, qr_ref, kr_ref, vr_ref, gr_ref, qm_ref, km_ref, vm_ref, ckv_ref, kpe_ref) = refs
    h = _ffn_ln(x_ref[...], w_in_ref, w_out_ref, ln_g_ref, ln_b_ref)
    h_ref[...] = h
    p = _dot(h.astype(bf16), w_mix_ref[...])

    c_r = tab_ref[:, 0 * LANES:1 * LANES]
    s_r = tab_ref[:, 1 * LANES:2 * LANES]
    c_m = tab_ref[:, 2 * LANES:3 * LANES]
    s_ma = tab_ref[:, 3 * LANES:4 * LANES]
    s_mb = tab_ref[:, 4 * LANES:5 * LANES]

    def rope_ret(xh):
        return xh * c_r + pltpu.roll(xh, 64, 1) * s_r

    def rope_mla(xh):
        return xh * c_m + pltpu.roll(xh, 96, 1) * s_ma + pltpu.roll(xh, 32, 1) * s_mb

    for hd in range(RET_HEADS):
        lo = hd * RET_DK
        qr_ref[:, lo:lo + RET_DK] = rope_ret(p[:, lo:lo + RET_DK]).astype(bf16)
        kr_ref[:, lo:lo + RET_DK] = (rope_ret(p[:, OFF_KR + lo:OFF_KR + lo + RET_DK])
                                     * RET_DK ** -0.5).astype(bf16)
    vr_ref[...] = p[:, OFF_VR:OFF_GR].astype(bf16)
    gr_ref[...] = p[:, OFF_GR:OFF_CQ]

    cq = _rms_norm(p[:, OFF_CQ:OFF_CKV], qn_g_ref[...]).astype(bf16)
    ckv = _rms_norm(p[:, OFF_CKV:OFF_KPE], kvn_g_ref[...])
    ckv_ref[...] = ckv
    ckv_b = ckv.astype(bf16)
    kpe = rope_mla(p[:, OFF_KPE:D_IN_PAD])
    kpe_ref[...] = kpe
    kpe_b = kpe.astype(bf16)
    k_nope = _dot(ckv_b, w_uk_ref[...])
    for hd in range(MLA_HEADS):
        lo = hd * MLA_QK_PAD
        km_ref[:, lo:lo + LANES] = k_nope[:, hd * MLA_NOPE:(hd + 1) * MLA_NOPE].astype(bf16)
        km_ref[:, lo + LANES:lo + 2 * LANES] = kpe_b

    if transposed:
        half = MLA_ROPE // 2
        c_t = tab_t_ref[:half, :]
        s_t = tab_t_ref[half:, :]
        q_t = _dot_nt(w_uq_ref[...], cq)
        for hd in range(MLA_HEADS):
            lo = hd * MLA_QK_PAD
            r0 = lo + MLA_NOPE
            x1 = q_t[r0:r0 + half, :]
            x2 = q_t[r0 + half:r0 + 2 * half, :]
            qm_ref[0, lo:r0, :] = q_t[lo:r0, :].astype(bf16)
            qm_ref[0, r0:r0 + half, :] = (x1 * c_t - x2 * s_t).astype(bf16)
            qm_ref[0, r0 + half:r0 + 2 * half, :] = (x1 * s_t + x2 * c_t).astype(bf16)
            qm_ref[0, r0 + 2 * half:lo + MLA_QK_PAD, :] = q_t[r0 + 2 * half:lo + MLA_QK_PAD, :].astype(bf16)
        vm_ref[0] = _dot_nt(w_uv_ref[...], ckv_b).astype(bf16)
    else:
        q = _dot(cq, w_uq_ref[...])
        for hd in range(MLA_HEADS):
            lo = hd * MLA_QK_PAD
            qm_ref[:, lo:lo + LANES] = q[:, lo:lo + LANES].astype(bf16)
            qm_ref[:, lo + LANES:lo + 2 * LANES] = rope_mla(q[:, lo + LANES:lo + 2 * LANES]).astype(bf16)
        vm_ref[...] = _dot(ckv_b, w_uv_ref[...]).astype(bf16)


def _const_spec(shape):
    nd = len(shape)
    return pl.BlockSpec(shape, lambda *_: (0,) * nd, pipeline_mode=pl.Buffered(1))


def _front_call(x, tab, tab_t, wts, tm, tab_blocks):
    n = x.shape[0]
    transposed = tab_t is not None
    row = lambda w: pl.BlockSpec((tm, w), lambda i: (i, 0))
    col = lambda w: pl.BlockSpec((1, w, tm), lambda i: (i, 0, 0))
    in_specs = [row(D_MODEL), pl.BlockSpec((tm, TAB_W), lambda i: (i % tab_blocks, 0))]
    args = [x, tab]
    if transposed:
        in_specs.append(pl.BlockSpec((MLA_ROPE, tm), lambda i: (0, i % tab_blocks)))
        args.append(tab_t)
    in_specs += [_const_spec(w.shape) for w in wts]
    rows = lambda w, dt: (row(w), jax.ShapeDtypeStruct((n, w), dt))
    cols = lambda w, dt: (col(w), jax.ShapeDtypeStruct((n // tm, w, tm), dt))
    qv = cols if transposed else rows
    outs = [rows(D_MODEL, f32), rows(RET_W, bf16), rows(RET_W, bf16), rows(RET_W, bf16), rows(RET_W, f32),
            qv(MLA_QW, bf16), rows(MLA_QW, bf16), qv(MLA_VW, bf16), rows(MLA_KV_LORA, f32), rows(LANES, f32)]
    return pl.pallas_call(
        functools.partial(_front_kernel, transposed),
        grid=(n // tm,),
        in_specs=in_specs,
        out_specs=[o[0] for o in outs],
        out_shape=[o[1] for o in outs],
        compiler_params=pltpu.CompilerParams(dimension_semantics=("arbitrary",),
                                             vmem_limit_bytes=VMEM_LIMIT),
        name="front_t" if transposed else "front",
    )(*args, *wts)


def _decay_mask(c, lg):
    r = lax.broadcasted_iota(jnp.int32, (c, c), 0)
    k = lax.broadcasted_iota(jnp.int32, (c, c), 1)
    rel = (r - k).astype(f32)
    return jnp.where(rel >= 0.0, jnp.exp(lg * jnp.maximum(rel, 0.0)), 0.0)


def _row_pow(c, lg, offset, sign):
    i = lax.broadcasted_iota(jnp.int32, (c, LANES), 0).astype(f32)
    return jnp.exp(lg * (offset + sign * i))


def _retention_chunk(q, k, v, s_prev, lg):
    c = q.shape[0]
    scores = _dot_nt(q, k) * _decay_mask(c, lg)
    inner = _dot(scores.astype(bf16), v)
    cross = _dot(q, s_prev.astype(bf16)) * _row_pow(c, lg, 1.0, 1.0)
    kd = (k.astype(f32) * _row_pow(c, lg, c - 1.0, -1.0)).astype(bf16)
    s_new = math.exp(lg * c) * s_prev + _dot_tn(kd, v)
    return inner + cross, s_new


def _retention_out(o, g, gn):
    mu = jnp.mean(o, axis=-1, keepdims=True)
    oc = o - mu
    var = jnp.mean(oc * oc, axis=-1, keepdims=True)
    return oc * lax.rsqrt(var + LN_EPS) * gn * _silu(g)


def _att_update(m_ref, l_ref, acc_ref, u, s, v_t, ok):
    s = s * MLA_SCALE_LOG2E
    if ok is not None:
        s = jnp.where(ok, s, -jnp.inf)
    m = m_ref[u]
    m_new = jnp.maximum(m, jnp.max(s, axis=0, keepdims=True))
    p = jnp.exp2(s - m_new)
    alpha = jnp.exp2(m - m_new)
    m_ref[u] = m_new
    l_ref[u] = alpha * l_ref[u] + jnp.sum(p, axis=0, keepdims=True)
    acc_ref[u] = alpha * acc_ref[u] + _dot(v_t, p.astype(bf16))


def _mixer_kernel(qt_ref, km_ref, vt_ref, kmeta_ref, vmeta_ref, qr_ref, kr_ref, vr_ref, gr_ref,
                  krmeta_ref, vrmeta_ref, gn_ref, mix_ref, state_ref, m_ref, l_ref, acc_ref, s_ref):
    qi = pl.program_id(1)

    @pl.when(qi == 0)
    def _():
        for hd in range(RET_HEADS):
            lo = hd * RET_DK
            lg = RET_LOG_GAMMA[hd]
            kd = (krmeta_ref[:, lo:lo + RET_DK].astype(f32)
                  * _row_pow(N_META, lg, N_META - 1.0, -1.0)).astype(bf16)
            state_ref[0, hd] = _dot_tn(kd, vrmeta_ref[:, lo:lo + RET_DV])

    for hd in range(RET_HEADS):
        lo = hd * RET_DK
        lg = RET_LOG_GAMMA[hd]
        s = state_ref[0, hd]
        for c in range(ATT_TQ // RET_C):
            rows = slice(c * RET_C, (c + 1) * RET_C)
            o, s = _retention_chunk(qr_ref[rows, lo:lo + RET_DK], kr_ref[rows, lo:lo + RET_DK],
                                    vr_ref[rows, lo:lo + RET_DV], s, lg)
            y = _retention_out(o, gr_ref[rows, lo:lo + RET_DV], gn_ref[:, lo:lo + RET_DV])
            mix_ref[rows, lo:lo + RET_DV] = y.astype(bf16)
        state_ref[0, hd] = s

    n_qb = ATT_TQ // ATT_QB
    units = [(hd, qb) for hd in range(MLA_HEADS) for qb in range(n_qb)]

    def q_block(hd, qb):
        return qt_ref[qb, hd * MLA_QK_PAD:(hd + 1) * MLA_QK_PAD, :]

    def v_tile(hd, blk0, n_blk):
        rows = slice(hd * MLA_V, (hd + 1) * MLA_V)
        return jnp.concatenate([vt_ref[blk0 + j, rows, :] for j in range(n_blk)], axis=1)

    def scores(start, n_keys, u):
        hd, qb = units[u]
        k = km_ref[0, pl.ds(start, n_keys), hd * MLA_QK_PAD:(hd + 1) * MLA_QK_PAD]
        return _dot(k, q_block(hd, qb))

    s_ref[...] = scores(0, ATT_TK, 0)

    for u, (hd, qb) in enumerate(units):
        s0 = _dot(kmeta_ref[:, hd * MLA_QK_PAD:(hd + 1) * MLA_QK_PAD], q_block(hd, qb)) * MLA_SCALE_LOG2E
        m = jnp.max(s0, axis=0, keepdims=True)
        p0 = jnp.exp2(s0 - m)
        m_ref[u] = m
        l_ref[u] = jnp.sum(p0, axis=0, keepdims=True)
        acc_ref[u] = _dot_tn(vmeta_ref[:, hd * MLA_V:(hd + 1) * MLA_V], p0.astype(bf16))

    def full_step(kt, _):
        start = pl.multiple_of(kt * ATT_TK, ATT_TK)
        s_cur = s_ref[...]
        for u, (hd, qb) in enumerate(units):
            if u + 1 < len(units):
                s_next = scores(start, ATT_TK, u + 1)
            else:
                s_next = scores(pl.multiple_of(start + ATT_TK, ATT_TK), ATT_TK, 0)
            _att_update(m_ref, l_ref, acc_ref, u, s_cur,
                        v_tile(hd, kt * (ATT_TK // ATT_QB), ATT_TK // ATT_QB), None)
            s_cur = s_next
        s_ref[...] = s_cur
        return 0

    lax.fori_loop(0, qi, full_step, 0)

    start = pl.multiple_of(qi * ATT_TQ, ATT_TQ)
    s_cur = s_ref[...]
    for u, (hd, qb) in enumerate(units):
        if u + 1 < len(units):
            s_next = scores(start, (units[u + 1][1] + 1) * ATT_QB, u + 1)
        k_chunk = lax.broadcasted_iota(jnp.int32, s_cur.shape, 0) // CHUNK
        q_chunk = lax.broadcasted_iota(jnp.int32, s_cur.shape, 1) // CHUNK + qb * (ATT_QB // CHUNK)
        _att_update(m_ref, l_ref, acc_ref, u, s_cur, v_tile(hd, qi * n_qb, s_cur.shape[0] // ATT_QB),
                    k_chunk <= q_chunk)
        out = (acc_ref[u] / l_ref[u]).T
        mix_ref[qb * ATT_QB:(qb + 1) * ATT_QB, RET_W + hd * MLA_V:RET_W + (hd + 1) * MLA_V] = out.astype(bf16)
        s_cur = s_next


def _mixer_call(qt, km, vt, kmeta, vmeta, qr, kr, vr, gr, krmeta, vrmeta, gn, batch, seq):
    nq = seq // ATT_TQ
    n_qb = ATT_TQ // ATT_QB
    tile = lambda w: pl.BlockSpec((ATT_TQ, w), lambda b, i: (b * nq + i, 0))
    const = lambda a: pl.BlockSpec(a.shape, lambda b, i: (0,) * a.ndim)
    return pl.pallas_call(
        _mixer_kernel,
        grid=(batch, nq),
        in_specs=[pl.BlockSpec((n_qb, MLA_QW, ATT_QB), lambda b, i: (b * nq + i, 0, 0)),
                  pl.BlockSpec((1, seq, MLA_QW), lambda b, i: (b, 0, 0)),
                  pl.BlockSpec((seq // ATT_QB, MLA_VW, ATT_QB), lambda b, i: (b, 0, 0)),
                  const(kmeta), const(vmeta),
                  tile(RET_W), tile(RET_W), tile(RET_W), tile(RET_W),
                  const(krmeta), const(vrmeta), const(gn)],
        out_specs=[tile(D_MODEL),
                   pl.BlockSpec((1, RET_HEADS, RET_DK, RET_DV), lambda b, i: (b, 0, 0, 0))],
        out_shape=[jax.ShapeDtypeStruct((batch * seq, D_MODEL), bf16),
                   jax.ShapeDtypeStruct((batch, RET_HEADS, RET_DK, RET_DV), f32)],
        scratch_shapes=[pltpu.VMEM((MLA_HEADS * n_qb, 1, ATT_QB), f32),
                        pltpu.VMEM((MLA_HEADS * n_qb, 1, ATT_QB), f32),
                        pltpu.VMEM((MLA_HEADS * n_qb, MLA_V, ATT_QB), f32),
                        pltpu.VMEM((ATT_TK, ATT_QB), f32)],
        compiler_params=pltpu.CompilerParams(dimension_semantics=("arbitrary", "arbitrary"),
                                             vmem_limit_bytes=VMEM_LIMIT),
        name="mixer",
    )(qt, km.reshape(batch, seq, MLA_QW), vt, kmeta, vmeta, qr, kr, vr, gr, krmeta, vrmeta, gn)


def _smixer_kernel(qm_ref, knew_ref, vnew_ref, kmeta_ref, vmeta_ref, cckv_ref, ckpe_ref, w_ukv_ref,
                   qr_ref, kr_ref, vr_ref, gr_ref, gn_ref, s0_ref, mix_ref, state_ref):
    for hd in range(RET_HEADS):
        lo = hd * RET_DK
        o, s = _retention_chunk(qr_ref[:, lo:lo + RET_DK], kr_ref[:, lo:lo + RET_DK],
                                vr_ref[:, lo:lo + RET_DV], s0_ref[0, hd], RET_LOG_GAMMA[hd])
        state_ref[0, hd] = s
        y = _retention_out(o, gr_ref[:, lo:lo + RET_DV], gn_ref[:, lo:lo + RET_DV])
        mix_ref[:, lo:lo + RET_DV] = y.astype(bf16)

    kv_c = _dot(cckv_ref[0].astype(bf16), w_ukv_ref[...])
    kpe_c = ckpe_ref[0].astype(bf16)
    for hd in range(MLA_HEADS):
        klo = hd * MLA_QK_PAD
        vlo = hd * MLA_V
        q = qm_ref[:, klo:klo + MLA_QK_PAD]
        s_m = _dot_nt(q, kmeta_ref[:, klo:klo + MLA_QK_PAD]) * MLA_SCALE
        s_c = (_dot_nt(q[:, :LANES], kv_c[:, hd * MLA_NOPE:(hd + 1) * MLA_NOPE].astype(bf16))
               + _dot_nt(q[:, LANES:], kpe_c)) * MLA_SCALE
        s_n = _dot_nt(q, knew_ref[:, klo:klo + MLA_QK_PAD]) * MLA_SCALE
        m = jnp.maximum(jnp.maximum(jnp.max(s_m, axis=-1, keepdims=True),
                                    jnp.max(s_c, axis=-1, keepdims=True)),
                        jnp.max(s_n, axis=-1, keepdims=True))
        p_m = jnp.exp(s_m - m)
        p_c = jnp.exp(s_c - m)
        p_n = jnp.exp(s_n - m)
        l = (jnp.sum(p_m, axis=-1, keepdims=True) + jnp.sum(p_c, axis=-1, keepdims=True)
             + jnp.sum(p_n, axis=-1, keepdims=True))
        v_c = kv_c[:, MLA_HEADS * MLA_NOPE + vlo:MLA_HEADS * MLA_NOPE + vlo + MLA_V].astype(bf16)
        acc = (_dot(p_m.astype(bf16), vmeta_ref[:, vlo:vlo + MLA_V]) + _dot(p_c.astype(bf16), v_c)
               + _dot(p_n.astype(bf16), vnew_ref[:, vlo:vlo + MLA_V]))
        mix_ref[:, RET_W + vlo:RET_W + vlo + MLA_V] = (acc / l).astype(bf16)


def _smixer_call(qm, knew, vnew, kmeta, vmeta, cckv, ckpe, w_ukv, qr, kr, vr, gr, gn, s0, t):
    db, past = cckv.shape[0], cckv.shape[1]
    tile = lambda w: pl.BlockSpec((t, w), lambda b: (b, 0))
    const = lambda a: pl.BlockSpec(a.shape, lambda b: (0,) * a.ndim)
    per = lambda *tail: pl.BlockSpec((1,) + tail, lambda b: (b,) + (0,) * len(tail))
    return pl.pallas_call(
        _smixer_kernel,
        grid=(db,),
        in_specs=[tile(MLA_QW), tile(MLA_QW), tile(MLA_VW), const(kmeta), const(vmeta),
                  per(past, MLA_KV_LORA), per(past, LANES), const(w_ukv),
                  tile(RET_W), tile(RET_W), tile(RET_W), tile(RET_W), const(gn),
                  per(RET_HEADS, RET_DK, RET_DV)],
        out_specs=[tile(D_MODEL), per(RET_HEADS, RET_DK, RET_DV)],
        out_shape=[jax.ShapeDtypeStruct((db * t, D_MODEL), bf16),
                   jax.ShapeDtypeStruct((db, RET_HEADS, RET_DK, RET_DV), f32)],
        compiler_params=pltpu.CompilerParams(dimension_semantics=("arbitrary",),
                                             vmem_limit_bytes=VMEM_LIMIT),
        name="smixer",
    )(qm, knew, vnew, kmeta, vmeta, cckv, ckpe, w_ukv, qr, kr, vr, gr, gn, s0)


def _tail_kernel(mix_ref, h_ref, w_mo_ref, ln2_g_ref, ln2_b_ref, w_in_ref, w_out_ref, ln3_g_ref, ln3_b_ref,
                 y_ref):
    mix = _dot(mix_ref[...], w_mo_ref[...])
    h2 = _layer_norm(DN_ALPHA * h_ref[...] + mix, ln2_g_ref[...], ln2_b_ref[...])
    y_ref[...] = _ffn_ln(h2, w_in_ref, w_out_ref, ln3_g_ref, ln3_b_ref)


def _tail_call(mix, h, wts, tm):
    n = mix.shape[0]
    row = lambda w: pl.BlockSpec((tm, w), lambda i: (i, 0))
    return pl.pallas_call(
        _tail_kernel,
        grid=(n // tm,),
        in_specs=[row(D_MODEL), row(D_MODEL)] + [_const_spec(w.shape) for w in wts],
        out_specs=row(D_MODEL),
        out_shape=jax.ShapeDtypeStruct((n, D_MODEL), f32),
        compiler_params=pltpu.CompilerParams(dimension_semantics=("arbitrary",),
                                             vmem_limit_bytes=VMEM_LIMIT),
        name="tail",
    )(mix, h, *wts)


def _rope_tables(pos):
    posf = pos.astype(f32)[:, None]
    inv_r = ROPE_BASE ** (-jnp.arange(0, RET_DK, 2, dtype=f32) / RET_DK)
    ang = posf * inv_r[None, :]
    c, s = jnp.cos(ang), jnp.sin(ang)
    inv_m = ROPE_BASE ** (-jnp.arange(0, MLA_ROPE, 2, dtype=f32) / MLA_ROPE)
    angm = posf * inv_m[None, :]
    cm, sm = jnp.cos(angm), jnp.sin(angm)
    z = jnp.zeros_like(cm)
    z2 = jnp.zeros_like(c)
    tab = jnp.concatenate([c, c, -s, s, cm, cm, z2, -sm, z, z2, z, sm, z2], axis=1)
    return tab, jnp.concatenate([cm, sm], axis=1).T


def kernel(x_prompt, x_sample, cache_mla_ckv, cache_mla_kpe, state_ret, meta_tokens,
           ffn1_w_in, ffn1_w_out, ln1_g, ln1_b, w_mix_in, ret_gn_g, mla_q_norm_g, mla_w_uq,
           mla_kv_norm_g, mla_w_ukv, w_mix_out, ln2_g, ln2_b, ffn2_w_in, ffn2_w_out, ln3_g, ln3_b):
    B, S, D = x_prompt.shape
    DB, T, _ = x_sample.shape
    P = cache_mla_ckv.shape[1]
    assert S % ATT_TQ == 0 and S % FRONT_TM == 0 and ATT_TQ % RET_C == 0
    row = lambda a: a.reshape(1, -1).astype(f32)

    w_mix = jnp.pad(w_mix_in, ((0, 0), (0, D_IN_PAD - D_IN))).astype(bf16)
    w_uq = mla_w_uq.reshape(MLA_Q_LORA, MLA_HEADS, MLA_NOPE + MLA_ROPE)
    w_uq = jnp.pad(w_uq, ((0, 0), (0, 0), (0, MLA_QK_PAD - MLA_NOPE - MLA_ROPE)))
    w_uq = w_uq.reshape(MLA_Q_LORA, MLA_QW).astype(bf16)
    w_ukv = mla_w_ukv.reshape(MLA_KV_LORA, MLA_HEADS, MLA_NOPE + MLA_V)
    w_uk = w_ukv[:, :, :MLA_NOPE].reshape(MLA_KV_LORA, -1).astype(bf16)
    w_uv = w_ukv[:, :, MLA_NOPE:].reshape(MLA_KV_LORA, -1).astype(bf16)
    shared_w = [ffn1_w_in.astype(bf16), ffn1_w_out.astype(bf16), row(ln1_g), row(ln1_b), w_mix,
                row(mla_q_norm_g)]
    front_w = shared_w + [w_uq, row(mla_kv_norm_g), w_uk, w_uv]
    front_t_w = shared_w + [w_uq.T, row(mla_kv_norm_g), w_uk, w_uv.T]
    tail_w = [w_mix_out.astype(bf16), row(ln2_g), row(ln2_b), ffn2_w_in.astype(bf16),
              ffn2_w_out.astype(bf16), row(ln3_g), row(ln3_b)]
    gn = row(ret_gn_g)

    tab_p, tab_t_p = _rope_tables(N_META + jnp.arange(S))
    (h_p, qr_p, kr_p, vr_p, gr_p, qt_p, km_p, vt_p, ckv_p, kpe_p) = _front_call(
        x_prompt.reshape(B * S, D), tab_p, tab_t_p, front_t_w, FRONT_TM, S // FRONT_TM)
    n_s = DB * T
    x_small = jnp.concatenate([x_sample.reshape(n_s, D), meta_tokens.astype(x_sample.dtype)], axis=0)
    pos_small = jnp.concatenate([jnp.tile(N_META + P + jnp.arange(T), DB), jnp.arange(N_META)])
    (h_s, qr_s, kr_s, vr_s, gr_s, qm_s, km_s, vm_s, ckv_s, kpe_s) = _front_call(
        x_small, _rope_tables(pos_small)[0], None, front_w, n_s + N_META, 1)
    kmeta, vmeta = km_s[n_s:], vm_s[n_s:]
    krmeta, vrmeta = kr_s[n_s:], vr_s[n_s:]

    mix_p, p_state = _mixer_call(qt_p, km_p, vt_p, kmeta, vmeta, qr_p, kr_p, vr_p, gr_p,
                                 krmeta, vrmeta, gn, B, S)
    ckpe = jnp.pad(cache_mla_kpe, ((0, 0), (0, 0), (0, LANES - MLA_ROPE)))
    mix_s, s_state = _smixer_call(qm_s[:n_s], km_s[:n_s], vm_s[:n_s], kmeta, vmeta, cache_mla_ckv, ckpe,
                                  jnp.concatenate([w_uk, w_uv], axis=1), qr_s[:n_s], kr_s[:n_s], vr_s[:n_s],
                                  gr_s[:n_s], gn, state_ret, T)

    y_p = _tail_call(mix_p, h_p, tail_w, FRONT_TM)
    y_s = _tail_call(mix_s, h_s[:n_s], tail_w, n_s)

    meta_ckv = jnp.broadcast_to(ckv_s[n_s:][None], (B, N_META, MLA_KV_LORA))
    meta_kpe = jnp.broadcast_to(kpe_s[n_s:, :MLA_ROPE][None], (B, N_META, MLA_ROPE))
    p_ckv = jnp.concatenate([meta_ckv, ckv_p.reshape(B, S, MLA_KV_LORA)], axis=1)
    p_kpe = jnp.concatenate([meta_kpe, kpe_p[:, :MLA_ROPE].reshape(B, S, MLA_ROPE)], axis=1)
    return (y_p.reshape(B, S, D), y_s.reshape(DB, T, D), p_ckv, p_kpe, p_state.astype(x_prompt.dtype),
            ckv_s[:n_s].reshape(DB, T, MLA_KV_LORA), kpe_s[:n_s, :MLA_ROPE].reshape(DB, T, MLA_ROPE),
            s_state.astype(state_ret.dtype))
```

```python
import functools
import math

import jax
import jax.numpy as jnp
from jax import lax
from jax.experimental import pallas as pl
from jax.experimental.pallas import tpu as pltpu

D_MODEL = 1024
DEPTH = 1
CHUNK = 64
N_META = 16
RET_HEADS = 4
RET_DK = 128
RET_DV = 128
MLA_HEADS = 4
MLA_NOPE = 128
MLA_ROPE = 64
MLA_V = 128
MLA_Q_LORA = 256
MLA_KV_LORA = 128
D_FF = 2816
ROPE_BASE = 10000.0
LN_EPS = 1e-5
RMS_EPS = 1e-6
DN_ALPHA = (2 * DEPTH) ** 0.25

OFF_KR = RET_HEADS * RET_DK
OFF_VR = 2 * RET_HEADS * RET_DK
OFF_GR = OFF_VR + RET_HEADS * RET_DV
OFF_CQ = OFF_GR + RET_HEADS * RET_DV
OFF_CKV = OFF_CQ + MLA_Q_LORA
OFF_KPE = OFF_CKV + MLA_KV_LORA
D_IN = OFF_KPE + MLA_ROPE

LANES = 128
D_IN_PAD = D_IN + (LANES - MLA_ROPE)
MLA_QK_PAD = 2 * LANES
RET_W = RET_HEADS * RET_DK
MLA_QW = MLA_HEADS * MLA_QK_PAD
MLA_VW = MLA_HEADS * MLA_V
TAB_W = 5 * LANES

FRONT_SUB = 256
FRONT_TM = 2 * FRONT_SUB
ATT_QB = FRONT_SUB
ATT_TQ = 2 * ATT_QB
ATT_TK = 2 * ATT_QB
RET_C = 256
VMEM_LIMIT = 56 * 1024 * 1024

MLA_SCALE = (MLA_NOPE + MLA_ROPE) ** -0.5
MLA_SCALE_LOG2E = MLA_SCALE * math.log2(math.e)
RET_LOG_GAMMA = tuple(math.log(1.0 - 2.0 ** (-5.0 - h)) for h in range(RET_HEADS))

f32 = jnp.float32
bf16 = jnp.bfloat16


def _dot(a, b):
    return jnp.dot(a, b, preferred_element_type=f32)


def _dot_nt(a, b):
    return lax.dot_general(a, b, (((1,), (1,)), ((), ())), preferred_element_type=f32)


def _dot_tn(a, b):
    return lax.dot_general(a, b, (((0,), (0,)), ((), ())), preferred_element_type=f32)


def _layer_norm(x, g, b):
    mu = jnp.mean(x, axis=-1, keepdims=True)
    xc = x - mu
    var = jnp.mean(xc * xc, axis=-1, keepdims=True)
    return xc * lax.rsqrt(var + LN_EPS) * g + b


def _rms_norm(x, g):
    return x * lax.rsqrt(jnp.mean(x * x, axis=-1, keepdims=True) + RMS_EPS) * g


def _silu(x):
    return x / (1.0 + jnp.exp(-x))


def _swiglu(x, w_in_ref, w_out_ref):
    xb = x.astype(bf16)
    hg = _dot(xb, w_in_ref[:, :D_FF])
    hu = _dot(xb, w_in_ref[:, D_FF:])
    return _dot((_silu(hg) * hu).astype(bf16), w_out_ref[...])


def _ffn_norm(x, y, g_ref, b_ref):
    return _layer_norm(DN_ALPHA * x + 0.5 * y, g_ref[...], b_ref[...])


def _front_kernel(transposed, sub, x_ref, tab_ref, *refs):
    if transposed:
        tab_t_ref, refs = refs[0], refs[1:]
    (w_in_ref, w_out_ref, ln_g_ref, ln_b_ref, w_mix_ref, gn_ref, qn_g_ref, w_uq_ref, kvn_g_ref, w_uk_ref, w_uv_ref,
     h_ref, qr_ref, kr_ref, vr_ref, gate_ref, qm_ref, km_ref, vm_ref, ckv_ref, kpe_ref) = refs
    n_sub = x_ref.shape[0] // sub
    tiles = [slice(j * sub, (j + 1) * sub) for j in range(n_sub)]

    xs = [x_ref[r, :] for r in tiles]
    ys = [_swiglu(x, w_in_ref, w_out_ref) for x in xs]
    ps = []
    for r, x, y in zip(tiles, xs, ys):
        h = _ffn_norm(x, y, ln_g_ref, ln_b_ref)
        h_ref[r, :] = h
        ps.append(_dot(h.astype(bf16), w_mix_ref[...]))

    for j, (r, p) in enumerate(zip(tiles, ps)):
        c_r = tab_ref[r, 0 * LANES:1 * LANES]
        s_r = tab_ref[r, 1 * LANES:2 * LANES]
        c_m = tab_ref[r, 2 * LANES:3 * LANES]
        s_ma = tab_ref[r, 3 * LANES:4 * LANES]
        s_mb = tab_ref[r, 4 * LANES:5 * LANES]

        def rope_ret(xh):
            return xh * c_r + pltpu.roll(xh, 64, 1) * s_r

        def rope_mla(xh):
            return xh * c_m + pltpu.roll(xh, 96, 1) * s_ma + pltpu.roll(xh, 32, 1) * s_mb

        cq = _rms_norm(p[:, OFF_CQ:OFF_CKV], qn_g_ref[...]).astype(bf16)
        ckv = _rms_norm(p[:, OFF_CKV:OFF_KPE], kvn_g_ref[...])
        ckv_ref[r, :] = ckv
        ckv_b = ckv.astype(bf16)
        k_nope = _dot(ckv_b, w_uk_ref[...])
        if transposed:
            q_t = _dot_nt(w_uq_ref[...], cq)
            v_t = _dot_nt(w_uv_ref[...], ckv_b)
        else:
            q = _dot(cq, w_uq_ref[...])
            v = _dot(ckv_b, w_uv_ref[...])

        for hd in range(RET_HEADS):
            lo = hd * RET_DK
            qr_ref[r, lo:lo + RET_DK] = rope_ret(p[:, lo:lo + RET_DK]).astype(bf16)
            kr_ref[r, lo:lo + RET_DK] = (rope_ret(p[:, OFF_KR + lo:OFF_KR + lo + RET_DK])
                                         * RET_DK ** -0.5).astype(bf16)
        vr_ref[r, :] = p[:, OFF_VR:OFF_GR].astype(bf16)
        gate_ref[r, :] = gn_ref[...] * _silu(p[:, OFF_GR:OFF_CQ])
        kpe = rope_mla(p[:, OFF_KPE:D_IN_PAD])
        kpe_ref[r, :] = kpe
        kpe_b = kpe.astype(bf16)
        for hd in range(MLA_HEADS):
            lo = hd * MLA_QK_PAD
            km_ref[r, lo:lo + LANES] = k_nope[:, hd * MLA_NOPE:(hd + 1) * MLA_NOPE].astype(bf16)
            km_ref[r, lo + LANES:lo + 2 * LANES] = kpe_b

        if transposed:
            half = MLA_ROPE // 2
            c_t = tab_t_ref[:half, r]
            s_t = tab_t_ref[half:, r]
            for hd in range(MLA_HEADS):
                lo = hd * MLA_QK_PAD
                r0 = lo + MLA_NOPE
                x1 = q_t[r0:r0 + half, :]
                x2 = q_t[r0 + half:r0 + 2 * half, :]
                qm_ref[j, lo:r0, :] = q_t[lo:r0, :].astype(bf16)
                qm_ref[j, r0:r0 + half, :] = (x1 * c_t - x2 * s_t).astype(bf16)
                qm_ref[j, r0 + half:r0 + 2 * half, :] = (x1 * s_t + x2 * c_t).astype(bf16)
                qm_ref[j, r0 + 2 * half:lo + MLA_QK_PAD, :] = q_t[r0 + 2 * half:lo + MLA_QK_PAD, :].astype(bf16)
            vm_ref[j] = v_t.astype(bf16)
        else:
            for hd in range(MLA_HEADS):
                lo = hd * MLA_QK_PAD
                qm_ref[r, lo:lo + LANES] = q[:, lo:lo + LANES].astype(bf16)
                qm_ref[r, lo + LANES:lo + 2 * LANES] = rope_mla(q[:, lo + LANES:lo + 2 * LANES]).astype(bf16)
            vm_ref[r, :] = v.astype(bf16)


def _const_spec(shape):
    nd = len(shape)
    return pl.BlockSpec(shape, lambda *_: (0,) * nd, pipeline_mode=pl.Buffered(1))


def _front_call(x, tab, tab_t, wts, tm, sub, tab_blocks):
    n = x.shape[0]
    transposed = tab_t is not None
    row = lambda w: pl.BlockSpec((tm, w), lambda i: (i, 0))
    col = lambda w: pl.BlockSpec((tm // sub, w, sub), lambda i: (i, 0, 0))
    in_specs = [row(D_MODEL), pl.BlockSpec((tm, TAB_W), lambda i: (i % tab_blocks, 0))]
    args = [x, tab]
    if transposed:
        in_specs.append(pl.BlockSpec((MLA_ROPE, tm), lambda i: (0, i % tab_blocks)))
        args.append(tab_t)
    in_specs += [_const_spec(w.shape) for w in wts]
    rows = lambda w, dt: (row(w), jax.ShapeDtypeStruct((n, w), dt))
    cols = lambda w, dt: (col(w), jax.ShapeDtypeStruct((n // sub, w, sub), dt))
    qv = cols if transposed else rows
    outs = [rows(D_MODEL, f32), rows(RET_W, bf16), rows(RET_W, bf16), rows(RET_W, bf16), rows(RET_W, f32),
            qv(MLA_QW, bf16), rows(MLA_QW, bf16), qv(MLA_VW, bf16), rows(MLA_KV_LORA, f32), rows(LANES, f32)]
    return pl.pallas_call(
        functools.partial(_front_kernel, transposed, sub),
        grid=(n // tm,),
        in_specs=in_specs,
        out_specs=[o[0] for o in outs],
        out_shape=[o[1] for o in outs],
        compiler_params=pltpu.CompilerParams(dimension_semantics=("arbitrary",),
                                             vmem_limit_bytes=VMEM_LIMIT),
        name="front_t" if transposed else "front",
    )(*args, *wts)


def _decay_mask(c, lg):
    r = lax.broadcasted_iota(jnp.int32, (c, c), 0)
    k = lax.broadcasted_iota(jnp.int32, (c, c), 1)
    rel = (r - k).astype(f32)
    return jnp.where(rel >= 0.0, jnp.exp(lg * jnp.maximum(rel, 0.0)), 0.0)


def _row_pow(c, lg, offset, sign):
    i = lax.broadcasted_iota(jnp.int32, (c, LANES), 0).astype(f32)
    return jnp.exp(lg * (offset + sign * i))


def _retention_tables(c, hd):
    lg = RET_LOG_GAMMA[hd]
    return _decay_mask(c, lg), _row_pow(c, lg, 1.0, 1.0), _row_pow(c, lg, c - 1.0, -1.0)


def _retention_chunk(qr_ref, kr_ref, vr_ref, gate_ref, mix_ref, rows, states, tables):
    c = rows.stop - rows.start
    heads = range(RET_HEADS)
    cols = [slice(hd * RET_DK, (hd + 1) * RET_DK) for hd in heads]
    q = [qr_ref[rows, cols[hd]] for hd in heads]
    k = [kr_ref[rows, cols[hd]] for hd in heads]
    v = [vr_ref[rows, cols[hd]] for hd in heads]
    scores = [_dot_nt(q[hd], k[hd]) for hd in heads]
    cross = [_dot(q[hd], states[hd].astype(bf16)) for hd in heads]
    kv = [_dot_tn((k[hd].astype(f32) * tables[hd][2][...]).astype(bf16), v[hd]) for hd in heads]
    new_states = [math.exp(RET_LOG_GAMMA[hd] * c) * states[hd] + kv[hd] for hd in heads]
    inner = [_dot((scores[hd] * tables[hd][0][...]).astype(bf16), v[hd]) for hd in heads]
    for hd in heads:
        o = inner[hd] + cross[hd] * tables[hd][1][...]
        mu = jnp.mean(o, axis=-1, keepdims=True)
        oc = o - mu
        var = jnp.mean(oc * oc, axis=-1, keepdims=True)
        y = oc * lax.rsqrt(var + LN_EPS) * gate_ref[rows, cols[hd]]
        mix_ref[rows, cols[hd]] = y.astype(bf16)
    return new_states


def _att_update(m_ref, l_ref, acc_ref, u, s, v_t, ok):
    s = s * MLA_SCALE_LOG2E
    if ok is not None:
        s = jnp.where(ok, s, -jnp.inf)
    m = m_ref[u]
    m_new = jnp.maximum(m, jnp.max(s, axis=0, keepdims=True))
    p = jnp.exp2(s - m_new)
    alpha = jnp.exp2(m - m_new)
    m_ref[u] = m_new
    l_ref[u] = alpha * l_ref[u] + jnp.sum(p, axis=0, keepdims=True)
    acc_ref[u] = alpha * acc_ref[u] + _dot(v_t, p.astype(bf16))


def _mixer_kernel(qt_ref, km_ref, vt_ref, kmeta_ref, vmeta_ref, qr_ref, kr_ref, vr_ref, gate_ref,
                  krmeta_ref, vrmeta_ref, mix_ref, state_ref,
                  m_ref, l_ref, acc_ref, s_ref, dmask_ref, qdec_ref, kdec_ref):
    qi = pl.program_id(1)

    @pl.when((pl.program_id(0) == 0) & (qi == 0))
    def _():
        for hd in range(RET_HEADS):
            dmask_ref[hd], qdec_ref[hd], kdec_ref[hd] = _retention_tables(RET_C, hd)

    @pl.when(qi == 0)
    def _():
        for hd in range(RET_HEADS):
            lo = hd * RET_DK
            lg = RET_LOG_GAMMA[hd]
            kd = (krmeta_ref[:, lo:lo + RET_DK].astype(f32)
                  * _row_pow(N_META, lg, N_META - 1.0, -1.0)).astype(bf16)
            state_ref[0, hd] = _dot_tn(kd, vrmeta_ref[:, lo:lo + RET_DV])

    tables = [(dmask_ref.at[hd], qdec_ref.at[hd], kdec_ref.at[hd]) for hd in range(RET_HEADS)]
    states = [state_ref[0, hd] for hd in range(RET_HEADS)]
    for c in range(ATT_TQ // RET_C):
        states = _retention_chunk(qr_ref, kr_ref, vr_ref, gate_ref, mix_ref,
                                  slice(c * RET_C, (c + 1) * RET_C), states, tables)
    for hd in range(RET_HEADS):
        state_ref[0, hd] = states[hd]

    n_qb = ATT_TQ // ATT_QB
    units = [(hd, qb) for hd in range(MLA_HEADS) for qb in range(n_qb)]

    def q_block(hd, qb):
        return qt_ref[qb, hd * MLA_QK_PAD:(hd + 1) * MLA_QK_PAD, :]

    def v_tile(hd, blk0, n_blk):
        rows = slice(hd * MLA_V, (hd + 1) * MLA_V)
        return jnp.concatenate([vt_ref[blk0 + j, rows, :] for j in range(n_blk)], axis=1)

    def scores(start, n_keys, u):
        hd, qb = units[u]
        k = km_ref[0, pl.ds(start, n_keys), hd * MLA_QK_PAD:(hd + 1) * MLA_QK_PAD]
        return _dot(k, q_block(hd, qb))

    s_ref[...] = scores(0, ATT_TK, 0)

    s_meta = [_dot(kmeta_ref[:, hd * MLA_QK_PAD:(hd + 1) * MLA_QK_PAD], q_block(hd, qb)) for hd, qb in units]
    p_meta = []
    for u, s0 in enumerate(s_meta):
        s0 = s0 * MLA_SCALE_LOG2E
        m = jnp.max(s0, axis=0, keepdims=True)
        p0 = jnp.exp2(s0 - m)
        m_ref[u] = m
        l_ref[u] = jnp.sum(p0, axis=0, keepdims=True)
        p_meta.append(p0.astype(bf16))
    for u, (hd, qb) in enumerate(units):
        acc_ref[u] = _dot_tn(vmeta_ref[:, hd * MLA_V:(hd + 1) * MLA_V], p_meta[u])

    def full_step(kt, _):
        start = pl.multiple_of(kt * ATT_TK, ATT_TK)
        s_cur = s_ref[...]
        for u, (hd, qb) in enumerate(units):
            if u + 1 < len(units):
                s_next = scores(start, ATT_TK, u + 1)
            else:
                s_next = scores(pl.multiple_of(start + ATT_TK, ATT_TK), ATT_TK, 0)
            _att_update(m_ref, l_ref, acc_ref, u, s_cur,
                        v_tile(hd, kt * (ATT_TK // ATT_QB), ATT_TK // ATT_QB), None)
            s_cur = s_next
        s_ref[...] = s_cur
        return 0

    lax.fori_loop(0, qi, full_step, 0)

    start = pl.multiple_of(qi * ATT_TQ, ATT_TQ)
    s_cur = s_ref[...]
    for u, (hd, qb) in enumerate(units):
        if u + 1 < len(units):
            s_next = scores(start, (units[u + 1][1] + 1) * ATT_QB, u + 1)
        k_chunk = lax.broadcasted_iota(jnp.int32, s_cur.shape, 0) // CHUNK
        q_chunk = lax.broadcasted_iota(jnp.int32, s_cur.shape, 1) // CHUNK + qb * (ATT_QB // CHUNK)
        _att_update(m_ref, l_ref, acc_ref, u, s_cur, v_tile(hd, qi * n_qb, s_cur.shape[0] // ATT_QB),
                    k_chunk <= q_chunk)
        out = (acc_ref[u] / l_ref[u]).T
        mix_ref[qb * ATT_QB:(qb + 1) * ATT_QB, RET_W + hd * MLA_V:RET_W + (hd + 1) * MLA_V] = out.astype(bf16)
        s_cur = s_next


def _mixer_call(qt, km, vt, kmeta, vmeta, qr, kr, vr, gate, krmeta, vrmeta, batch, seq):
    nq = seq // ATT_TQ
    n_qb = ATT_TQ // ATT_QB
    tile = lambda w: pl.BlockSpec((ATT_TQ, w), lambda b, i: (b * nq + i, 0))
    const = lambda a: pl.BlockSpec(a.shape, lambda b, i: (0,) * a.ndim)
    return pl.pallas_call(
        _mixer_kernel,
        grid=(batch, nq),
        in_specs=[pl.BlockSpec((n_qb, MLA_QW, ATT_QB), lambda b, i: (b * nq + i, 0, 0)),
                  pl.BlockSpec((1, seq, MLA_QW), lambda b, i: (b, 0, 0)),
                  pl.BlockSpec((seq // ATT_QB, MLA_VW, ATT_QB), lambda b, i: (b, 0, 0)),
                  const(kmeta), const(vmeta),
                  tile(RET_W), tile(RET_W), tile(RET_W), tile(RET_W),
                  const(krmeta), const(vrmeta)],
        out_specs=[tile(D_MODEL),
                   pl.BlockSpec((1, RET_HEADS, RET_DK, RET_DV), lambda b, i: (b, 0, 0, 0))],
        out_shape=[jax.ShapeDtypeStruct((batch * seq, D_MODEL), bf16),
                   jax.ShapeDtypeStruct((batch, RET_HEADS, RET_DK, RET_DV), f32)],
        scratch_shapes=[pltpu.VMEM((MLA_HEADS * n_qb, 1, ATT_QB), f32),
                        pltpu.VMEM((MLA_HEADS * n_qb, 1, ATT_QB), f32),
                        pltpu.VMEM((MLA_HEADS * n_qb, MLA_V, ATT_QB), f32),
                        pltpu.VMEM((ATT_TK, ATT_QB), f32),
                        pltpu.VMEM((RET_HEADS, RET_C, RET_C), f32),
                        pltpu.VMEM((RET_HEADS, RET_C, LANES), f32),
                        pltpu.VMEM((RET_HEADS, RET_C, LANES), f32)],
        compiler_params=pltpu.CompilerParams(dimension_semantics=("arbitrary", "arbitrary"),
                                             vmem_limit_bytes=VMEM_LIMIT),
        name="mixer",
    )(qt, km.reshape(batch, seq, MLA_QW), vt, kmeta, vmeta, qr, kr, vr, gate, krmeta, vrmeta)


def _smixer_kernel(qm_ref, knew_ref, vnew_ref, kmeta_ref, vmeta_ref, cckv_ref, ckpe_ref, w_ukv_ref,
                   qr_ref, kr_ref, vr_ref, gate_ref, s0_ref, mix_ref, state_ref):
    t = qr_ref.shape[0]
    states = _retention_chunk(qr_ref, kr_ref, vr_ref, gate_ref, mix_ref, slice(0, t),
                              [s0_ref[0, hd] for hd in range(RET_HEADS)],
                              [_retention_tables(t, hd) for hd in range(RET_HEADS)])
    for hd in range(RET_HEADS):
        state_ref[0, hd] = states[hd]

    kv_c = _dot(cckv_ref[0].astype(bf16), w_ukv_ref[...])
    kpe_c = ckpe_ref[0].astype(bf16)
    for hd in range(MLA_HEADS):
        klo = hd * MLA_QK_PAD
        vlo = hd * MLA_V
        q = qm_ref[:, klo:klo + MLA_QK_PAD]
        s_m = _dot_nt(q, kmeta_ref[:, klo:klo + MLA_QK_PAD]) * MLA_SCALE
        s_c = (_dot_nt(q[:, :LANES], kv_c[:, hd * MLA_NOPE:(hd + 1) * MLA_NOPE].astype(bf16))
               + _dot_nt(q[:, LANES:], kpe_c)) * MLA_SCALE
        s_n = _dot_nt(q, knew_ref[:, klo:klo + MLA_QK_PAD]) * MLA_SCALE
        m = jnp.maximum(jnp.maximum(jnp.max(s_m, axis=-1, keepdims=True),
                                    jnp.max(s_c, axis=-1, keepdims=True)),
                        jnp.max(s_n, axis=-1, keepdims=True))
        p_m = jnp.exp(s_m - m)
        p_c = jnp.exp(s_c - m)
        p_n = jnp.exp(s_n - m)
        l = (jnp.sum(p_m, axis=-1, keepdims=True) + jnp.sum(p_c, axis=-1, keepdims=True)
             + jnp.sum(p_n, axis=-1, keepdims=True))
        v_c = kv_c[:, MLA_HEADS * MLA_NOPE + vlo:MLA_HEADS * MLA_NOPE + vlo + MLA_V].astype(bf16)
        acc = (_dot(p_m.astype(bf16), vmeta_ref[:, vlo:vlo + MLA_V]) + _dot(p_c.astype(bf16), v_c)
               + _dot(p_n.astype(bf16), vnew_ref[:, vlo:vlo + MLA_V]))
        mix_ref[:, RET_W + vlo:RET_W + vlo + MLA_V] = (acc / l).astype(bf16)


def _smixer_call(qm, knew, vnew, kmeta, vmeta, cckv, ckpe, w_ukv, qr, kr, vr, gate, s0, t):
    db, past = cckv.shape[0], cckv.shape[1]
    tile = lambda w: pl.BlockSpec((t, w), lambda b: (b, 0))
    const = lambda a: pl.BlockSpec(a.shape, lambda b: (0,) * a.ndim)
    per = lambda *tail: pl.BlockSpec((1,) + tail, lambda b: (b,) + (0,) * len(tail))
    return pl.pallas_call(
        _smixer_kernel,
        grid=(db,),
        in_specs=[tile(MLA_QW), tile(MLA_QW), tile(MLA_VW), const(kmeta), const(vmeta),
                  per(past, MLA_KV_LORA), per(past, LANES), const(w_ukv),
                  tile(RET_W), tile(RET_W), tile(RET_W), tile(RET_W),
                  per(RET_HEADS, RET_DK, RET_DV)],
        out_specs=[tile(D_MODEL), per(RET_HEADS, RET_DK, RET_DV)],
        out_shape=[jax.ShapeDtypeStruct((db * t, D_MODEL), bf16),
                   jax.ShapeDtypeStruct((db, RET_HEADS, RET_DK, RET_DV), f32)],
        compiler_params=pltpu.CompilerParams(dimension_semantics=("arbitrary",),
                                             vmem_limit_bytes=VMEM_LIMIT),
        name="smixer",
    )(qm, knew, vnew, kmeta, vmeta, cckv, ckpe, w_ukv, qr, kr, vr, gate, s0)


def _tail_kernel(sub, mix_ref, h_ref, w_mo_ref, ln2_g_ref, ln2_b_ref, w_in_ref, w_out_ref, ln3_g_ref, ln3_b_ref,
                 y_ref):
    tiles = [slice(j * sub, (j + 1) * sub) for j in range(mix_ref.shape[0] // sub)]
    mixes = [_dot(mix_ref[r, :], w_mo_ref[...]) for r in tiles]
    h2s, ys = [], []
    for r, mix in zip(tiles, mixes):
        h2 = _layer_norm(DN_ALPHA * h_ref[r, :] + mix, ln2_g_ref[...], ln2_b_ref[...])
        h2s.append(h2)
        ys.append(_swiglu(h2, w_in_ref, w_out_ref))
    for r, h2, y in zip(tiles, h2s, ys):
        y_ref[r, :] = _ffn_norm(h2, y, ln3_g_ref, ln3_b_ref)


def _tail_call(mix, h, wts, tm, sub):
    n = mix.shape[0]
    row = lambda w: pl.BlockSpec((tm, w), lambda i: (i, 0))
    return pl.pallas_call(
        functools.partial(_tail_kernel, sub),
        grid=(n // tm,),
        in_specs=[row(D_MODEL), row(D_MODEL)] + [_const_spec(w.shape) for w in wts],
        out_specs=row(D_MODEL),
        out_shape=jax.ShapeDtypeStruct((n, D_MODEL), f32),
        compiler_params=pltpu.CompilerParams(dimension_semantics=("arbitrary",),
                                             vmem_limit_bytes=VMEM_LIMIT),
        name="tail",
    )(mix, h, *wts)


def _rope_tables(pos):
    posf = pos.astype(f32)[:, None]
    inv_r = ROPE_BASE ** (-jnp.arange(0, RET_DK, 2, dtype=f32) / RET_DK)
    ang = posf * inv_r[None, :]
    c, s = jnp.cos(ang), jnp.sin(ang)
    inv_m = ROPE_BASE ** (-jnp.arange(0, MLA_ROPE, 2, dtype=f32) / MLA_ROPE)
    angm = posf * inv_m[None, :]
    cm, sm = jnp.cos(angm), jnp.sin(angm)
    z = jnp.zeros_like(cm)
    z2 = jnp.zeros_like(c)
    tab = jnp.concatenate([c, c, -s, s, cm, cm, z2, -sm, z, z2, z, sm, z2], axis=1)
    return tab, jnp.concatenate([cm, sm], axis=1).T


def kernel(x_prompt, x_sample, cache_mla_ckv, cache_mla_kpe, state_ret, meta_tokens,
           ffn1_w_in, ffn1_w_out, ln1_g, ln1_b, w_mix_in, ret_gn_g, mla_q_norm_g, mla_w_uq,
           mla_kv_norm_g, mla_w_ukv, w_mix_out, ln2_g, ln2_b, ffn2_w_in, ffn2_w_out, ln3_g, ln3_b):
    B, S, D = x_prompt.shape
    DB, T, _ = x_sample.shape
    P = cache_mla_ckv.shape[1]
    assert S % ATT_TQ == 0 and S % FRONT_TM == 0 and ATT_TQ % RET_C == 0
    row = lambda a: a.reshape(1, -1).astype(f32)

    w_mix = jnp.pad(w_mix_in, ((0, 0), (0, D_IN_PAD - D_IN))).astype(bf16)
    w_uq = mla_w_uq.reshape(MLA_Q_LORA, MLA_HEADS, MLA_NOPE + MLA_ROPE)
    w_uq = jnp.pad(w_uq, ((0, 0), (0, 0), (0, MLA_QK_PAD - MLA_NOPE - MLA_ROPE)))
    w_uq = w_uq.reshape(MLA_Q_LORA, MLA_QW).astype(bf16)
    w_ukv = mla_w_ukv.reshape(MLA_KV_LORA, MLA_HEADS, MLA_NOPE + MLA_V)
    w_uk = w_ukv[:, :, :MLA_NOPE].reshape(MLA_KV_LORA, -1).astype(bf16)
    w_uv = w_ukv[:, :, MLA_NOPE:].reshape(MLA_KV_LORA, -1).astype(bf16)
    shared_w = [ffn1_w_in.astype(bf16), ffn1_w_out.astype(bf16), row(ln1_g), row(ln1_b), w_mix,
                row(ret_gn_g), row(mla_q_norm_g)]
    front_w = shared_w + [w_uq, row(mla_kv_norm_g), w_uk, w_uv]
    front_t_w = shared_w + [w_uq.T, row(mla_kv_norm_g), w_uk, w_uv.T]
    tail_w = [w_mix_out.astype(bf16), row(ln2_g), row(ln2_b), ffn2_w_in.astype(bf16),
              ffn2_w_out.astype(bf16), row(ln3_g), row(ln3_b)]

    tab_p, tab_t_p = _rope_tables(N_META + jnp.arange(S))
    (h_p, qr_p, kr_p, vr_p, gr_p, qt_p, km_p, vt_p, ckv_p, kpe_p) = _front_call(
        x_prompt.reshape(B * S, D), tab_p, tab_t_p, front_t_w, FRONT_TM, FRONT_SUB, S // FRONT_TM)
    n_s = DB * T
    x_small = jnp.concatenate([x_sample.reshape(n_s, D), meta_tokens.astype(x_sample.dtype)], axis=0)
    pos_small = jnp.concatenate([jnp.tile(N_META + P + jnp.arange(T), DB), jnp.arange(N_META)])
    (h_s, qr_s, kr_s, vr_s, gr_s, qm_s, km_s, vm_s, ckv_s, kpe_s) = _front_call(
        x_small, _rope_tables(pos_small)[0], None, front_w, n_s + N_META, n_s + N_META, 1)
    kmeta, vmeta = km_s[n_s:], vm_s[n_s:]
    krmeta, vrmeta = kr_s[n_s:], vr_s[n_s:]

    mix_p, p_state = _mixer_call(qt_p, km_p, vt_p, kmeta, vmeta, qr_p, kr_p, vr_p, gr_p,
                                 krmeta, vrmeta, B, S)
    ckpe = jnp.pad(cache_mla_kpe, ((0, 0), (0, 0), (0, LANES - MLA_ROPE)))
    mix_s, s_state = _smixer_call(qm_s[:n_s], km_s[:n_s], vm_s[:n_s], kmeta, vmeta, cache_mla_ckv, ckpe,
                                  jnp.concatenate([w_uk, w_uv], axis=1), qr_s[:n_s], kr_s[:n_s], vr_s[:n_s],
                                  gr_s[:n_s], state_ret, T)

    y_p = _tail_call(mix_p, h_p, tail_w, FRONT_TM, FRONT_SUB)
    y_s = _tail_call(mix_s, h_s[:n_s], tail_w, n_s, n_s // 2)

    meta_ckv = jnp.broadcast_to(ckv_s[n_s:][None], (B, N_META, MLA_KV_LORA))
    meta_kpe = jnp.broadcast_to(kpe_s[n_s:, :MLA_ROPE][None], (B, N_META, MLA_ROPE))
    p_ckv = jnp.concatenate([meta_ckv, ckv_p.reshape(B, S, MLA_KV_LORA)], axis=1)
    p_kpe = jnp.concatenate([meta_kpe, kpe_p[:, :MLA_ROPE].reshape(B, S, MLA_ROPE)], axis=1)
    return (y_p.reshape(B, S, D), y_s.reshape(DB, T, D), p_ckv, p_kpe, p_state.astype(x_prompt.dtype),
            ckv_s[:n_s].reshape(DB, T, MLA_KV_LORA), kpe_s[:n_s, :MLA_ROPE].reshape(DB, T, MLA_ROPE),
            s_state.astype(state_ret.dtype))
```

```python
import functools
import math

import jax
import jax.numpy as jnp
from jax import lax
from jax.experimental import pallas as pl
from jax.experimental.pallas import tpu as pltpu

D_MODEL = 1024
DEPTH = 1
CHUNK = 64
N_META = 16
RET_HEADS = 4
RET_DK = 128
RET_DV = 128
MLA_HEADS = 4
MLA_NOPE = 128
MLA_ROPE = 64
MLA_V = 128
MLA_Q_LORA = 256
MLA_KV_LORA = 128
D_FF = 2816
ROPE_BASE = 10000.0
LN_EPS = 1e-5
RMS_EPS = 1e-6
DN_ALPHA = (2 * DEPTH) ** 0.25

OFF_KR = RET_HEADS * RET_DK
OFF_VR = 2 * RET_HEADS * RET_DK
OFF_GR = OFF_VR + RET_HEADS * RET_DV
OFF_CQ = OFF_GR + RET_HEADS * RET_DV
OFF_CKV = OFF_CQ + MLA_Q_LORA
OFF_KPE = OFF_CKV + MLA_KV_LORA
D_IN = OFF_KPE + MLA_ROPE

LANES = 128
MLA_QK_PAD = 2 * LANES
RET_W = RET_HEADS * RET_DK
MLA_QW = MLA_HEADS * MLA_QK_PAD
MLA_VW = MLA_HEADS * MLA_V

FRONT_SUB = 256
FRONT_TM = 2 * FRONT_SUB
ATT_QB = FRONT_SUB
ATT_TQ = 2 * ATT_QB
ATT_TK = 2 * ATT_QB
RET_C = 256
VMEM_LIMIT = 56 * 1024 * 1024

MLA_SCALE = (MLA_NOPE + MLA_ROPE) ** -0.5
MLA_SCALE_LOG2E = MLA_SCALE * math.log2(math.e)
RET_LOG_GAMMA = tuple(math.log(1.0 - 2.0 ** (-5.0 - h)) for h in range(RET_HEADS))

f32 = jnp.float32
bf16 = jnp.bfloat16


def _dot(a, b):
    return jnp.dot(a, b, preferred_element_type=f32)


def _dot_nt(a, b):
    return lax.dot_general(a, b, (((1,), (1,)), ((), ())), preferred_element_type=f32)


def _dot_tn(a, b):
    return lax.dot_general(a, b, (((0,), (0,)), ((), ())), preferred_element_type=f32)


def _layer_norm(x, g, b):
    mu = jnp.mean(x, axis=-1, keepdims=True)
    xc = x - mu
    var = jnp.mean(xc * xc, axis=-1, keepdims=True)
    return xc * lax.rsqrt(var + LN_EPS) * g + b


def _rms_norm(x, g):
    return x * lax.rsqrt(jnp.mean(x * x, axis=-1, keepdims=True) + RMS_EPS) * g


def _silu(x):
    return x / (1.0 + jnp.exp(-x))


def _swiglu(x, w_in_ref, w_out_ref):
    xb = x.astype(bf16)
    hg = _dot(xb, w_in_ref[:, :D_FF])
    hu = _dot(xb, w_in_ref[:, D_FF:])
    return _dot((_silu(hg) * hu).astype(bf16), w_out_ref[...])


def _ffn_norm(x, y, g_ref, b_ref):
    return _layer_norm(DN_ALPHA * x + 0.5 * y, g_ref[...], b_ref[...])


def _front_kernel(transposed, sub, x_ref, cr_ref, sr_ref, cm_ref, sm_ref, *refs):
    if transposed:
        tab_t_ref, refs = refs[0], refs[1:]
    (w_in_ref, w_out_ref, ln_g_ref, ln_b_ref, w_mix_ref, w_kpe_ref, gn_ref, qn_g_ref, w_uq_ref, kvn_g_ref,
     w_uk_ref, w_uv_ref,
     h_ref, qr_ref, kr_ref, vr_ref, gate_ref, qm_ref, km_ref, vm_ref, ckv_ref, kpe_ref) = refs
    n_sub = x_ref.shape[0] // sub
    tiles = [slice(j * sub, (j + 1) * sub) for j in range(n_sub)]

    xs = [x_ref[r, :] for r in tiles]
    ys = [_swiglu(x, w_in_ref, w_out_ref) for x in xs]
    ps = []
    for r, x, y in zip(tiles, xs, ys):
        h = _ffn_norm(x, y, ln_g_ref, ln_b_ref)
        h_ref[r, :] = h
        hb = h.astype(bf16)
        ps.append((_dot(hb, w_mix_ref[...]), _dot(hb, w_kpe_ref[...])))

    for j, (r, (p, p_kpe)) in enumerate(zip(tiles, ps)):
        c, s, cm, sm = cr_ref[r, :], sr_ref[r, :], cm_ref[r, :], sm_ref[r, :]
        z_q, z_h = jnp.zeros_like(cm), jnp.zeros_like(c)
        c_r = jnp.concatenate([c, c], axis=1)
        s_r = jnp.concatenate([-s, s], axis=1)
        c_m = jnp.concatenate([cm, cm, z_h], axis=1)
        s_ma = jnp.concatenate([-sm, z_q, z_h], axis=1)
        s_mb = jnp.concatenate([z_q, sm, z_h], axis=1)

        def rope_ret(xh):
            return xh * c_r + pltpu.roll(xh, 64, 1) * s_r

        def rope_mla(xh):
            return xh * c_m + pltpu.roll(xh, 96, 1) * s_ma + pltpu.roll(xh, 32, 1) * s_mb

        cq = _rms_norm(p[:, OFF_CQ:OFF_CKV], qn_g_ref[...]).astype(bf16)
        ckv = _rms_norm(p[:, OFF_CKV:OFF_KPE], kvn_g_ref[...])
        ckv_ref[r, :] = ckv
        ckv_b = ckv.astype(bf16)
        k_nope = _dot(ckv_b, w_uk_ref[...])
        if transposed:
            q_t = _dot_nt(w_uq_ref[...], cq)
            v_t = _dot_nt(w_uv_ref[...], ckv_b)
        else:
            q = _dot(cq, w_uq_ref[...])
            v = _dot(ckv_b, w_uv_ref[...])

        for hd in range(RET_HEADS):
            lo = hd * RET_DK
            qr_ref[r, lo:lo + RET_DK] = rope_ret(p[:, lo:lo + RET_DK]).astype(bf16)
            kr_ref[r, lo:lo + RET_DK] = (rope_ret(p[:, OFF_KR + lo:OFF_KR + lo + RET_DK])
                                         * RET_DK ** -0.5).astype(bf16)
        vr_ref[r, :] = p[:, OFF_VR:OFF_GR].astype(bf16)
        gate_ref[r, :] = gn_ref[...] * _silu(p[:, OFF_GR:OFF_CQ])
        kpe = rope_mla(p_kpe)
        kpe_ref[r, :] = kpe[:, :MLA_ROPE]
        kpe_b = kpe.astype(bf16)
        for hd in range(MLA_HEADS):
            lo = hd * MLA_QK_PAD
            km_ref[r, lo:lo + LANES] = k_nope[:, hd * MLA_NOPE:(hd + 1) * MLA_NOPE].astype(bf16)
            km_ref[r, lo + LANES:lo + 2 * LANES] = kpe_b

        if transposed:
            half = MLA_ROPE // 2
            c_t = tab_t_ref[:half, r]
            s_t = tab_t_ref[half:, r]
            for hd in range(MLA_HEADS):
                lo = hd * MLA_QK_PAD
                r0 = lo + MLA_NOPE
                x1 = q_t[r0:r0 + half, :]
                x2 = q_t[r0 + half:r0 + 2 * half, :]
                qm_ref[j, lo:r0, :] = q_t[lo:r0, :].astype(bf16)
                qm_ref[j, r0:r0 + half, :] = (x1 * c_t - x2 * s_t).astype(bf16)
                qm_ref[j, r0 + half:r0 + 2 * half, :] = (x1 * s_t + x2 * c_t).astype(bf16)
                qm_ref[j, r0 + 2 * half:lo + MLA_QK_PAD, :] = q_t[r0 + 2 * half:lo + MLA_QK_PAD, :].astype(bf16)
            vm_ref[j] = v_t.astype(bf16)
        else:
            for hd in range(MLA_HEADS):
                lo = hd * MLA_QK_PAD
                qm_ref[r, lo:lo + LANES] = q[:, lo:lo + LANES].astype(bf16)
                qm_ref[r, lo + LANES:lo + 2 * LANES] = rope_mla(q[:, lo + LANES:lo + 2 * LANES]).astype(bf16)
            vm_ref[r, :] = v.astype(bf16)


def _const_spec(shape):
    nd = len(shape)
    return pl.BlockSpec(shape, lambda *_: (0,) * nd, pipeline_mode=pl.Buffered(1))


def _front_call(x, tabs, tab_t, wts, tm, sub, tab_blocks):
    n = x.shape[0]
    transposed = tab_t is not None
    row = lambda w: pl.BlockSpec((tm, w), lambda i: (i, 0))
    col = lambda w: pl.BlockSpec((tm // sub, w, sub), lambda i: (i, 0, 0))
    in_specs = [row(D_MODEL)] + [pl.BlockSpec((tm, t.shape[1]), lambda i: (i % tab_blocks, 0)) for t in tabs]
    args = [x, *tabs]
    if transposed:
        in_specs.append(pl.BlockSpec((MLA_ROPE, tm), lambda i: (0, i % tab_blocks)))
        args.append(tab_t)
    in_specs += [_const_spec(w.shape) for w in wts]
    rows = lambda w, dt: (row(w), jax.ShapeDtypeStruct((n, w), dt))
    cols = lambda w, dt: (col(w), jax.ShapeDtypeStruct((n // sub, w, sub), dt))
    qv = cols if transposed else rows
    outs = [rows(D_MODEL, f32), rows(RET_W, bf16), rows(RET_W, bf16), rows(RET_W, bf16), rows(RET_W, f32),
            qv(MLA_QW, bf16), rows(MLA_QW, bf16), qv(MLA_VW, bf16), rows(MLA_KV_LORA, f32), rows(MLA_ROPE, f32)]
    return pl.pallas_call(
        functools.partial(_front_kernel, transposed, sub),
        grid=(n // tm,),
        in_specs=in_specs,
        out_specs=[o[0] for o in outs],
        out_shape=[o[1] for o in outs],
        compiler_params=pltpu.CompilerParams(dimension_semantics=("arbitrary",),
                                             vmem_limit_bytes=VMEM_LIMIT),
        name="front_t" if transposed else "front",
    )(*args, *wts)


def _decay_mask(c, lg):
    r = lax.broadcasted_iota(jnp.int32, (c, c), 0)
    k = lax.broadcasted_iota(jnp.int32, (c, c), 1)
    rel = (r - k).astype(f32)
    return jnp.where(rel >= 0.0, jnp.exp(lg * jnp.maximum(rel, 0.0)), 0.0)


def _row_pow(c, lg, offset, sign):
    i = lax.broadcasted_iota(jnp.int32, (c, LANES), 0).astype(f32)
    return jnp.exp(lg * (offset + sign * i))


def _retention_tables(c, hd):
    lg = RET_LOG_GAMMA[hd]
    return _decay_mask(c, lg), _row_pow(c, lg, 1.0, 1.0), _row_pow(c, lg, c - 1.0, -1.0)


def _retention_chunk(qr_ref, kr_ref, vr_ref, gate_ref, mix_ref, rows, states, tables):
    c = rows.stop - rows.start
    heads = range(RET_HEADS)
    cols = [slice(hd * RET_DK, (hd + 1) * RET_DK) for hd in heads]
    q = [qr_ref[rows, cols[hd]] for hd in heads]
    k = [kr_ref[rows, cols[hd]] for hd in heads]
    v = [vr_ref[rows, cols[hd]] for hd in heads]
    scores = [_dot_nt(q[hd], k[hd]) for hd in heads]
    cross = [_dot(q[hd], states[hd].astype(bf16)) for hd in heads]
    kv = [_dot_tn((k[hd].astype(f32) * tables[hd][2][...]).astype(bf16), v[hd]) for hd in heads]
    new_states = [math.exp(RET_LOG_GAMMA[hd] * c) * states[hd] + kv[hd] for hd in heads]
    inner = [_dot((scores[hd] * tables[hd][0][...]).astype(bf16), v[hd]) for hd in heads]
    for hd in heads:
        o = inner[hd] + cross[hd] * tables[hd][1][...]
        mu = jnp.mean(o, axis=-1, keepdims=True)
        oc = o - mu
        var = jnp.mean(oc * oc, axis=-1, keepdims=True)
        y = oc * lax.rsqrt(var + LN_EPS) * gate_ref[rows, cols[hd]]
        mix_ref[rows, cols[hd]] = y.astype(bf16)
    return new_states


def _att_update(m_ref, l_ref, acc_ref, u, s, v_t, ok):
    s = s * MLA_SCALE_LOG2E
    if ok is not None:
        s = jnp.where(ok, s, -jnp.inf)
    m = m_ref[u]
    m_new = jnp.maximum(m, jnp.max(s, axis=0, keepdims=True))
    p = jnp.exp2(s - m_new)
    alpha = jnp.exp2(m - m_new)
    m_ref[u] = m_new
    l_ref[u] = alpha * l_ref[u] + jnp.sum(p, axis=0, keepdims=True)
    acc_ref[u] = alpha * acc_ref[u] + _dot(v_t, p.astype(bf16))


def _mixer_kernel(qt_ref, km_ref, vt_ref, kmeta_ref, vmeta_ref, qr_ref, kr_ref, vr_ref, gate_ref,
                  krmeta_ref, vrmeta_ref, mix_ref, state_ref,
                  m_ref, l_ref, acc_ref, s_ref, dmask_ref, qdec_ref, kdec_ref):
    qi = pl.program_id(1)

    @pl.when((pl.program_id(0) == 0) & (qi == 0))
    def _():
        for hd in range(RET_HEADS):
            dmask_ref[hd], qdec_ref[hd], kdec_ref[hd] = _retention_tables(RET_C, hd)

    @pl.when(qi == 0)
    def _():
        for hd in range(RET_HEADS):
            lo = hd * RET_DK
            lg = RET_LOG_GAMMA[hd]
            kd = (krmeta_ref[:, lo:lo + RET_DK].astype(f32)
                  * _row_pow(N_META, lg, N_META - 1.0, -1.0)).astype(bf16)
            state_ref[0, hd] = _dot_tn(kd, vrmeta_ref[:, lo:lo + RET_DV])

    tables = [(dmask_ref.at[hd], qdec_ref.at[hd], kdec_ref.at[hd]) for hd in range(RET_HEADS)]
    states = [state_ref[0, hd] for hd in range(RET_HEADS)]
    for c in range(ATT_TQ // RET_C):
        states = _retention_chunk(qr_ref, kr_ref, vr_ref, gate_ref, mix_ref,
                                  slice(c * RET_C, (c + 1) * RET_C), states, tables)
    for hd in range(RET_HEADS):
        state_ref[0, hd] = states[hd]

    n_qb = ATT_TQ // ATT_QB
    units = [(hd, qb) for hd in range(MLA_HEADS) for qb in range(n_qb)]

    def q_block(hd, qb):
        return qt_ref[qb, hd * MLA_QK_PAD:(hd + 1) * MLA_QK_PAD, :]

    def v_tile(hd, blk0, n_blk):
        rows = slice(hd * MLA_V, (hd + 1) * MLA_V)
        return jnp.concatenate([vt_ref[blk0 + j, rows, :] for j in range(n_blk)], axis=1)

    def scores(start, n_keys, u):
        hd, qb = units[u]
        k = km_ref[0, pl.ds(start, n_keys), hd * MLA_QK_PAD:(hd + 1) * MLA_QK_PAD]
        return _dot(k, q_block(hd, qb))

    s_ref[...] = scores(0, ATT_TK, 0)

    s_meta = [_dot(kmeta_ref[:, hd * MLA_QK_PAD:(hd + 1) * MLA_QK_PAD], q_block(hd, qb)) for hd, qb in units]
    p_meta = []
    for u, s0 in enumerate(s_meta):
        s0 = s0 * MLA_SCALE_LOG2E
        m = jnp.max(s0, axis=0, keepdims=True)
        p0 = jnp.exp2(s0 - m)
        m_ref[u] = m
        l_ref[u] = jnp.sum(p0, axis=0, keepdims=True)
        p_meta.append(p0.astype(bf16))
    for u, (hd, qb) in enumerate(units):
        acc_ref[u] = _dot_tn(vmeta_ref[:, hd * MLA_V:(hd + 1) * MLA_V], p_meta[u])

    def full_step(kt, _):
        start = pl.multiple_of(kt * ATT_TK, ATT_TK)
        s_cur = s_ref[...]
        for u, (hd, qb) in enumerate(units):
            if u + 1 < len(units):
                s_next = scores(start, ATT_TK, u + 1)
            else:
                s_next = scores(pl.multiple_of(start + ATT_TK, ATT_TK), ATT_TK, 0)
            _att_update(m_ref, l_ref, acc_ref, u, s_cur,
                        v_tile(hd, kt * (ATT_TK // ATT_QB), ATT_TK // ATT_QB), None)
            s_cur = s_next
        s_ref[...] = s_cur
        return 0

    lax.fori_loop(0, qi, full_step, 0)

    start = pl.multiple_of(qi * ATT_TQ, ATT_TQ)
    s_cur = s_ref[...]
    for u, (hd, qb) in enumerate(units):
        if u + 1 < len(units):
            s_next = scores(start, (units[u + 1][1] + 1) * ATT_QB, u + 1)
        k_chunk = lax.broadcasted_iota(jnp.int32, s_cur.shape, 0) // CHUNK
        q_chunk = lax.broadcasted_iota(jnp.int32, s_cur.shape, 1) // CHUNK + qb * (ATT_QB // CHUNK)
        _att_update(m_ref, l_ref, acc_ref, u, s_cur, v_tile(hd, qi * n_qb, s_cur.shape[0] // ATT_QB),
                    k_chunk <= q_chunk)
        out = (acc_ref[u] / l_ref[u]).T
        mix_ref[qb * ATT_QB:(qb + 1) * ATT_QB, RET_W + hd * MLA_V:RET_W + (hd + 1) * MLA_V] = out.astype(bf16)
        s_cur = s_next


def _mixer_call(qt, km, vt, kmeta, vmeta, qr, kr, vr, gate, krmeta, vrmeta, batch, seq):
    nq = seq // ATT_TQ
    n_qb = ATT_TQ // ATT_QB
    tile = lambda w: pl.BlockSpec((ATT_TQ, w), lambda b, i: (b * nq + i, 0))
    const = lambda a: pl.BlockSpec(a.shape, lambda b, i: (0,) * a.ndim)
    return pl.pallas_call(
        _mixer_kernel,
        grid=(batch, nq),
        in_specs=[pl.BlockSpec((n_qb, MLA_QW, ATT_QB), lambda b, i: (b * nq + i, 0, 0)),
                  pl.BlockSpec((1, seq, MLA_QW), lambda b, i: (b, 0, 0)),
                  pl.BlockSpec((seq // ATT_QB, MLA_VW, ATT_QB), lambda b, i: (b, 0, 0)),
                  const(kmeta), const(vmeta),
                  tile(RET_W), tile(RET_W), tile(RET_W), tile(RET_W),
                  const(krmeta), const(vrmeta)],
        out_specs=[tile(D_MODEL),
                   pl.BlockSpec((1, RET_HEADS, RET_DK, RET_DV), lambda b, i: (b, 0, 0, 0))],
        out_shape=[jax.ShapeDtypeStruct((batch * seq, D_MODEL), bf16),
                   jax.ShapeDtypeStruct((batch, RET_HEADS, RET_DK, RET_DV), f32)],
        scratch_shapes=[pltpu.VMEM((MLA_HEADS * n_qb, 1, ATT_QB), f32),
                        pltpu.VMEM((MLA_HEADS * n_qb, 1, ATT_QB), f32),
                        pltpu.VMEM((MLA_HEADS * n_qb, MLA_V, ATT_QB), f32),
                        pltpu.VMEM((ATT_TK, ATT_QB), f32),
                        pltpu.VMEM((RET_HEADS, RET_C, RET_C), f32),
                        pltpu.VMEM((RET_HEADS, RET_C, LANES), f32),
                        pltpu.VMEM((RET_HEADS, RET_C, LANES), f32)],
        compiler_params=pltpu.CompilerParams(dimension_semantics=("arbitrary", "arbitrary"),
                                             vmem_limit_bytes=VMEM_LIMIT),
        name="mixer",
    )(qt, km.reshape(batch, seq, MLA_QW), vt, kmeta, vmeta, qr, kr, vr, gate, krmeta, vrmeta)


def _smixer_kernel(qm_ref, knew_ref, vnew_ref, kmeta_ref, vmeta_ref, cckv_ref, ckpe_ref, w_ukv_ref,
                   qr_ref, kr_ref, vr_ref, gate_ref, s0_ref, mix_ref, state_ref):
    t = qr_ref.shape[0]
    states = _retention_chunk(qr_ref, kr_ref, vr_ref, gate_ref, mix_ref, slice(0, t),
                              [s0_ref[0, hd] for hd in range(RET_HEADS)],
                              [_retention_tables(t, hd) for hd in range(RET_HEADS)])
    for hd in range(RET_HEADS):
        state_ref[0, hd] = states[hd]

    kv_c = _dot(cckv_ref[0].astype(bf16), w_ukv_ref[...])
    kpe_c = ckpe_ref[0].astype(bf16)
    for hd in range(MLA_HEADS):
        klo = hd * MLA_QK_PAD
        vlo = hd * MLA_V
        q = qm_ref[:, klo:klo + MLA_QK_PAD]
        s_m = _dot_nt(q, kmeta_ref[:, klo:klo + MLA_QK_PAD]) * MLA_SCALE
        s_c = (_dot_nt(q[:, :LANES], kv_c[:, hd * MLA_NOPE:(hd + 1) * MLA_NOPE].astype(bf16))
               + _dot_nt(q[:, LANES:LANES + MLA_ROPE], kpe_c)) * MLA_SCALE
        s_n = _dot_nt(q, knew_ref[:, klo:klo + MLA_QK_PAD]) * MLA_SCALE
        m = jnp.maximum(jnp.maximum(jnp.max(s_m, axis=-1, keepdims=True),
                                    jnp.max(s_c, axis=-1, keepdims=True)),
                        jnp.max(s_n, axis=-1, keepdims=True))
        p_m = jnp.exp(s_m - m)
        p_c = jnp.exp(s_c - m)
        p_n = jnp.exp(s_n - m)
        l = (jnp.sum(p_m, axis=-1, keepdims=True) + jnp.sum(p_c, axis=-1, keepdims=True)
             + jnp.sum(p_n, axis=-1, keepdims=True))
        v_c = kv_c[:, MLA_HEADS * MLA_NOPE + vlo:MLA_HEADS * MLA_NOPE + vlo + MLA_V].astype(bf16)
        acc = (_dot(p_m.astype(bf16), vmeta_ref[:, vlo:vlo + MLA_V]) + _dot(p_c.astype(bf16), v_c)
               + _dot(p_n.astype(bf16), vnew_ref[:, vlo:vlo + MLA_V]))
        mix_ref[:, RET_W + vlo:RET_W + vlo + MLA_V] = (acc / l).astype(bf16)


def _smixer_call(qm, knew, vnew, kmeta, vmeta, cckv, ckpe, w_ukv, qr, kr, vr, gate, s0, t):
    db, past = cckv.shape[0], cckv.shape[1]
    tile = lambda w: pl.BlockSpec((t, w), lambda b: (b, 0))
    const = lambda a: pl.BlockSpec(a.shape, lambda b: (0,) * a.ndim)
    per = lambda *tail: pl.BlockSpec((1,) + tail, lambda b: (b,) + (0,) * len(tail))
    return pl.pallas_call(
        _smixer_kernel,
        grid=(db,),
        in_specs=[tile(MLA_QW), tile(MLA_QW), tile(MLA_VW), const(kmeta), const(vmeta),
                  per(past, MLA_KV_LORA), per(past, MLA_ROPE), const(w_ukv),
                  tile(RET_W), tile(RET_W), tile(RET_W), tile(RET_W),
                  per(RET_HEADS, RET_DK, RET_DV)],
        out_specs=[tile(D_MODEL), per(RET_HEADS, RET_DK, RET_DV)],
        out_shape=[jax.ShapeDtypeStruct((db * t, D_MODEL), bf16),
                   jax.ShapeDtypeStruct((db, RET_HEADS, RET_DK, RET_DV), f32)],
        compiler_params=pltpu.CompilerParams(dimension_semantics=("arbitrary",),
                                             vmem_limit_bytes=VMEM_LIMIT),
        name="smixer",
    )(qm, knew, vnew, kmeta, vmeta, cckv, ckpe, w_ukv, qr, kr, vr, gate, s0)


def _tail_kernel(sub, mix_ref, h_ref, w_mo_ref, ln2_g_ref, ln2_b_ref, w_in_ref, w_out_ref, ln3_g_ref, ln3_b_ref,
                 y_ref):
    tiles = [slice(j * sub, (j + 1) * sub) for j in range(mix_ref.shape[0] // sub)]
    mixes = [_dot(mix_ref[r, :], w_mo_ref[...]) for r in tiles]
    h2s, ys = [], []
    for r, mix in zip(tiles, mixes):
        h2 = _layer_norm(DN_ALPHA * h_ref[r, :] + mix, ln2_g_ref[...], ln2_b_ref[...])
        h2s.append(h2)
        ys.append(_swiglu(h2, w_in_ref, w_out_ref))
    for r, h2, y in zip(tiles, h2s, ys):
        y_ref[r, :] = _ffn_norm(h2, y, ln3_g_ref, ln3_b_ref)


def _tail_call(mix, h, wts, tm, sub):
    n = mix.shape[0]
    row = lambda w: pl.BlockSpec((tm, w), lambda i: (i, 0))
    return pl.pallas_call(
        functools.partial(_tail_kernel, sub),
        grid=(n // tm,),
        in_specs=[row(D_MODEL), row(D_MODEL)] + [_const_spec(w.shape) for w in wts],
        out_specs=row(D_MODEL),
        out_shape=jax.ShapeDtypeStruct((n, D_MODEL), f32),
        compiler_params=pltpu.CompilerParams(dimension_semantics=("arbitrary",),
                                             vmem_limit_bytes=VMEM_LIMIT),
        name="tail",
    )(mix, h, *wts)


def _rope_tables(pos):
    posf = pos.astype(f32)[:, None]
    inv_r = ROPE_BASE ** (-jnp.arange(0, RET_DK, 2, dtype=f32) / RET_DK)
    ang = posf * inv_r[None, :]
    inv_m = ROPE_BASE ** (-jnp.arange(0, MLA_ROPE, 2, dtype=f32) / MLA_ROPE)
    angm = posf * inv_m[None, :]
    return jnp.cos(ang), jnp.sin(ang), jnp.cos(angm), jnp.sin(angm)


def kernel(x_prompt, x_sample, cache_mla_ckv, cache_mla_kpe, state_ret, meta_tokens,
           ffn1_w_in, ffn1_w_out, ln1_g, ln1_b, w_mix_in, ret_gn_g, mla_q_norm_g, mla_w_uq,
           mla_kv_norm_g, mla_w_ukv, w_mix_out, ln2_g, ln2_b, ffn2_w_in, ffn2_w_out, ln3_g, ln3_b):
    B, S, D = x_prompt.shape
    DB, T, _ = x_sample.shape
    P = cache_mla_ckv.shape[1]
    assert S % ATT_TQ == 0 and S % FRONT_TM == 0 and ATT_TQ % RET_C == 0
    row = lambda a: a.reshape(1, -1).astype(f32)

    w_mix = w_mix_in[:, :OFF_KPE].astype(bf16)
    w_kpe = jnp.pad(w_mix_in[:, OFF_KPE:], ((0, 0), (0, LANES - MLA_ROPE))).astype(bf16)
    w_uq = mla_w_uq.reshape(MLA_Q_LORA, MLA_HEADS, MLA_NOPE + MLA_ROPE)
    w_uq = jnp.pad(w_uq, ((0, 0), (0, 0), (0, MLA_QK_PAD - MLA_NOPE - MLA_ROPE)))
    w_uq = w_uq.reshape(MLA_Q_LORA, MLA_QW).astype(bf16)
    w_ukv = mla_w_ukv.reshape(MLA_KV_LORA, MLA_HEADS, MLA_NOPE + MLA_V)
    w_uk = w_ukv[:, :, :MLA_NOPE].reshape(MLA_KV_LORA, -1).astype(bf16)
    w_uv = w_ukv[:, :, MLA_NOPE:].reshape(MLA_KV_LORA, -1).astype(bf16)
    shared_w = [ffn1_w_in.astype(bf16), ffn1_w_out.astype(bf16), row(ln1_g), row(ln1_b), w_mix, w_kpe,
                row(ret_gn_g), row(mla_q_norm_g)]
    front_w = shared_w + [w_uq, row(mla_kv_norm_g), w_uk, w_uv]
    front_t_w = shared_w + [w_uq.T, row(mla_kv_norm_g), w_uk, w_uv.T]
    tail_w = [w_mix_out.astype(bf16), row(ln2_g), row(ln2_b), ffn2_w_in.astype(bf16),
              ffn2_w_out.astype(bf16), row(ln3_g), row(ln3_b)]

    tabs_p = _rope_tables(N_META + jnp.arange(S))
    tab_t_p = jnp.concatenate([tabs_p[2].T, tabs_p[3].T], axis=0)
    (h_p, qr_p, kr_p, vr_p, gr_p, qt_p, km_p, vt_p, ckv_p, kpe_p) = _front_call(
        x_prompt.reshape(B * S, D), tabs_p, tab_t_p, front_t_w, FRONT_TM, FRONT_SUB, S // FRONT_TM)
    n_s = DB * T
    x_small = jnp.concatenate([x_sample.reshape(n_s, D), meta_tokens.astype(x_sample.dtype)], axis=0)
    pos_small = jnp.concatenate([jnp.tile(N_META + P + jnp.arange(T), DB), jnp.arange(N_META)])
    (h_s, qr_s, kr_s, vr_s, gr_s, qm_s, km_s, vm_s, ckv_s, kpe_s) = _front_call(
        x_small, _rope_tables(pos_small), None, front_w, n_s + N_META, n_s + N_META, 1)
    kmeta, vmeta = km_s[n_s:], vm_s[n_s:]
    krmeta, vrmeta = kr_s[n_s:], vr_s[n_s:]

    mix_p, p_state = _mixer_call(qt_p, km_p, vt_p, kmeta, vmeta, qr_p, kr_p, vr_p, gr_p,
                                 krmeta, vrmeta, B, S)
    mix_s, s_state = _smixer_call(qm_s[:n_s], km_s[:n_s], vm_s[:n_s], kmeta, vmeta, cache_mla_ckv, cache_mla_kpe,
                                  jnp.concatenate([w_uk, w_uv], axis=1), qr_s[:n_s], kr_s[:n_s], vr_s[:n_s],
                                  gr_s[:n_s], state_ret, T)

    y_p = _tail_call(mix_p, h_p, tail_w, FRONT_TM, FRONT_SUB)
    y_s = _tail_call(mix_s, h_s[:n_s], tail_w, n_s, n_s // 2)

    meta_ckv = jnp.broadcast_to(ckv_s[n_s:][None], (B, N_META, MLA_KV_LORA))
    meta_kpe = jnp.broadcast_to(kpe_s[n_s:][None], (B, N_META, MLA_ROPE))
    p_ckv = jnp.concatenate([meta_ckv, ckv_p.reshape(B, S, MLA_KV_LORA)], axis=1)
    p_kpe = jnp.concatenate([meta_kpe, kpe_p.reshape(B, S, MLA_ROPE)], axis=1)
    return (y_p.reshape(B, S, D), y_s.reshape(DB, T, D), p_ckv, p_kpe, p_state.astype(x_prompt.dtype),
            ckv_s[:n_s].reshape(DB, T, MLA_KV_LORA), kpe_s[:n_s].reshape(DB, T, MLA_ROPE),
            s_state.astype(state_ret.dtype))
```

```python
import functools
import math

import jax
import jax.numpy as jnp
from jax import lax
from jax.experimental import pallas as pl
from jax.experimental.pallas import tpu as pltpu

D_MODEL = 1024
DEPTH = 1
CHUNK = 64
N_META = 16
RET_HEADS = 4
RET_DK = 128
RET_DV = 128
MLA_HEADS = 4
MLA_NOPE = 128
MLA_ROPE = 64
MLA_V = 128
MLA_Q_LORA = 256
MLA_KV_LORA = 128
D_FF = 2816
ROPE_BASE = 10000.0
LN_EPS = 1e-5
RMS_EPS = 1e-6
DN_ALPHA = (2 * DEPTH) ** 0.25

OFF_KR = RET_HEADS * RET_DK
OFF_VR = 2 * RET_HEADS * RET_DK
OFF_GR = OFF_VR + RET_HEADS * RET_DV
OFF_CQ = OFF_GR + RET_HEADS * RET_DV
OFF_CKV = OFF_CQ + MLA_Q_LORA
OFF_KPE = OFF_CKV + MLA_KV_LORA
D_IN = OFF_KPE + MLA_ROPE

LANES = 128
MLA_QK_PAD = 2 * LANES
RET_W = RET_HEADS * RET_DK
MLA_QW = MLA_HEADS * MLA_QK_PAD
MLA_VW = MLA_HEADS * MLA_V
BF16_SUBLANES = 16
MLA_VT_ROWS = MLA_V + BF16_SUBLANES

FRONT_SUB = 256
FRONT_TM = 2 * FRONT_SUB
ATT_QB = FRONT_TM
ATT_TQ = ATT_QB
ATT_TK = ATT_QB
RET_C = 256
VMEM_LIMIT = 56 * 1024 * 1024

MLA_SCALE = (MLA_NOPE + MLA_ROPE) ** -0.5
MLA_SCALE_LOG2E = MLA_SCALE * math.log2(math.e)
RET_LOG_GAMMA = tuple(math.log(1.0 - 2.0 ** (-5.0 - h)) for h in range(RET_HEADS))

f32 = jnp.float32
bf16 = jnp.bfloat16


def _dot(a, b):
    return jnp.dot(a, b, preferred_element_type=f32)


def _dot_nt(a, b):
    return lax.dot_general(a, b, (((1,), (1,)), ((), ())), preferred_element_type=f32)


def _dot_tn(a, b):
    return lax.dot_general(a, b, (((0,), (0,)), ((), ())), preferred_element_type=f32)


def _layer_norm(x, g, b):
    mu = jnp.mean(x, axis=-1, keepdims=True)
    xc = x - mu
    var = jnp.mean(xc * xc, axis=-1, keepdims=True)
    return xc * lax.rsqrt(var + LN_EPS) * g + b


def _rms_norm(x, g):
    return x * lax.rsqrt(jnp.mean(x * x, axis=-1, keepdims=True) + RMS_EPS) * g


def _silu(x):
    return x / (1.0 + jnp.exp(-x))


def _swiglu(x, w_in_ref, w_out_ref):
    xb = x.astype(bf16)
    hg = _dot(xb, w_in_ref[:, :D_FF])
    hu = _dot(xb, w_in_ref[:, D_FF:])
    return _dot((_silu(hg) * hu).astype(bf16), w_out_ref[...])


def _ffn_norm(x, y, g_ref, b_ref):
    return _layer_norm(DN_ALPHA * x + 0.5 * y, g_ref[...], b_ref[...])


def _front_kernel(transposed, sub, x_ref, cr_ref, sr_ref, cm_ref, sm_ref, *refs):
    if transposed:
        tab_t_ref, refs = refs[0], refs[1:]
    (w_in_ref, w_out_ref, ln_g_ref, ln_b_ref, w_mix_ref, w_kpe_ref, gn_ref, qn_g_ref, w_uq_ref, kvn_g_ref,
     w_uk_ref, w_uv_ref,
     h_ref, qr_ref, kr_ref, vr_ref, gate_ref, qm_ref, km_ref, vm_ref, ckv_ref, kpe_ref) = refs
    n_sub = x_ref.shape[0] // sub
    tiles = [slice(j * sub, (j + 1) * sub) for j in range(n_sub)]

    xs = [x_ref[r, :] for r in tiles]
    ys = [_swiglu(x, w_in_ref, w_out_ref) for x in xs]
    ps = []
    for r, x, y in zip(tiles, xs, ys):
        h = _ffn_norm(x, y, ln_g_ref, ln_b_ref)
        h_ref[r, :] = h
        hb = h.astype(bf16)
        ps.append((_dot(hb, w_mix_ref[...]), _dot(hb, w_kpe_ref[...])))

    for j, (r, (p, p_kpe)) in enumerate(zip(tiles, ps)):
        c, s, cm, sm = cr_ref[r, :], sr_ref[r, :], cm_ref[r, :], sm_ref[r, :]
        z_q, z_h = jnp.zeros_like(cm), jnp.zeros_like(c)
        c_r = jnp.concatenate([c, c], axis=1)
        s_r = jnp.concatenate([-s, s], axis=1)
        c_m = jnp.concatenate([cm, cm, z_h], axis=1)
        s_ma = jnp.concatenate([-sm, z_q, z_h], axis=1)
        s_mb = jnp.concatenate([z_q, sm, z_h], axis=1)

        def rope_ret(xh):
            return xh * c_r + pltpu.roll(xh, 64, 1) * s_r

        def rope_mla(xh):
            return xh * c_m + pltpu.roll(xh, 96, 1) * s_ma + pltpu.roll(xh, 32, 1) * s_mb

        cq = _rms_norm(p[:, OFF_CQ:OFF_CKV], qn_g_ref[...]).astype(bf16)
        ckv = _rms_norm(p[:, OFF_CKV:OFF_KPE], kvn_g_ref[...])
        ckv_ref[r, :] = ckv
        ckv_b = ckv.astype(bf16)
        k_nope = _dot(ckv_b, w_uk_ref[...])
        if transposed:
            q_t = _dot_nt(w_uq_ref[...], cq) * MLA_SCALE_LOG2E
            v_t = _dot_nt(w_uv_ref[...], ckv_b)
        else:
            q = _dot(cq, w_uq_ref[...])
            v = _dot(ckv_b, w_uv_ref[...])

        for hd in range(RET_HEADS):
            lo = hd * RET_DK
            qr_ref[r, lo:lo + RET_DK] = rope_ret(p[:, lo:lo + RET_DK]).astype(bf16)
            kr_ref[r, lo:lo + RET_DK] = (rope_ret(p[:, OFF_KR + lo:OFF_KR + lo + RET_DK])
                                         * RET_DK ** -0.5).astype(bf16)
        vr_ref[r, :] = p[:, OFF_VR:OFF_GR].astype(bf16)
        gate_ref[r, :] = gn_ref[...] * _silu(p[:, OFF_GR:OFF_CQ])
        kpe = rope_mla(p_kpe)
        kpe_ref[r, :] = kpe[:, :MLA_ROPE]
        kpe_b = kpe.astype(bf16)
        for hd in range(MLA_HEADS):
            lo = hd * MLA_QK_PAD
            km_ref[r, lo:lo + LANES] = k_nope[:, hd * MLA_NOPE:(hd + 1) * MLA_NOPE].astype(bf16)
            km_ref[r, lo + LANES:lo + 2 * LANES] = kpe_b

        if transposed:
            half = MLA_ROPE // 2
            c_t = tab_t_ref[:half, r]
            s_t = tab_t_ref[half:, r]
            for hd in range(MLA_HEADS):
                lo = hd * MLA_QK_PAD
                r0 = lo + MLA_NOPE
                x1 = q_t[r0:r0 + half, :]
                x2 = q_t[r0 + half:r0 + 2 * half, :]
                qm_ref[0, lo:r0, r] = q_t[lo:r0, :].astype(bf16)
                qm_ref[0, r0:r0 + half, r] = (x1 * c_t - x2 * s_t).astype(bf16)
                qm_ref[0, r0 + half:r0 + 2 * half, r] = (x1 * s_t + x2 * c_t).astype(bf16)
                qm_ref[0, r0 + 2 * half:lo + MLA_QK_PAD, r] = q_t[r0 + 2 * half:lo + MLA_QK_PAD, :].astype(bf16)
            ones = jnp.ones((MLA_VT_ROWS - MLA_V, v_t.shape[1]), bf16)
            for hd in range(MLA_HEADS):
                vm_ref[0, hd * MLA_VT_ROWS:hd * MLA_VT_ROWS + MLA_V, r] = v_t[hd * MLA_V:(hd + 1) * MLA_V, :].astype(bf16)
                vm_ref[0, hd * MLA_VT_ROWS + MLA_V:(hd + 1) * MLA_VT_ROWS, r] = ones
        else:
            for hd in range(MLA_HEADS):
                lo = hd * MLA_QK_PAD
                qm_ref[r, lo:lo + LANES] = q[:, lo:lo + LANES].astype(bf16)
                qm_ref[r, lo + LANES:lo + 2 * LANES] = rope_mla(q[:, lo + LANES:lo + 2 * LANES]).astype(bf16)
            vm_ref[r, :] = v.astype(bf16)


def _const_spec(shape):
    nd = len(shape)
    return pl.BlockSpec(shape, lambda *_: (0,) * nd, pipeline_mode=pl.Buffered(1))


def _front_call(x, tabs, tab_t, wts, tm, sub, tab_blocks):
    n = x.shape[0]
    transposed = tab_t is not None
    row = lambda w: pl.BlockSpec((tm, w), lambda i: (i, 0))
    col = lambda w: pl.BlockSpec((1, w, tm), lambda i: (i, 0, 0))
    in_specs = [row(D_MODEL)] + [pl.BlockSpec((tm, t.shape[1]), lambda i: (i % tab_blocks, 0)) for t in tabs]
    args = [x, *tabs]
    if transposed:
        in_specs.append(pl.BlockSpec((MLA_ROPE, tm), lambda i: (0, i % tab_blocks)))
        args.append(tab_t)
    in_specs += [_const_spec(w.shape) for w in wts]
    rows = lambda w, dt: (row(w), jax.ShapeDtypeStruct((n, w), dt))
    cols = lambda w, dt: (col(w), jax.ShapeDtypeStruct((n // tm, w, tm), dt))
    qv = cols if transposed else rows
    outs = [rows(D_MODEL, f32), rows(RET_W, bf16), rows(RET_W, bf16), rows(RET_W, bf16), rows(RET_W, f32),
            qv(MLA_QW, bf16), rows(MLA_QW, bf16),
            cols(MLA_HEADS * MLA_VT_ROWS, bf16) if transposed else rows(MLA_VW, bf16),
            rows(MLA_KV_LORA, f32), rows(MLA_ROPE, f32)]
    return pl.pallas_call(
        functools.partial(_front_kernel, transposed, sub),
        grid=(n // tm,),
        in_specs=in_specs,
        out_specs=[o[0] for o in outs],
        out_shape=[o[1] for o in outs],
        compiler_params=pltpu.CompilerParams(dimension_semantics=("arbitrary",),
                                             vmem_limit_bytes=VMEM_LIMIT),
        name="front_t" if transposed else "front",
    )(*args, *wts)


def _decay_mask(c, lg):
    r = lax.broadcasted_iota(jnp.int32, (c, c), 0)
    k = lax.broadcasted_iota(jnp.int32, (c, c), 1)
    rel = (r - k).astype(f32)
    return jnp.where(rel >= 0.0, jnp.exp(lg * jnp.maximum(rel, 0.0)), 0.0)


def _row_pow(c, lg, offset, sign):
    i = lax.broadcasted_iota(jnp.int32, (c, LANES), 0).astype(f32)
    return jnp.exp(lg * (offset + sign * i))


def _retention_tables(c, hd):
    lg = RET_LOG_GAMMA[hd]
    return _decay_mask(c, lg), _row_pow(c, lg, 1.0, 1.0), _row_pow(c, lg, c - 1.0, -1.0)


def _retention_chunk(qr_ref, kr_ref, vr_ref, gate_ref, mix_ref, rows, states, tables):
    c = rows.stop - rows.start
    heads = range(RET_HEADS)
    cols = [slice(hd * RET_DK, (hd + 1) * RET_DK) for hd in heads]
    q = [qr_ref[rows, cols[hd]] for hd in heads]
    k = [kr_ref[rows, cols[hd]] for hd in heads]
    v = [vr_ref[rows, cols[hd]] for hd in heads]
    scores = [_dot_nt(q[hd], k[hd]) for hd in heads]
    cross = [_dot(q[hd], states[hd].astype(bf16)) for hd in heads]
    kv = [_dot_tn((k[hd].astype(f32) * tables[hd][2][...]).astype(bf16), v[hd]) for hd in heads]
    new_states = [math.exp(RET_LOG_GAMMA[hd] * c) * states[hd] + kv[hd] for hd in heads]
    inner = [_dot((scores[hd] * tables[hd][0][...]).astype(bf16), v[hd]) for hd in heads]
    for hd in heads:
        o = inner[hd] + cross[hd] * tables[hd][1][...]
        mu = jnp.mean(o, axis=-1, keepdims=True)
        oc = o - mu
        var = jnp.mean(oc * oc, axis=-1, keepdims=True)
        y = oc * lax.rsqrt(var + LN_EPS) * gate_ref[rows, cols[hd]]
        mix_ref[rows, cols[hd]] = y.astype(bf16)
    return new_states


def _att_update(m_ref, acc_ref, u, s, v_t, ok):
    if ok is not None:
        s = jnp.where(ok, s, -jnp.inf)
    m = m_ref[u]
    m_new = jnp.maximum(m, jnp.max(s, axis=0, keepdims=True))
    p = jnp.exp2(s - m_new)
    m_ref[u] = m_new
    acc_ref[u] = jnp.exp2(m - m_new) * acc_ref[u] + _dot(v_t, p.astype(bf16))


def _mixer_kernel(qt_ref, km_ref, vt_ref, kmeta_ref, vmeta_ref, qr_ref, kr_ref, vr_ref, gate_ref,
                  krmeta_ref, vrmeta_ref, mix_ref, state_ref,
                  m_ref, acc_ref, s_ref, dmask_ref, qdec_ref, kdec_ref):
    qi = pl.program_id(1)

    @pl.when((pl.program_id(0) == 0) & (qi == 0))
    def _():
        for hd in range(RET_HEADS):
            dmask_ref[hd], qdec_ref[hd], kdec_ref[hd] = _retention_tables(RET_C, hd)

    @pl.when(qi == 0)
    def _():
        for hd in range(RET_HEADS):
            lo = hd * RET_DK
            lg = RET_LOG_GAMMA[hd]
            kd = (krmeta_ref[:, lo:lo + RET_DK].astype(f32)
                  * _row_pow(N_META, lg, N_META - 1.0, -1.0)).astype(bf16)
            state_ref[0, hd] = _dot_tn(kd, vrmeta_ref[:, lo:lo + RET_DV])

    tables = [(dmask_ref.at[hd], qdec_ref.at[hd], kdec_ref.at[hd]) for hd in range(RET_HEADS)]
    states = [state_ref[0, hd] for hd in range(RET_HEADS)]
    for c in range(ATT_TQ // RET_C):
        states = _retention_chunk(qr_ref, kr_ref, vr_ref, gate_ref, mix_ref,
                                  slice(c * RET_C, (c + 1) * RET_C), states, tables)
    for hd in range(RET_HEADS):
        state_ref[0, hd] = states[hd]

    n_qb = ATT_TQ // ATT_QB
    units = [(hd, qb) for hd in range(MLA_HEADS) for qb in range(n_qb)]

    def q_block(hd, qb):
        return qt_ref[qb, hd * MLA_QK_PAD:(hd + 1) * MLA_QK_PAD, :]

    def v_tile(hd, blk0, n_blk):
        rows = slice(hd * MLA_VT_ROWS, (hd + 1) * MLA_VT_ROWS)
        return jnp.concatenate([vt_ref[blk0 + j, rows, :] for j in range(n_blk)], axis=1)

    def scores(start, n_keys, u):
        hd, qb = units[u]
        k = km_ref[0, pl.ds(start, n_keys), hd * MLA_QK_PAD:(hd + 1) * MLA_QK_PAD]
        return _dot(k, q_block(hd, qb))

    s_ref[...] = scores(0, ATT_TK, 0)

    s_meta = [_dot(kmeta_ref[:, hd * MLA_QK_PAD:(hd + 1) * MLA_QK_PAD], q_block(hd, qb)) for hd, qb in units]
    p_meta = []
    for u, s0 in enumerate(s_meta):
        m = jnp.max(s0, axis=0, keepdims=True)
        m_ref[u] = m
        p_meta.append(jnp.exp2(s0 - m).astype(bf16))
    ones = jnp.ones((N_META, MLA_VT_ROWS - MLA_V), bf16)
    for u, (hd, qb) in enumerate(units):
        v_ext = jnp.concatenate([vmeta_ref[:, hd * MLA_V:(hd + 1) * MLA_V], ones], axis=1)
        acc_ref[u] = _dot_tn(v_ext, p_meta[u])

    def full_step(kt, _):
        start = pl.multiple_of(kt * ATT_TK, ATT_TK)
        s_cur = s_ref[...]
        for u, (hd, qb) in enumerate(units):
            if u + 1 < len(units):
                s_next = scores(start, ATT_TK, u + 1)
            else:
                s_next = scores(pl.multiple_of(start + ATT_TK, ATT_TK), ATT_TK, 0)
            _att_update(m_ref, acc_ref, u, s_cur, v_tile(hd, kt * (ATT_TK // ATT_QB), ATT_TK // ATT_QB), None)
            s_cur = s_next
        s_ref[...] = s_cur
        return 0

    lax.fori_loop(0, qi, full_step, 0)

    start = pl.multiple_of(qi * ATT_TQ, ATT_TQ)
    s_cur = s_ref[...]
    for u, (hd, qb) in enumerate(units):
        if u + 1 < len(units):
            s_next = scores(start, (units[u + 1][1] + 1) * ATT_QB, u + 1)
        k_chunk = lax.broadcasted_iota(jnp.int32, s_cur.shape, 0) // CHUNK
        q_chunk = lax.broadcasted_iota(jnp.int32, s_cur.shape, 1) // CHUNK + qb * (ATT_QB // CHUNK)
        _att_update(m_ref, acc_ref, u, s_cur, v_tile(hd, qi * n_qb, s_cur.shape[0] // ATT_QB),
                    k_chunk <= q_chunk)
        out = (acc_ref[u, :MLA_V, :] / acc_ref[u, MLA_V:MLA_V + 1, :]).T
        mix_ref[qb * ATT_QB:(qb + 1) * ATT_QB, RET_W + hd * MLA_V:RET_W + (hd + 1) * MLA_V] = out.astype(bf16)
        s_cur = s_next


def _mixer_call(qt, km, vt, kmeta, vmeta, qr, kr, vr, gate, krmeta, vrmeta, batch, seq):
    nq = seq // ATT_TQ
    n_qb = ATT_TQ // ATT_QB
    tile = lambda w: pl.BlockSpec((ATT_TQ, w), lambda b, i: (b * nq + i, 0))
    const = lambda a: pl.BlockSpec(a.shape, lambda b, i: (0,) * a.ndim)
    return pl.pallas_call(
        _mixer_kernel,
        grid=(batch, nq),
        in_specs=[pl.BlockSpec((n_qb, MLA_QW, ATT_QB), lambda b, i: (b * nq + i, 0, 0)),
                  pl.BlockSpec((1, seq, MLA_QW), lambda b, i: (b, 0, 0)),
                  pl.BlockSpec((seq // ATT_QB, MLA_HEADS * MLA_VT_ROWS, ATT_QB), lambda b, i: (b, 0, 0)),
                  const(kmeta), const(vmeta),
                  tile(RET_W), tile(RET_W), tile(RET_W), tile(RET_W),
                  const(krmeta), const(vrmeta)],
        out_specs=[tile(D_MODEL),
                   pl.BlockSpec((1, RET_HEADS, RET_DK, RET_DV), lambda b, i: (b, 0, 0, 0))],
        out_shape=[jax.ShapeDtypeStruct((batch * seq, D_MODEL), bf16),
                   jax.ShapeDtypeStruct((batch, RET_HEADS, RET_DK, RET_DV), f32)],
        scratch_shapes=[pltpu.VMEM((MLA_HEADS * n_qb, 1, ATT_QB), f32),
                        pltpu.VMEM((MLA_HEADS * n_qb, MLA_VT_ROWS, ATT_QB), f32),
                        pltpu.VMEM((ATT_TK, ATT_QB), f32),
                        pltpu.VMEM((RET_HEADS, RET_C, RET_C), f32),
                        pltpu.VMEM((RET_HEADS, RET_C, LANES), f32),
                        pltpu.VMEM((RET_HEADS, RET_C, LANES), f32)],
        compiler_params=pltpu.CompilerParams(dimension_semantics=("arbitrary", "arbitrary"),
                                             vmem_limit_bytes=VMEM_LIMIT),
        name="mixer",
    )(qt, km.reshape(batch, seq, MLA_QW), vt, kmeta, vmeta, qr, kr, vr, gate, krmeta, vrmeta)


def _smixer_kernel(qm_ref, knew_ref, vnew_ref, kmeta_ref, vmeta_ref, cckv_ref, ckpe_ref, w_ukv_ref,
                   qr_ref, kr_ref, vr_ref, gate_ref, s0_ref, mix_ref, state_ref):
    t = qr_ref.shape[0]
    states = _retention_chunk(qr_ref, kr_ref, vr_ref, gate_ref, mix_ref, slice(0, t),
                              [s0_ref[0, hd] for hd in range(RET_HEADS)],
                              [_retention_tables(t, hd) for hd in range(RET_HEADS)])
    for hd in range(RET_HEADS):
        state_ref[0, hd] = states[hd]

    kv_c = _dot(cckv_ref[0].astype(bf16), w_ukv_ref[...])
    kpe_c = ckpe_ref[0].astype(bf16)
    for hd in range(MLA_HEADS):
        klo = hd * MLA_QK_PAD
        vlo = hd * MLA_V
        q = qm_ref[:, klo:klo + MLA_QK_PAD]
        s_m = _dot_nt(q, kmeta_ref[:, klo:klo + MLA_QK_PAD]) * MLA_SCALE
        s_c = (_dot_nt(q[:, :LANES], kv_c[:, hd * MLA_NOPE:(hd + 1) * MLA_NOPE].astype(bf16))
               + _dot_nt(q[:, LANES:LANES + MLA_ROPE], kpe_c)) * MLA_SCALE
        s_n = _dot_nt(q, knew_ref[:, klo:klo + MLA_QK_PAD]) * MLA_SCALE
        m = jnp.maximum(jnp.maximum(jnp.max(s_m, axis=-1, keepdims=True),
                                    jnp.max(s_c, axis=-1, keepdims=True)),
                        jnp.max(s_n, axis=-1, keepdims=True))
        p_m = jnp.exp(s_m - m)
        p_c = jnp.exp(s_c - m)
        p_n = jnp.exp(s_n - m)
        l = (jnp.sum(p_m, axis=-1, keepdims=True) + jnp.sum(p_c, axis=-1, keepdims=True)
             + jnp.sum(p_n, axis=-1, keepdims=True))
        v_c = kv_c[:, MLA_HEADS * MLA_NOPE + vlo:MLA_HEADS * MLA_NOPE + vlo + MLA_V].astype(bf16)
        acc = (_dot(p_m.astype(bf16), vmeta_ref[:, vlo:vlo + MLA_V]) + _dot(p_c.astype(bf16), v_c)
               + _dot(p_n.astype(bf16), vnew_ref[:, vlo:vlo + MLA_V]))
        mix_ref[:, RET_W + vlo:RET_W + vlo + MLA_V] = (acc / l).astype(bf16)


def _smixer_call(qm, knew, vnew, kmeta, vmeta, cckv, ckpe, w_ukv, qr, kr, vr, gate, s0, t):
    db, past = cckv.shape[0], cckv.shape[1]
    tile = lambda w: pl.BlockSpec((t, w), lambda b: (b, 0))
    const = lambda a: pl.BlockSpec(a.shape, lambda b: (0,) * a.ndim)
    per = lambda *tail: pl.BlockSpec((1,) + tail, lambda b: (b,) + (0,) * len(tail))
    return pl.pallas_call(
        _smixer_kernel,
        grid=(db,),
        in_specs=[tile(MLA_QW), tile(MLA_QW), tile(MLA_VW), const(kmeta), const(vmeta),
                  per(past, MLA_KV_LORA), per(past, MLA_ROPE), const(w_ukv),
                  tile(RET_W), tile(RET_W), tile(RET_W), tile(RET_W),
                  per(RET_HEADS, RET_DK, RET_DV)],
        out_specs=[tile(D_MODEL), per(RET_HEADS, RET_DK, RET_DV)],
        out_shape=[jax.ShapeDtypeStruct((db * t, D_MODEL), bf16),
                   jax.ShapeDtypeStruct((db, RET_HEADS, RET_DK, RET_DV), f32)],
        compiler_params=pltpu.CompilerParams(dimension_semantics=("arbitrary",),
                                             vmem_limit_bytes=VMEM_LIMIT),
        name="smixer",
    )(qm, knew, vnew, kmeta, vmeta, cckv, ckpe, w_ukv, qr, kr, vr, gate, s0)


def _tail_kernel(sub, mix_ref, h_ref, w_mo_ref, ln2_g_ref, ln2_b_ref, w_in_ref, w_out_ref, ln3_g_ref, ln3_b_ref,
                 y_ref):
    tiles = [slice(j * sub, (j + 1) * sub) for j in range(mix_ref.shape[0] // sub)]
    mixes = [_dot(mix_ref[r, :], w_mo_ref[...]) for r in tiles]
    h2s, ys = [], []
    for r, mix in zip(tiles, mixes):
        h2 = _layer_norm(DN_ALPHA * h_ref[r, :] + mix, ln2_g_ref[...], ln2_b_ref[...])
        h2s.append(h2)
        ys.append(_swiglu(h2, w_in_ref, w_out_ref))
    for r, h2, y in zip(tiles, h2s, ys):
        y_ref[r, :] = _ffn_norm(h2, y, ln3_g_ref, ln3_b_ref)


def _tail_call(mix, h, wts, tm, sub):
    n = mix.shape[0]
    row = lambda w: pl.BlockSpec((tm, w), lambda i: (i, 0))
    return pl.pallas_call(
        functools.partial(_tail_kernel, sub),
        grid=(n // tm,),
        in_specs=[row(D_MODEL), row(D_MODEL)] + [_const_spec(w.shape) for w in wts],
        out_specs=row(D_MODEL),
        out_shape=jax.ShapeDtypeStruct((n, D_MODEL), f32),
        compiler_params=pltpu.CompilerParams(dimension_semantics=("arbitrary",),
                                             vmem_limit_bytes=VMEM_LIMIT),
        name="tail",
    )(mix, h, *wts)


def _rope_tables(pos):
    posf = pos.astype(f32)[:, None]
    inv_r = ROPE_BASE ** (-jnp.arange(0, RET_DK, 2, dtype=f32) / RET_DK)
    ang = posf * inv_r[None, :]
    inv_m = ROPE_BASE ** (-jnp.arange(0, MLA_ROPE, 2, dtype=f32) / MLA_ROPE)
    angm = posf * inv_m[None, :]
    return jnp.cos(ang), jnp.sin(ang), jnp.cos(angm), jnp.sin(angm)


def kernel(x_prompt, x_sample, cache_mla_ckv, cache_mla_kpe, state_ret, meta_tokens,
           ffn1_w_in, ffn1_w_out, ln1_g, ln1_b, w_mix_in, ret_gn_g, mla_q_norm_g, mla_w_uq,
           mla_kv_norm_g, mla_w_ukv, w_mix_out, ln2_g, ln2_b, ffn2_w_in, ffn2_w_out, ln3_g, ln3_b):
    B, S, D = x_prompt.shape
    DB, T, _ = x_sample.shape
    P = cache_mla_ckv.shape[1]
    assert S % ATT_TQ == 0 and S % FRONT_TM == 0 and ATT_TQ % RET_C == 0
    row = lambda a: a.reshape(1, -1).astype(f32)

    w_mix = w_mix_in[:, :OFF_KPE].astype(bf16)
    w_kpe = jnp.pad(w_mix_in[:, OFF_KPE:], ((0, 0), (0, LANES - MLA_ROPE))).astype(bf16)
    w_uq = mla_w_uq.reshape(MLA_Q_LORA, MLA_HEADS, MLA_NOPE + MLA_ROPE)
    w_uq = jnp.pad(w_uq, ((0, 0), (0, 0), (0, MLA_QK_PAD - MLA_NOPE - MLA_ROPE)))
    w_uq = w_uq.reshape(MLA_Q_LORA, MLA_QW).astype(bf16)
    w_ukv = mla_w_ukv.reshape(MLA_KV_LORA, MLA_HEADS, MLA_NOPE + MLA_V)
    w_uk = w_ukv[:, :, :MLA_NOPE].reshape(MLA_KV_LORA, -1).astype(bf16)
    w_uv = w_ukv[:, :, MLA_NOPE:].reshape(MLA_KV_LORA, -1).astype(bf16)
    shared_w = [ffn1_w_in.astype(bf16), ffn1_w_out.astype(bf16), row(ln1_g), row(ln1_b), w_mix, w_kpe,
                row(ret_gn_g), row(mla_q_norm_g)]
    front_w = shared_w + [w_uq, row(mla_kv_norm_g), w_uk, w_uv]
    front_t_w = shared_w + [w_uq.T, row(mla_kv_norm_g), w_uk, w_uv.T]
    tail_w = [w_mix_out.astype(bf16), row(ln2_g), row(ln2_b), ffn2_w_in.astype(bf16),
              ffn2_w_out.astype(bf16), row(ln3_g), row(ln3_b)]

    tabs_p = _rope_tables(N_META + jnp.arange(S))
    tab_t_p = jnp.concatenate([tabs_p[2].T, tabs_p[3].T], axis=0)
    (h_p, qr_p, kr_p, vr_p, gr_p, qt_p, km_p, vt_p, ckv_p, kpe_p) = _front_call(
        x_prompt.reshape(B * S, D), tabs_p, tab_t_p, front_t_w, FRONT_TM, FRONT_SUB, S // FRONT_TM)
    n_s = DB * T
    x_small = jnp.concatenate([x_sample.reshape(n_s, D), meta_tokens.astype(x_sample.dtype)], axis=0)
    pos_small = jnp.concatenate([jnp.tile(N_META + P + jnp.arange(T), DB), jnp.arange(N_META)])
    (h_s, qr_s, kr_s, vr_s, gr_s, qm_s, km_s, vm_s, ckv_s, kpe_s) = _front_call(
        x_small, _rope_tables(pos_small), None, front_w, n_s + N_META, n_s + N_META, 1)
    kmeta, vmeta = km_s[n_s:], vm_s[n_s:]
    krmeta, vrmeta = kr_s[n_s:], vr_s[n_s:]

    mix_p, p_state = _mixer_call(qt_p, km_p, vt_p, kmeta, vmeta, qr_p, kr_p, vr_p, gr_p,
                                 krmeta, vrmeta, B, S)
    mix_s, s_state = _smixer_call(qm_s[:n_s], km_s[:n_s], vm_s[:n_s], kmeta, vmeta, cache_mla_ckv, cache_mla_kpe,
                                  jnp.concatenate([w_uk, w_uv], axis=1), qr_s[:n_s], kr_s[:n_s], vr_s[:n_s],
                                  gr_s[:n_s], state_ret, T)

    y_p = _tail_call(mix_p, h_p, tail_w, FRONT_TM, FRONT_SUB)
    y_s = _tail_call(mix_s, h_s[:n_s], tail_w, n_s, n_s // 2)

    meta_ckv = jnp.broadcast_to(ckv_s[n_s:][None], (B, N_META, MLA_KV_LORA))
    meta_kpe = jnp.broadcast_to(kpe_s[n_s:][None], (B, N_META, MLA_ROPE))
    p_ckv = jnp.concatenate([meta_ckv, ckv_p.reshape(B, S, MLA_KV_LORA)], axis=1)
    p_kpe = jnp.concatenate([meta_kpe, kpe_p.reshape(B, S, MLA_ROPE)], axis=1)
    return (y_p.reshape(B, S, D), y_s.reshape(DB, T, D), p_ckv, p_kpe, p_state.astype(x_prompt.dtype),
            ckv_s[:n_s].reshape(DB, T, MLA_KV_LORA), kpe_s[:n_s].reshape(DB, T, MLA_ROPE),
            s_state.astype(state_ret.dtype))
```

```python
import functools
import math

import jax
import jax.numpy as jnp
from jax import lax
from jax.experimental import pallas as pl
from jax.experimental.pallas import tpu as pltpu

D_MODEL = 1024
DEPTH = 1
CHUNK = 64
N_META = 16
RET_HEADS = 4
RET_DK = 128
RET_DV = 128
MLA_HEADS = 4
MLA_NOPE = 128
MLA_ROPE = 64
MLA_V = 128
MLA_Q_LORA = 256
MLA_KV_LORA = 128
D_FF = 2816
ROPE_BASE = 10000.0
LN_EPS = 1e-5
RMS_EPS = 1e-6
DN_ALPHA = (2 * DEPTH) ** 0.25

OFF_KR = RET_HEADS * RET_DK
OFF_VR = 2 * RET_HEADS * RET_DK
OFF_GR = OFF_VR + RET_HEADS * RET_DV
OFF_CQ = OFF_GR + RET_HEADS * RET_DV
OFF_CKV = OFF_CQ + MLA_Q_LORA
OFF_KPE = OFF_CKV + MLA_KV_LORA
D_IN = OFF_KPE + MLA_ROPE

LANES = 128
MLA_QK_PAD = 2 * LANES
RET_W = RET_HEADS * RET_DK
MLA_QW = MLA_HEADS * MLA_QK_PAD
MLA_VW = MLA_HEADS * MLA_V
BF16_SUBLANES = 16
MLA_VT_ROWS = MLA_V + BF16_SUBLANES

FRONT_SUB = 256
FRONT_TM = 2 * FRONT_SUB
TAIL_TM = 4 * FRONT_SUB
ATT_QB = FRONT_TM
ATT_TQ = ATT_QB
ATT_TK = ATT_QB
RET_C = 256
VMEM_LIMIT = 56 * 1024 * 1024

MLA_SCALE = (MLA_NOPE + MLA_ROPE) ** -0.5
MLA_SCALE_LOG2E = MLA_SCALE * math.log2(math.e)
RET_LOG_GAMMA = tuple(math.log(1.0 - 2.0 ** (-5.0 - h)) for h in range(RET_HEADS))

f32 = jnp.float32
bf16 = jnp.bfloat16


def _dot(a, b):
    return jnp.dot(a, b, preferred_element_type=f32)


def _dot_nt(a, b):
    return lax.dot_general(a, b, (((1,), (1,)), ((), ())), preferred_element_type=f32)


def _dot_tn(a, b):
    return lax.dot_general(a, b, (((0,), (0,)), ((), ())), preferred_element_type=f32)


def _layer_norm(x, g, b):
    mu = jnp.mean(x, axis=-1, keepdims=True)
    xc = x - mu
    var = jnp.mean(xc * xc, axis=-1, keepdims=True)
    return xc * lax.rsqrt(var + LN_EPS) * g + b


def _rms_norm(x, g):
    return x * lax.rsqrt(jnp.mean(x * x, axis=-1, keepdims=True) + RMS_EPS) * g


def _silu(x):
    return x / (1.0 + jnp.exp(-x))


def _swiglu(x, w_in_ref, w_out_ref):
    xb = x.astype(bf16)
    hg = _dot(xb, w_in_ref[:, :D_FF])
    hu = _dot(xb, w_in_ref[:, D_FF:])
    return _dot((_silu(hg) * hu).astype(bf16), w_out_ref[...])


def _ffn_norm(x, y, g_ref, b_ref):
    return _layer_norm(DN_ALPHA * x + 0.5 * y, g_ref[...], b_ref[...])


def _front_kernel(transposed, sub, x_ref, cr_ref, sr_ref, cm_ref, sm_ref, *refs):
    if transposed:
        tab_t_ref, refs = refs[0], refs[1:]
    (w_in_ref, w_out_ref, ln_g_ref, ln_b_ref, w_mix_ref, w_kpe_ref, gn_ref, qn_g_ref, w_uq_ref, kvn_g_ref,
     w_uk_ref, w_uv_ref,
     h_ref, qr_ref, kr_ref, vr_ref, gate_ref, qm_ref, km_ref, vm_ref, ckv_ref, kpe_ref) = refs
    n_sub = x_ref.shape[0] // sub
    tiles = [slice(j * sub, (j + 1) * sub) for j in range(n_sub)]

    xs = [x_ref[r, :] for r in tiles]
    ys = [_swiglu(x, w_in_ref, w_out_ref) for x in xs]
    ps = []
    for r, x, y in zip(tiles, xs, ys):
        h = _ffn_norm(x, y, ln_g_ref, ln_b_ref)
        h_ref[r, :] = h
        hb = h.astype(bf16)
        ps.append((_dot(hb, w_mix_ref[...]), _dot(hb, w_kpe_ref[...])))

    for j, (r, (p, p_kpe)) in enumerate(zip(tiles, ps)):
        c, s, cm, sm = cr_ref[r, :], sr_ref[r, :], cm_ref[r, :], sm_ref[r, :]
        z_q, z_h = jnp.zeros_like(cm), jnp.zeros_like(c)
        c_r = jnp.concatenate([c, c], axis=1)
        s_r = jnp.concatenate([-s, s], axis=1)
        c_m = jnp.concatenate([cm, cm, z_h], axis=1)
        s_ma = jnp.concatenate([-sm, z_q, z_h], axis=1)
        s_mb = jnp.concatenate([z_q, sm, z_h], axis=1)

        def rope_ret(xh):
            return xh * c_r + pltpu.roll(xh, 64, 1) * s_r

        def rope_mla(xh):
            return xh * c_m + pltpu.roll(xh, 96, 1) * s_ma + pltpu.roll(xh, 32, 1) * s_mb

        cq = _rms_norm(p[:, OFF_CQ:OFF_CKV], qn_g_ref[...]).astype(bf16)
        ckv = _rms_norm(p[:, OFF_CKV:OFF_KPE], kvn_g_ref[...])
        ckv_ref[r, :] = ckv
        ckv_b = ckv.astype(bf16)
        k_nope = _dot(ckv_b, w_uk_ref[...])
        if transposed:
            q_t = _dot_nt(w_uq_ref[...], cq) * MLA_SCALE_LOG2E
            v_t = _dot_nt(w_uv_ref[...], ckv_b)
        else:
            q = _dot(cq, w_uq_ref[...])
            v = _dot(ckv_b, w_uv_ref[...])

        for hd in range(RET_HEADS):
            lo = hd * RET_DK
            qr_ref[r, lo:lo + RET_DK] = rope_ret(p[:, lo:lo + RET_DK]).astype(bf16)
            kr_ref[r, lo:lo + RET_DK] = (rope_ret(p[:, OFF_KR + lo:OFF_KR + lo + RET_DK])
                                         * RET_DK ** -0.5).astype(bf16)
        vr_ref[r, :] = p[:, OFF_VR:OFF_GR].astype(bf16)
        gate_ref[r, :] = gn_ref[...] * _silu(p[:, OFF_GR:OFF_CQ])
        kpe = rope_mla(p_kpe)
        kpe_ref[r, :] = kpe[:, :MLA_ROPE]
        kpe_b = kpe.astype(bf16)
        for hd in range(MLA_HEADS):
            lo = hd * MLA_QK_PAD
            km_ref[r, lo:lo + LANES] = k_nope[:, hd * MLA_NOPE:(hd + 1) * MLA_NOPE].astype(bf16)
            km_ref[r, lo + LANES:lo + 2 * LANES] = kpe_b

        if transposed:
            half = MLA_ROPE // 2
            c_t = tab_t_ref[:half, r]
            s_t = tab_t_ref[half:, r]
            for hd in range(MLA_HEADS):
                lo = hd * MLA_QK_PAD
                r0 = lo + MLA_NOPE
                x1 = q_t[r0:r0 + half, :]
                x2 = q_t[r0 + half:r0 + 2 * half, :]
                qm_ref[0, lo:r0, r] = q_t[lo:r0, :].astype(bf16)
                qm_ref[0, r0:r0 + half, r] = (x1 * c_t - x2 * s_t).astype(bf16)
                qm_ref[0, r0 + half:r0 + 2 * half, r] = (x1 * s_t + x2 * c_t).astype(bf16)
                qm_ref[0, r0 + 2 * half:lo + MLA_QK_PAD, r] = q_t[r0 + 2 * half:lo + MLA_QK_PAD, :].astype(bf16)
            ones = jnp.ones((MLA_VT_ROWS - MLA_V, v_t.shape[1]), bf16)
            for hd in range(MLA_HEADS):
                vm_ref[0, hd * MLA_VT_ROWS:hd * MLA_VT_ROWS + MLA_V, r] = v_t[hd * MLA_V:(hd + 1) * MLA_V, :].astype(bf16)
                vm_ref[0, hd * MLA_VT_ROWS + MLA_V:(hd + 1) * MLA_VT_ROWS, r] = ones
        else:
            for hd in range(MLA_HEADS):
                lo = hd * MLA_QK_PAD
                qm_ref[r, lo:lo + LANES] = q[:, lo:lo + LANES].astype(bf16)
                qm_ref[r, lo + LANES:lo + 2 * LANES] = rope_mla(q[:, lo + LANES:lo + 2 * LANES]).astype(bf16)
            vm_ref[r, :] = v.astype(bf16)


def _const_spec(shape):
    nd = len(shape)
    return pl.BlockSpec(shape, lambda *_: (0,) * nd, pipeline_mode=pl.Buffered(1))


def _front_call(x, tabs, tab_t, wts, tm, sub, tab_blocks):
    n = x.shape[0]
    transposed = tab_t is not None
    row = lambda w: pl.BlockSpec((tm, w), lambda i: (i, 0))
    col = lambda w: pl.BlockSpec((1, w, tm), lambda i: (i, 0, 0))
    in_specs = [row(D_MODEL)] + [pl.BlockSpec((tm, t.shape[1]), lambda i: (i % tab_blocks, 0)) for t in tabs]
    args = [x, *tabs]
    if transposed:
        in_specs.append(pl.BlockSpec((MLA_ROPE, tm), lambda i: (0, i % tab_blocks)))
        args.append(tab_t)
    in_specs += [_const_spec(w.shape) for w in wts]
    rows = lambda w, dt: (row(w), jax.ShapeDtypeStruct((n, w), dt))
    cols = lambda w, dt: (col(w), jax.ShapeDtypeStruct((n // tm, w, tm), dt))
    qv = cols if transposed else rows
    outs = [rows(D_MODEL, f32), rows(RET_W, bf16), rows(RET_W, bf16), rows(RET_W, bf16), rows(RET_W, f32),
            qv(MLA_QW, bf16), rows(MLA_QW, bf16),
            cols(MLA_HEADS * MLA_VT_ROWS, bf16) if transposed else rows(MLA_VW, bf16),
            rows(MLA_KV_LORA, f32), rows(MLA_ROPE, f32)]
    return pl.pallas_call(
        functools.partial(_front_kernel, transposed, sub),
        grid=(n // tm,),
        in_specs=in_specs,
        out_specs=[o[0] for o in outs],
        out_shape=[o[1] for o in outs],
        compiler_params=pltpu.CompilerParams(dimension_semantics=("arbitrary",),
                                             vmem_limit_bytes=VMEM_LIMIT),
        name="front_t" if transposed else "front",
    )(*args, *wts)


def _decay_mask(c, lg):
    r = lax.broadcasted_iota(jnp.int32, (c, c), 0)
    k = lax.broadcasted_iota(jnp.int32, (c, c), 1)
    rel = (r - k).astype(f32)
    return jnp.where(rel >= 0.0, jnp.exp(lg * jnp.maximum(rel, 0.0)), 0.0)


def _row_pow(c, lg, offset, sign):
    i = lax.broadcasted_iota(jnp.int32, (c, LANES), 0).astype(f32)
    return jnp.exp(lg * (offset + sign * i))


def _retention_tables(c, hd):
    lg = RET_LOG_GAMMA[hd]
    return _decay_mask(c, lg), _row_pow(c, lg, 1.0, 1.0), _row_pow(c, lg, c - 1.0, -1.0)


def _retention_chunk(qr_ref, kr_ref, vr_ref, gate_ref, mix_ref, rows, states, tables):
    c = rows.stop - rows.start
    heads = range(RET_HEADS)
    cols = [slice(hd * RET_DK, (hd + 1) * RET_DK) for hd in heads]
    q = [qr_ref[rows, cols[hd]] for hd in heads]
    k = [kr_ref[rows, cols[hd]] for hd in heads]
    v = [vr_ref[rows, cols[hd]] for hd in heads]
    scores = [_dot_nt(q[hd], k[hd]) for hd in heads]
    cross = [_dot(q[hd], states[hd].astype(bf16)) for hd in heads]
    kv = [_dot_tn((k[hd].astype(f32) * tables[hd][2][...]).astype(bf16), v[hd]) for hd in heads]
    new_states = [math.exp(RET_LOG_GAMMA[hd] * c) * states[hd] + kv[hd] for hd in heads]

    def finish():
        inner = [_dot((scores[hd] * tables[hd][0][...]).astype(bf16), v[hd]) for hd in heads]
        for hd in heads:
            o = inner[hd] + cross[hd] * tables[hd][1][...]
            mu = jnp.mean(o, axis=-1, keepdims=True)
            oc = o - mu
            var = jnp.mean(oc * oc, axis=-1, keepdims=True)
            y = oc * lax.rsqrt(var + LN_EPS) * gate_ref[rows, cols[hd]]
            mix_ref[rows, cols[hd]] = y.astype(bf16)

    return new_states, finish


def _att_update(m_ref, acc_ref, u, s, v_t, ok):
    if ok is not None:
        s = jnp.where(ok, s, -jnp.inf)
    m = m_ref[u]
    m_new = jnp.maximum(m, jnp.max(s, axis=0, keepdims=True))
    p = jnp.exp2(s - m_new)
    m_ref[u] = m_new
    acc_ref[u] = jnp.exp2(m - m_new) * acc_ref[u] + _dot(v_t, p.astype(bf16))


def _mixer_kernel(qt_ref, km_ref, vt_ref, kmeta_ref, vmeta_ref, qr_ref, kr_ref, vr_ref, gate_ref,
                  krmeta_ref, vrmeta_ref, mix_ref, state_ref,
                  m_ref, acc_ref, s_ref, dmask_ref, qdec_ref, kdec_ref):
    qi = pl.program_id(1)

    @pl.when((pl.program_id(0) == 0) & (qi == 0))
    def _():
        for hd in range(RET_HEADS):
            dmask_ref[hd], qdec_ref[hd], kdec_ref[hd] = _retention_tables(RET_C, hd)

    @pl.when(qi == 0)
    def _():
        for hd in range(RET_HEADS):
            lo = hd * RET_DK
            lg = RET_LOG_GAMMA[hd]
            kd = (krmeta_ref[:, lo:lo + RET_DK].astype(f32)
                  * _row_pow(N_META, lg, N_META - 1.0, -1.0)).astype(bf16)
            state_ref[0, hd] = _dot_tn(kd, vrmeta_ref[:, lo:lo + RET_DV])

    tables = [(dmask_ref.at[hd], qdec_ref.at[hd], kdec_ref.at[hd]) for hd in range(RET_HEADS)]
    states = [state_ref[0, hd] for hd in range(RET_HEADS)]
    finish_prev = None
    for c in range(ATT_TQ // RET_C):
        states, finish_chunk = _retention_chunk(qr_ref, kr_ref, vr_ref, gate_ref, mix_ref,
                                                slice(c * RET_C, (c + 1) * RET_C), states, tables)
        if finish_prev is not None:
            finish_prev()
        finish_prev = finish_chunk
    finish_prev()
    for hd in range(RET_HEADS):
        state_ref[0, hd] = states[hd]

    n_qb = ATT_TQ // ATT_QB
    units = [(hd, qb) for hd in range(MLA_HEADS) for qb in range(n_qb)]

    def q_block(hd, qb):
        return qt_ref[qb, hd * MLA_QK_PAD:(hd + 1) * MLA_QK_PAD, :]

    def v_tile(hd, blk0, n_blk):
        rows = slice(hd * MLA_VT_ROWS, (hd + 1) * MLA_VT_ROWS)
        return jnp.concatenate([vt_ref[blk0 + j, rows, :] for j in range(n_blk)], axis=1)

    def scores(start, n_keys, u):
        hd, qb = units[u]
        k = km_ref[0, pl.ds(start, n_keys), hd * MLA_QK_PAD:(hd + 1) * MLA_QK_PAD]
        return _dot(k, q_block(hd, qb))

    s_ref[...] = scores(0, ATT_TK, 0)

    s_meta = [_dot(kmeta_ref[:, hd * MLA_QK_PAD:(hd + 1) * MLA_QK_PAD], q_block(hd, qb)) for hd, qb in units]
    p_meta = []
    for u, s0 in enumerate(s_meta):
        m = jnp.max(s0, axis=0, keepdims=True)
        m_ref[u] = m
        p_meta.append(jnp.exp2(s0 - m).astype(bf16))
    ones = jnp.ones((N_META, MLA_VT_ROWS - MLA_V), bf16)
    for u, (hd, qb) in enumerate(units):
        v_ext = jnp.concatenate([vmeta_ref[:, hd * MLA_V:(hd + 1) * MLA_V], ones], axis=1)
        acc_ref[u] = _dot_tn(v_ext, p_meta[u])

    def full_step(kt, _):
        start = pl.multiple_of(kt * ATT_TK, ATT_TK)
        s_cur = s_ref[...]
        for u, (hd, qb) in enumerate(units):
            if u + 1 < len(units):
                s_next = scores(start, ATT_TK, u + 1)
            else:
                s_next = scores(pl.multiple_of(start + ATT_TK, ATT_TK), ATT_TK, 0)
            _att_update(m_ref, acc_ref, u, s_cur, v_tile(hd, kt * (ATT_TK // ATT_QB), ATT_TK // ATT_QB), None)
            s_cur = s_next
        s_ref[...] = s_cur
        return 0

    lax.fori_loop(0, qi, full_step, 0)

    start = pl.multiple_of(qi * ATT_TQ, ATT_TQ)
    s_cur = s_ref[...]
    for u, (hd, qb) in enumerate(units):
        if u + 1 < len(units):
            s_next = scores(start, (units[u + 1][1] + 1) * ATT_QB, u + 1)
        k_chunk = lax.broadcasted_iota(jnp.int32, s_cur.shape, 0) // CHUNK
        q_chunk = lax.broadcasted_iota(jnp.int32, s_cur.shape, 1) // CHUNK + qb * (ATT_QB // CHUNK)
        _att_update(m_ref, acc_ref, u, s_cur, v_tile(hd, qi * n_qb, s_cur.shape[0] // ATT_QB),
                    k_chunk <= q_chunk)
        out = (acc_ref[u, :MLA_V, :] / acc_ref[u, MLA_V:MLA_V + 1, :]).T
        mix_ref[qb * ATT_QB:(qb + 1) * ATT_QB, RET_W + hd * MLA_V:RET_W + (hd + 1) * MLA_V] = out.astype(bf16)
        s_cur = s_next


def _mixer_call(qt, km, vt, kmeta, vmeta, qr, kr, vr, gate, krmeta, vrmeta, batch, seq):
    nq = seq // ATT_TQ
    n_qb = ATT_TQ // ATT_QB
    tile = lambda w: pl.BlockSpec((ATT_TQ, w), lambda b, i: (b * nq + i, 0))
    const = lambda a: pl.BlockSpec((N_META, a.shape[1]), lambda b, i: (a.shape[0] // N_META - 1, 0))
    return pl.pallas_call(
        _mixer_kernel,
        grid=(batch, nq),
        in_specs=[pl.BlockSpec((n_qb, MLA_QW, ATT_QB), lambda b, i: (b * nq + i, 0, 0)),
                  pl.BlockSpec((1, seq, MLA_QW), lambda b, i: (b, 0, 0)),
                  pl.BlockSpec((seq // ATT_QB, MLA_HEADS * MLA_VT_ROWS, ATT_QB), lambda b, i: (b, 0, 0)),
                  const(kmeta), const(vmeta),
                  tile(RET_W), tile(RET_W), tile(RET_W), tile(RET_W),
                  const(krmeta), const(vrmeta)],
        out_specs=[tile(D_MODEL),
                   pl.BlockSpec((1, RET_HEADS, RET_DK, RET_DV), lambda b, i: (b, 0, 0, 0))],
        out_shape=[jax.ShapeDtypeStruct((batch * seq, D_MODEL), bf16),
                   jax.ShapeDtypeStruct((batch, RET_HEADS, RET_DK, RET_DV), f32)],
        scratch_shapes=[pltpu.VMEM((MLA_HEADS * n_qb, 1, ATT_QB), f32),
                        pltpu.VMEM((MLA_HEADS * n_qb, MLA_VT_ROWS, ATT_QB), f32),
                        pltpu.VMEM((ATT_TK, ATT_QB), f32),
                        pltpu.VMEM((RET_HEADS, RET_C, RET_C), f32),
                        pltpu.VMEM((RET_HEADS, RET_C, LANES), f32),
                        pltpu.VMEM((RET_HEADS, RET_C, LANES), f32)],
        compiler_params=pltpu.CompilerParams(dimension_semantics=("arbitrary", "arbitrary"),
                                             vmem_limit_bytes=VMEM_LIMIT),
        name="mixer",
    )(qt, km.reshape(batch, seq, MLA_QW), vt, kmeta, vmeta, qr, kr, vr, gate, krmeta, vrmeta)


def _smixer_kernel(qm_ref, knew_ref, vnew_ref, kmeta_ref, vmeta_ref, cckv_ref, ckpe_ref, w_ukv_ref,
                   qr_ref, kr_ref, vr_ref, gate_ref, s0_ref, mix_ref, state_ref):
    t = qr_ref.shape[0]
    states, finish_retention = _retention_chunk(qr_ref, kr_ref, vr_ref, gate_ref, mix_ref, slice(0, t),
                                                [s0_ref[0, hd] for hd in range(RET_HEADS)],
                                                [_retention_tables(t, hd) for hd in range(RET_HEADS)])
    for hd in range(RET_HEADS):
        state_ref[0, hd] = states[hd]
    finish_retention()

    heads = range(MLA_HEADS)
    kcols = [slice(hd * MLA_QK_PAD, (hd + 1) * MLA_QK_PAD) for hd in heads]
    vcols = [slice(hd * MLA_V, (hd + 1) * MLA_V) for hd in heads]
    kv_c = _dot(cckv_ref[0].astype(bf16), w_ukv_ref[...])
    kpe_c = ckpe_ref[0].astype(bf16)
    q = [qm_ref[:, kcols[hd]] for hd in heads]
    s_m = [_dot_nt(q[hd], kmeta_ref[:, kcols[hd]]) for hd in heads]
    s_n = [_dot_nt(q[hd], knew_ref[:, kcols[hd]]) for hd in heads]
    s_r = [_dot_nt(q[hd][:, LANES:LANES + MLA_ROPE], kpe_c) for hd in heads]
    s_c = [_dot_nt(q[hd][:, :LANES], kv_c[:, hd * MLA_NOPE:(hd + 1) * MLA_NOPE].astype(bf16)) + s_r[hd]
           for hd in heads]
    probs, denom = [], []
    for hd in heads:
        parts = [s_m[hd] * MLA_SCALE, s_c[hd] * MLA_SCALE, s_n[hd] * MLA_SCALE]
        m = functools.reduce(jnp.maximum, [jnp.max(s, axis=-1, keepdims=True) for s in parts])
        ps = [jnp.exp(s - m) for s in parts]
        denom.append(functools.reduce(jnp.add, [jnp.sum(p, axis=-1, keepdims=True) for p in ps]))
        probs.append([p.astype(bf16) for p in ps])
    for hd in heads:
        v_c = kv_c[:, MLA_HEADS * MLA_NOPE + hd * MLA_V:MLA_HEADS * MLA_NOPE + (hd + 1) * MLA_V].astype(bf16)
        acc = (_dot(probs[hd][0], vmeta_ref[:, vcols[hd]]) + _dot(probs[hd][1], v_c)
               + _dot(probs[hd][2], vnew_ref[:, vcols[hd]]))
        mix_ref[:, RET_W + hd * MLA_V:RET_W + (hd + 1) * MLA_V] = (acc / denom[hd]).astype(bf16)


def _smixer_call(qm, km, vm, cckv, ckpe, w_ukv, qr, kr, vr, gate, s0, t):
    db, past = cckv.shape[0], cckv.shape[1]
    tile = lambda w: pl.BlockSpec((t, w), lambda b: (b, 0))
    meta = lambda w: pl.BlockSpec((N_META, w), lambda b: (db * t // N_META, 0))
    const = lambda a: pl.BlockSpec(a.shape, lambda b: (0,) * a.ndim)
    per = lambda *tail: pl.BlockSpec((1,) + tail, lambda b: (b,) + (0,) * len(tail))
    return pl.pallas_call(
        _smixer_kernel,
        grid=(db,),
        in_specs=[tile(MLA_QW), tile(MLA_QW), tile(MLA_VW), meta(MLA_QW), meta(MLA_VW),
                  per(past, MLA_KV_LORA), per(past, MLA_ROPE), const(w_ukv),
                  tile(RET_W), tile(RET_W), tile(RET_W), tile(RET_W),
                  per(RET_HEADS, RET_DK, RET_DV)],
        out_specs=[tile(D_MODEL), per(RET_HEADS, RET_DK, RET_DV)],
        out_shape=[jax.ShapeDtypeStruct((db * t, D_MODEL), bf16),
                   jax.ShapeDtypeStruct((db, RET_HEADS, RET_DK, RET_DV), f32)],
        compiler_params=pltpu.CompilerParams(dimension_semantics=("arbitrary",),
                                             vmem_limit_bytes=VMEM_LIMIT),
        name="smixer",
    )(qm, km, vm, km, vm, cckv, ckpe, w_ukv, qr, kr, vr, gate, s0)


def _tail_kernel(sub, mix_ref, h_ref, w_mo_ref, ln2_g_ref, ln2_b_ref, w_in_ref, w_out_ref, ln3_g_ref, ln3_b_ref,
                 y_ref):
    tiles = [slice(j * sub, (j + 1) * sub) for j in range(mix_ref.shape[0] // sub)]
    mixes = [_dot(mix_ref[r, :], w_mo_ref[...]) for r in tiles]
    h2s, ys = [], []
    for r, mix in zip(tiles, mixes):
        h2 = _layer_norm(DN_ALPHA * h_ref[r, :] + mix, ln2_g_ref[...], ln2_b_ref[...])
        h2s.append(h2)
        ys.append(_swiglu(h2, w_in_ref, w_out_ref))
    for r, h2, y in zip(tiles, h2s, ys):
        y_ref[r, :] = _ffn_norm(h2, y, ln3_g_ref, ln3_b_ref)


def _tail_call(mix, h, wts, tm, sub):
    n = mix.shape[0]
    row = lambda w: pl.BlockSpec((tm, w), lambda i: (i, 0))
    return pl.pallas_call(
        functools.partial(_tail_kernel, sub),
        grid=(n // tm,),
        in_specs=[row(D_MODEL), row(D_MODEL)] + [_const_spec(w.shape) for w in wts],
        out_specs=row(D_MODEL),
        out_shape=jax.ShapeDtypeStruct((n, D_MODEL), f32),
        compiler_params=pltpu.CompilerParams(dimension_semantics=("arbitrary",),
                                             vmem_limit_bytes=VMEM_LIMIT),
        name="tail",
    )(mix, h, *wts)


def _rope_tables(pos):
    posf = pos.astype(f32)[:, None]
    inv_r = ROPE_BASE ** (-jnp.arange(0, RET_DK, 2, dtype=f32) / RET_DK)
    ang = posf * inv_r[None, :]
    inv_m = ROPE_BASE ** (-jnp.arange(0, MLA_ROPE, 2, dtype=f32) / MLA_ROPE)
    angm = posf * inv_m[None, :]
    return jnp.cos(ang), jnp.sin(ang), jnp.cos(angm), jnp.sin(angm)


def kernel(x_prompt, x_sample, cache_mla_ckv, cache_mla_kpe, state_ret, meta_tokens,
           ffn1_w_in, ffn1_w_out, ln1_g, ln1_b, w_mix_in, ret_gn_g, mla_q_norm_g, mla_w_uq,
           mla_kv_norm_g, mla_w_ukv, w_mix_out, ln2_g, ln2_b, ffn2_w_in, ffn2_w_out, ln3_g, ln3_b):
    B, S, D = x_prompt.shape
    DB, T, _ = x_sample.shape
    P = cache_mla_ckv.shape[1]
    assert S % ATT_TQ == 0 and S % FRONT_TM == 0 and (B * S) % TAIL_TM == 0 and ATT_TQ % RET_C == 0
    row = lambda a: a.reshape(1, -1).astype(f32)

    w_mix = w_mix_in[:, :OFF_KPE].astype(bf16)
    w_kpe = jnp.pad(w_mix_in[:, OFF_KPE:], ((0, 0), (0, LANES - MLA_ROPE))).astype(bf16)
    w_uq = mla_w_uq.reshape(MLA_Q_LORA, MLA_HEADS, MLA_NOPE + MLA_ROPE)
    w_uq = jnp.pad(w_uq, ((0, 0), (0, 0), (0, MLA_QK_PAD - MLA_NOPE - MLA_ROPE)))
    w_uq = w_uq.reshape(MLA_Q_LORA, MLA_QW).astype(bf16)
    w_ukv = mla_w_ukv.reshape(MLA_KV_LORA, MLA_HEADS, MLA_NOPE + MLA_V)
    w_uk = w_ukv[:, :, :MLA_NOPE].reshape(MLA_KV_LORA, -1).astype(bf16)
    w_uv = w_ukv[:, :, MLA_NOPE:].reshape(MLA_KV_LORA, -1).astype(bf16)
    shared_w = [ffn1_w_in.astype(bf16), ffn1_w_out.astype(bf16), row(ln1_g), row(ln1_b), w_mix, w_kpe,
                row(ret_gn_g), row(mla_q_norm_g)]
    front_w = shared_w + [w_uq, row(mla_kv_norm_g), w_uk, w_uv]
    front_t_w = shared_w + [w_uq.T, row(mla_kv_norm_g), w_uk, w_uv.T]
    tail_w = [w_mix_out.astype(bf16), row(ln2_g), row(ln2_b), ffn2_w_in.astype(bf16),
              ffn2_w_out.astype(bf16), row(ln3_g), row(ln3_b)]

    tabs_p = _rope_tables(N_META + jnp.arange(S))
    tab_t_p = jnp.concatenate([tabs_p[2].T, tabs_p[3].T], axis=0)
    (h_p, qr_p, kr_p, vr_p, gr_p, qt_p, km_p, vt_p, ckv_p, kpe_p) = _front_call(
        x_prompt.reshape(B * S, D), tabs_p, tab_t_p, front_t_w, FRONT_TM, FRONT_SUB, S // FRONT_TM)
    n_s = DB * T
    x_small = jnp.concatenate([x_sample.reshape(n_s, D), meta_tokens.astype(x_sample.dtype)], axis=0)
    pos_small = jnp.concatenate([jnp.tile(N_META + P + jnp.arange(T), DB), jnp.arange(N_META)])
    (h_s, qr_s, kr_s, vr_s, gr_s, qm_s, km_s, vm_s, ckv_s, kpe_s) = _front_call(
        x_small, _rope_tables(pos_small), None, front_w, n_s + N_META, n_s + N_META, 1)

    mix_p, p_state = _mixer_call(qt_p, km_p, vt_p, km_s, vm_s, qr_p, kr_p, vr_p, gr_p, kr_s, vr_s, B, S)
    mix_s, s_state = _smixer_call(qm_s, km_s, vm_s, cache_mla_ckv, cache_mla_kpe,
                                  jnp.concatenate([w_uk, w_uv], axis=1), qr_s, kr_s, vr_s, gr_s, state_ret, T)

    y_p = _tail_call(mix_p, h_p, tail_w, TAIL_TM, FRONT_SUB)
    y_s = _tail_call(mix_s, h_s, tail_w, n_s, n_s // 2)

    meta_ckv = jnp.broadcast_to(ckv_s[n_s:][None], (B, N_META, MLA_KV_LORA))
    meta_kpe = jnp.broadcast_to(kpe_s[n_s:][None], (B, N_META, MLA_ROPE))
    p_ckv = jnp.concatenate([meta_ckv, ckv_p.reshape(B, S, MLA_KV_LORA)], axis=1)
    p_kpe = jnp.concatenate([meta_kpe, kpe_p.reshape(B, S, MLA_ROPE)], axis=1)
    return (y_p.reshape(B, S, D), y_s.reshape(DB, T, D), p_ckv, p_kpe, p_state.astype(x_prompt.dtype),
            ckv_s[:n_s].reshape(DB, T, MLA_KV_LORA), kpe_s[:n_s].reshape(DB, T, MLA_ROPE),
            s_state.astype(state_ret.dtype))
```

```python
import functools
import math

import jax
import jax.numpy as jnp
from jax import lax
from jax.experimental import pallas as pl
from jax.experimental.pallas import tpu as pltpu

D_MODEL = 1024
DEPTH = 1
CHUNK = 64
N_META = 16
RET_HEADS = 4
RET_DK = 128
RET_DV = 128
MLA_HEADS = 4
MLA_NOPE = 128
MLA_ROPE = 64
MLA_V = 128
MLA_Q_LORA = 256
MLA_KV_LORA = 128
D_FF = 2816
ROPE_BASE = 10000.0
LN_EPS = 1e-5
RMS_EPS = 1e-6
DN_ALPHA = (2 * DEPTH) ** 0.25

OFF_KR = RET_HEADS * RET_DK
OFF_VR = 2 * RET_HEADS * RET_DK
OFF_GR = OFF_VR + RET_HEADS * RET_DV
OFF_CQ = OFF_GR + RET_HEADS * RET_DV
OFF_CKV = OFF_CQ + MLA_Q_LORA
OFF_KPE = OFF_CKV + MLA_KV_LORA
D_IN = OFF_KPE + MLA_ROPE

LANES = 128
MLA_QK_PAD = 2 * LANES
RET_W = RET_HEADS * RET_DK
MLA_QW = MLA_HEADS * MLA_QK_PAD
MLA_VW = MLA_HEADS * MLA_V
BF16_SUBLANES = 16
MLA_VT_ROWS = MLA_V + BF16_SUBLANES

FRONT_SUB = 256
FRONT_TM = 2 * FRONT_SUB
TAIL_TM = 4 * FRONT_SUB
ATT_QB = FRONT_TM
ATT_TQ = ATT_QB
ATT_TK = ATT_QB
RET_C = 256
VMEM_LIMIT = 56 * 1024 * 1024

MLA_SCALE = (MLA_NOPE + MLA_ROPE) ** -0.5
MLA_SCALE_LOG2E = MLA_SCALE * math.log2(math.e)
RET_LOG_GAMMA = tuple(math.log(1.0 - 2.0 ** (-5.0 - h)) for h in range(RET_HEADS))

f32 = jnp.float32
bf16 = jnp.bfloat16


def _dot(a, b):
    return jnp.dot(a, b, preferred_element_type=f32)


def _dot_nt(a, b):
    return lax.dot_general(a, b, (((1,), (1,)), ((), ())), preferred_element_type=f32)


def _dot_tn(a, b):
    return lax.dot_general(a, b, (((0,), (0,)), ((), ())), preferred_element_type=f32)


def _layer_norm(x, g, b):
    mu = jnp.mean(x, axis=-1, keepdims=True)
    xc = x - mu
    var = jnp.mean(xc * xc, axis=-1, keepdims=True)
    return xc * lax.rsqrt(var + LN_EPS) * g + b


def _rms_norm(x, g):
    return x * lax.rsqrt(jnp.mean(x * x, axis=-1, keepdims=True) + RMS_EPS) * g


def _silu(x):
    return x / (1.0 + jnp.exp(-x))


def _swiglu(x, w_in_ref, w_out_ref):
    xb = x.astype(bf16)
    hg = _dot(xb, w_in_ref[:, :D_FF])
    hu = _dot(xb, w_in_ref[:, D_FF:])
    return _dot((_silu(hg) * hu).astype(bf16), w_out_ref[...])


def _ffn_norm(x, y, g_ref, b_ref):
    return _layer_norm(DN_ALPHA * x + 0.5 * y, g_ref[...], b_ref[...])


def _rotate(cb, sb, ca, sa):
    return cb * ca - sb * sa, sb * ca + cb * sa


def _front_kernel(transposed, sub, x_ref, cr_ref, sr_ref, cm_ref, sm_ref, step_ref, *refs):
    if transposed:
        tab_t_ref, step_t_ref, refs = refs[0], refs[1], refs[2:]
    (w_in_ref, w_out_ref, ln_g_ref, ln_b_ref, w_mix_ref, gn_ref, qn_g_ref, w_uq_ref, kvn_g_ref,
     w_uk_ref, w_uv_ref,
     h_ref, qr_ref, kr_ref, vr_ref, gate_ref, qm_ref, km_ref, vm_ref, ckv_ref, kpe_ref) = refs
    n_sub = x_ref.shape[0] // sub
    tiles = [slice(j * sub, (j + 1) * sub) for j in range(n_sub)]

    xs = [x_ref[r, :] for r in tiles]
    ys = [_swiglu(x, w_in_ref, w_out_ref) for x in xs]
    ps = []
    for r, x, y in zip(tiles, xs, ys):
        h = _ffn_norm(x, y, ln_g_ref, ln_b_ref)
        h_ref[r, :] = h
        hb = h.astype(bf16)
        p_kpe = _dot(hb, w_mix_ref[:, OFF_KPE:])
        ps.append((_dot(hb, w_mix_ref[:, :OFF_KPE]),
                   jnp.concatenate([p_kpe, jnp.zeros_like(p_kpe)], axis=1)))

    for j, (r, (p, p_kpe)) in enumerate(zip(tiles, ps)):
        half_r, half_m = RET_DK // 2, MLA_ROPE // 2
        c, s = _rotate(cr_ref[r, :], sr_ref[r, :], step_ref[0, :, :half_r], step_ref[0, :, half_r:2 * half_r])
        cm, sm = _rotate(cm_ref[r, :], sm_ref[r, :], step_ref[0, :, 2 * half_r:2 * half_r + half_m],
                         step_ref[0, :, 2 * half_r + half_m:])
        z_q, z_h = jnp.zeros_like(cm), jnp.zeros_like(c)
        c_r = jnp.concatenate([c, c], axis=1)
        s_r = jnp.concatenate([-s, s], axis=1)
        c_m = jnp.concatenate([cm, cm, z_h], axis=1)
        s_ma = jnp.concatenate([-sm, z_q, z_h], axis=1)
        s_mb = jnp.concatenate([z_q, sm, z_h], axis=1)

        def rope_ret(xh):
            return xh * c_r + pltpu.roll(xh, 64, 1) * s_r

        def rope_mla(xh):
            return xh * c_m + pltpu.roll(xh, 96, 1) * s_ma + pltpu.roll(xh, 32, 1) * s_mb

        cq = _rms_norm(p[:, OFF_CQ:OFF_CKV], qn_g_ref[...]).astype(bf16)
        ckv = _rms_norm(p[:, OFF_CKV:OFF_KPE], kvn_g_ref[...])
        ckv_ref[r, :] = ckv
        ckv_b = ckv.astype(bf16)
        k_nope = _dot(ckv_b, w_uk_ref[...])
        if transposed:
            q_t = _dot_nt(w_uq_ref[...], cq) * MLA_SCALE_LOG2E
            v_t = _dot_nt(w_uv_ref[...], ckv_b)
        else:
            q = _dot(cq, w_uq_ref[...])
            v = _dot(ckv_b, w_uv_ref[...])

        for hd in range(RET_HEADS):
            lo = hd * RET_DK
            qr_ref[r, lo:lo + RET_DK] = rope_ret(p[:, lo:lo + RET_DK]).astype(bf16)
            kr_ref[r, lo:lo + RET_DK] = (rope_ret(p[:, OFF_KR + lo:OFF_KR + lo + RET_DK])
                                         * RET_DK ** -0.5).astype(bf16)
        vr_ref[r, :] = p[:, OFF_VR:OFF_GR].astype(bf16)
        gate_ref[r, :] = gn_ref[...] * _silu(p[:, OFF_GR:OFF_CQ])
        kpe = rope_mla(p_kpe)
        kpe_ref[r, :] = kpe[:, :MLA_ROPE]
        kpe_b = kpe.astype(bf16)
        for hd in range(MLA_HEADS):
            lo = hd * MLA_QK_PAD
            km_ref[r, lo:lo + LANES] = k_nope[:, hd * MLA_NOPE:(hd + 1) * MLA_NOPE].astype(bf16)
            km_ref[r, lo + LANES:lo + 2 * LANES] = kpe_b

        if transposed:
            half = MLA_ROPE // 2
            c_t, s_t = _rotate(tab_t_ref[:half, r], tab_t_ref[half:, r], step_t_ref[0, :half, :], step_t_ref[0, half:, :])
            for hd in range(MLA_HEADS):
                lo = hd * MLA_QK_PAD
                r0 = lo + MLA_NOPE
                x1 = q_t[r0:r0 + half, :]
                x2 = q_t[r0 + half:r0 + 2 * half, :]
                qm_ref[0, lo:r0, r] = q_t[lo:r0, :].astype(bf16)
                qm_ref[0, r0:r0 + half, r] = (x1 * c_t - x2 * s_t).astype(bf16)
                qm_ref[0, r0 + half:r0 + 2 * half, r] = (x1 * s_t + x2 * c_t).astype(bf16)
                qm_ref[0, r0 + 2 * half:lo + MLA_QK_PAD, r] = q_t[r0 + 2 * half:lo + MLA_QK_PAD, :].astype(bf16)
            ones = jnp.ones((MLA_VT_ROWS - MLA_V, v_t.shape[1]), bf16)
            for hd in range(MLA_HEADS):
                vm_ref[0, hd * MLA_VT_ROWS:hd * MLA_VT_ROWS + MLA_V, r] = v_t[hd * MLA_V:(hd + 1) * MLA_V, :].astype(bf16)
                vm_ref[0, hd * MLA_VT_ROWS + MLA_V:(hd + 1) * MLA_VT_ROWS, r] = ones
        else:
            for hd in range(MLA_HEADS):
                lo = hd * MLA_QK_PAD
                qm_ref[r, lo:lo + LANES] = q[:, lo:lo + LANES].astype(bf16)
                qm_ref[r, lo + LANES:lo + 2 * LANES] = rope_mla(q[:, lo + LANES:lo + 2 * LANES]).astype(bf16)
            vm_ref[r, :] = v.astype(bf16)


def _const_spec(shape):
    nd = len(shape)
    return pl.BlockSpec(shape, lambda *_: (0,) * nd, pipeline_mode=pl.Buffered(1))


def _front_call(x, tabs, steps, tabs_t, wts, tm, sub):
    n = x.shape[0]
    transposed = tabs_t is not None
    period = steps.shape[0]
    row = lambda w: pl.BlockSpec((tm, w), lambda i: (i, 0))
    col = lambda w: pl.BlockSpec((1, w, tm), lambda i: (i, 0, 0))
    in_specs = [row(D_MODEL)] + [pl.BlockSpec((tm, t.shape[1]), lambda i: (0, 0)) for t in tabs]
    in_specs.append(pl.BlockSpec((1,) + steps.shape[1:], lambda i: (i % period, 0, 0)))
    args = [x, *tabs, steps]
    if transposed:
        in_specs.append(pl.BlockSpec((MLA_ROPE, tm), lambda i: (0, 0)))
        in_specs.append(pl.BlockSpec((1,) + tabs_t[1].shape[1:], lambda i: (i % period, 0, 0)))
        args += list(tabs_t)
    in_specs += [_const_spec(w.shape) for w in wts]
    rows = lambda w, dt: (row(w), jax.ShapeDtypeStruct((n, w), dt))
    cols = lambda w, dt: (col(w), jax.ShapeDtypeStruct((n // tm, w, tm), dt))
    qv = cols if transposed else rows
    outs = [rows(D_MODEL, f32), rows(RET_W, bf16), rows(RET_W, bf16), rows(RET_W, bf16), rows(RET_W, f32),
            qv(MLA_QW, bf16), rows(MLA_QW, bf16),
            cols(MLA_HEADS * MLA_VT_ROWS, bf16) if transposed else rows(MLA_VW, bf16),
            rows(MLA_KV_LORA, f32), rows(MLA_ROPE, f32)]
    return pl.pallas_call(
        functools.partial(_front_kernel, transposed, sub),
        grid=(n // tm,),
        in_specs=in_specs,
        out_specs=[o[0] for o in outs],
        out_shape=[o[1] for o in outs],
        compiler_params=pltpu.CompilerParams(dimension_semantics=("arbitrary",),
                                             vmem_limit_bytes=VMEM_LIMIT),
        name="front_t" if transposed else "front",
    )(*args, *wts)


def _decay_mask(c, lg):
    r = lax.broadcasted_iota(jnp.int32, (c, c), 0)
    k = lax.broadcasted_iota(jnp.int32, (c, c), 1)
    rel = (r - k).astype(f32)
    return jnp.where(rel >= 0.0, jnp.exp(lg * jnp.maximum(rel, 0.0)), 0.0)


def _row_pow(c, lg, offset, sign):
    i = lax.broadcasted_iota(jnp.int32, (c, LANES), 0).astype(f32)
    return jnp.exp(lg * (offset + sign * i))


def _retention_tables(c, hd):
    lg = RET_LOG_GAMMA[hd]
    return _decay_mask(c, lg), _row_pow(c, lg, 1.0, 1.0), _row_pow(c, lg, c - 1.0, -1.0)


def _retention_chunk(qr_ref, kr_ref, vr_ref, gate_ref, mix_ref, rows, states, tables):
    c = rows.stop - rows.start
    heads = range(RET_HEADS)
    cols = [slice(hd * RET_DK, (hd + 1) * RET_DK) for hd in heads]
    q = [qr_ref[rows, cols[hd]] for hd in heads]
    k = [kr_ref[rows, cols[hd]] for hd in heads]
    v = [vr_ref[rows, cols[hd]] for hd in heads]
    scores = [_dot_nt(q[hd], k[hd]) for hd in heads]
    cross = [_dot(q[hd], states[hd].astype(bf16)) for hd in heads]
    kv = [_dot_tn((k[hd].astype(f32) * tables[hd][2][...]).astype(bf16), v[hd]) for hd in heads]
    new_states = [math.exp(RET_LOG_GAMMA[hd] * c) * states[hd] + kv[hd] for hd in heads]

    def finish():
        inner = [_dot((scores[hd] * tables[hd][0][...]).astype(bf16), v[hd]) for hd in heads]
        for hd in heads:
            o = inner[hd] + cross[hd] * tables[hd][1][...]
            mu = jnp.mean(o, axis=-1, keepdims=True)
            oc = o - mu
            var = jnp.mean(oc * oc, axis=-1, keepdims=True)
            y = oc * lax.rsqrt(var + LN_EPS) * gate_ref[rows, cols[hd]]
            mix_ref[rows, cols[hd]] = y.astype(bf16)

    return new_states, finish


def _att_update(m_ref, acc_ref, u, s, v_t, ok):
    if ok is not None:
        s = jnp.where(ok, s, -jnp.inf)
    m = m_ref[u]
    m_new = jnp.maximum(m, jnp.max(s, axis=0, keepdims=True))
    p = jnp.exp2(s - m_new)
    m_ref[u] = m_new
    acc_ref[u] = jnp.exp2(m - m_new) * acc_ref[u] + _dot(v_t, p.astype(bf16))


def _mixer_kernel(qt_ref, km_ref, vt_ref, kmeta_ref, vmeta_ref, qr_ref, kr_ref, vr_ref, gate_ref,
                  krmeta_ref, vrmeta_ref, mix_ref, state_ref,
                  m_ref, acc_ref, s_ref, dmask_ref, qdec_ref, kdec_ref):
    qi = pl.program_id(1)

    @pl.when((pl.program_id(0) == 0) & (qi == 0))
    def _():
        for hd in range(RET_HEADS):
            dmask_ref[hd], qdec_ref[hd], kdec_ref[hd] = _retention_tables(RET_C, hd)

    @pl.when(qi == 0)
    def _():
        for hd in range(RET_HEADS):
            lo = hd * RET_DK
            lg = RET_LOG_GAMMA[hd]
            kd = (krmeta_ref[:, lo:lo + RET_DK].astype(f32)
                  * _row_pow(N_META, lg, N_META - 1.0, -1.0)).astype(bf16)
            state_ref[0, hd] = _dot_tn(kd, vrmeta_ref[:, lo:lo + RET_DV])

    tables = [(dmask_ref.at[hd], qdec_ref.at[hd], kdec_ref.at[hd]) for hd in range(RET_HEADS)]
    states = [state_ref[0, hd] for hd in range(RET_HEADS)]
    finish_prev = None
    for c in range(ATT_TQ // RET_C):
        states, finish_chunk = _retention_chunk(qr_ref, kr_ref, vr_ref, gate_ref, mix_ref,
                                                slice(c * RET_C, (c + 1) * RET_C), states, tables)
        if finish_prev is not None:
            finish_prev()
        finish_prev = finish_chunk
    finish_prev()
    for hd in range(RET_HEADS):
        state_ref[0, hd] = states[hd]

    n_qb = ATT_TQ // ATT_QB
    units = [(hd, qb) for hd in range(MLA_HEADS) for qb in range(n_qb)]

    def q_block(hd, qb):
        return qt_ref[qb, hd * MLA_QK_PAD:(hd + 1) * MLA_QK_PAD, :]

    def v_tile(hd, blk0, n_blk):
        rows = slice(hd * MLA_VT_ROWS, (hd + 1) * MLA_VT_ROWS)
        return jnp.concatenate([vt_ref[blk0 + j, rows, :] for j in range(n_blk)], axis=1)

    def scores(start, n_keys, u):
        hd, qb = units[u]
        k = km_ref[0, pl.ds(start, n_keys), hd * MLA_QK_PAD:(hd + 1) * MLA_QK_PAD]
        return _dot(k, q_block(hd, qb))

    s_ref[...] = scores(0, ATT_TK, 0)

    s_meta = [_dot(kmeta_ref[:, hd * MLA_QK_PAD:(hd + 1) * MLA_QK_PAD], q_block(hd, qb)) for hd, qb in units]
    p_meta = []
    for u, s0 in enumerate(s_meta):
        m = jnp.max(s0, axis=0, keepdims=True)
        m_ref[u] = m
        p_meta.append(jnp.exp2(s0 - m).astype(bf16))
    ones = jnp.ones((N_META, MLA_VT_ROWS - MLA_V), bf16)
    for u, (hd, qb) in enumerate(units):
        v_ext = jnp.concatenate([vmeta_ref[:, hd * MLA_V:(hd + 1) * MLA_V], ones], axis=1)
        acc_ref[u] = _dot_tn(v_ext, p_meta[u])

    def full_step(kt, _):
        start = pl.multiple_of(kt * ATT_TK, ATT_TK)
        s_cur = s_ref[...]
        for u, (hd, qb) in enumerate(units):
            if u + 1 < len(units):
                s_next = scores(start, ATT_TK, u + 1)
            else:
                s_next = scores(pl.multiple_of(start + ATT_TK, ATT_TK), ATT_TK, 0)
            _att_update(m_ref, acc_ref, u, s_cur, v_tile(hd, kt * (ATT_TK // ATT_QB), ATT_TK // ATT_QB), None)
            s_cur = s_next
        s_ref[...] = s_cur
        return 0

    lax.fori_loop(0, qi, full_step, 0)

    start = pl.multiple_of(qi * ATT_TQ, ATT_TQ)
    s_cur = s_ref[...]
    for u, (hd, qb) in enumerate(units):
        if u + 1 < len(units):
            s_next = scores(start, (units[u + 1][1] + 1) * ATT_QB, u + 1)
        k_chunk = lax.broadcasted_iota(jnp.int32, s_cur.shape, 0) // CHUNK
        q_chunk = lax.broadcasted_iota(jnp.int32, s_cur.shape, 1) // CHUNK + qb * (ATT_QB // CHUNK)
        _att_update(m_ref, acc_ref, u, s_cur, v_tile(hd, qi * n_qb, s_cur.shape[0] // ATT_QB),
                    k_chunk <= q_chunk)
        out = (acc_ref[u, :MLA_V, :] / acc_ref[u, MLA_V:MLA_V + 1, :]).T
        mix_ref[qb * ATT_QB:(qb + 1) * ATT_QB, RET_W + hd * MLA_V:RET_W + (hd + 1) * MLA_V] = out.astype(bf16)
        s_cur = s_next


def _mixer_call(qt, km, vt, kmeta, vmeta, qr, kr, vr, gate, krmeta, vrmeta, batch, seq):
    nq = seq // ATT_TQ
    n_qb = ATT_TQ // ATT_QB
    tile = lambda w: pl.BlockSpec((ATT_TQ, w), lambda b, i: (b * nq + i, 0))
    const = lambda a: pl.BlockSpec((N_META, a.shape[1]), lambda b, i: (a.shape[0] // N_META - 1, 0))
    return pl.pallas_call(
        _mixer_kernel,
        grid=(batch, nq),
        in_specs=[pl.BlockSpec((n_qb, MLA_QW, ATT_QB), lambda b, i: (b * nq + i, 0, 0)),
                  pl.BlockSpec((1, seq, MLA_QW), lambda b, i: (b, 0, 0)),
                  pl.BlockSpec((seq // ATT_QB, MLA_HEADS * MLA_VT_ROWS, ATT_QB), lambda b, i: (b, 0, 0)),
                  const(kmeta), const(vmeta),
                  tile(RET_W), tile(RET_W), tile(RET_W), tile(RET_W),
                  const(krmeta), const(vrmeta)],
        out_specs=[tile(D_MODEL),
                   pl.BlockSpec((1, RET_HEADS, RET_DK, RET_DV), lambda b, i: (b, 0, 0, 0))],
        out_shape=[jax.ShapeDtypeStruct((batch * seq, D_MODEL), bf16),
                   jax.ShapeDtypeStruct((batch, RET_HEADS, RET_DK, RET_DV), f32)],
        scratch_shapes=[pltpu.VMEM((MLA_HEADS * n_qb, 1, ATT_QB), f32),
                        pltpu.VMEM((MLA_HEADS * n_qb, MLA_VT_ROWS, ATT_QB), f32),
                        pltpu.VMEM((ATT_TK, ATT_QB), f32),
                        pltpu.VMEM((RET_HEADS, RET_C, RET_C), f32),
                        pltpu.VMEM((RET_HEADS, RET_C, LANES), f32),
                        pltpu.VMEM((RET_HEADS, RET_C, LANES), f32)],
        compiler_params=pltpu.CompilerParams(dimension_semantics=("arbitrary", "arbitrary"),
                                             vmem_limit_bytes=VMEM_LIMIT),
        name="mixer",
    )(qt, km.reshape(batch, seq, MLA_QW), vt, kmeta, vmeta, qr, kr, vr, gate, krmeta, vrmeta)


def _smixer_kernel(qm_ref, knew_ref, vnew_ref, kmeta_ref, vmeta_ref, cckv_ref, ckpe_ref, w_ukv_ref,
                   qr_ref, kr_ref, vr_ref, gate_ref, s0_ref, mix_ref, state_ref):
    t = qr_ref.shape[0]
    states, finish_retention = _retention_chunk(qr_ref, kr_ref, vr_ref, gate_ref, mix_ref, slice(0, t),
                                                [s0_ref[0, hd] for hd in range(RET_HEADS)],
                                                [_retention_tables(t, hd) for hd in range(RET_HEADS)])
    for hd in range(RET_HEADS):
        state_ref[0, hd] = states[hd]
    finish_retention()

    heads = range(MLA_HEADS)
    kcols = [slice(hd * MLA_QK_PAD, (hd + 1) * MLA_QK_PAD) for hd in heads]
    vcols = [slice(hd * MLA_V, (hd + 1) * MLA_V) for hd in heads]
    kv_c = _dot(cckv_ref[0].astype(bf16), w_ukv_ref[...])
    kpe_c = ckpe_ref[0].astype(bf16)
    q = [qm_ref[:, kcols[hd]] for hd in heads]
    s_m = [_dot_nt(q[hd], kmeta_ref[:, kcols[hd]]) for hd in heads]
    s_n = [_dot_nt(q[hd], knew_ref[:, kcols[hd]]) for hd in heads]
    s_r = [_dot_nt(q[hd][:, LANES:LANES + MLA_ROPE], kpe_c) for hd in heads]
    s_c = [_dot_nt(q[hd][:, :LANES], kv_c[:, hd * MLA_NOPE:(hd + 1) * MLA_NOPE].astype(bf16)) + s_r[hd]
           for hd in heads]
    probs, denom = [], []
    for hd in heads:
        parts = [s_m[hd] * MLA_SCALE, s_c[hd] * MLA_SCALE, s_n[hd] * MLA_SCALE]
        m = functools.reduce(jnp.maximum, [jnp.max(s, axis=-1, keepdims=True) for s in parts])
        ps = [jnp.exp(s - m) for s in parts]
        denom.append(functools.reduce(jnp.add, [jnp.sum(p, axis=-1, keepdims=True) for p in ps]))
        probs.append([p.astype(bf16) for p in ps])
    for hd in heads:
        v_c = kv_c[:, MLA_HEADS * MLA_NOPE + hd * MLA_V:MLA_HEADS * MLA_NOPE + (hd + 1) * MLA_V].astype(bf16)
        acc = (_dot(probs[hd][0], vmeta_ref[:, vcols[hd]]) + _dot(probs[hd][1], v_c)
               + _dot(probs[hd][2], vnew_ref[:, vcols[hd]]))
        mix_ref[:, RET_W + hd * MLA_V:RET_W + (hd + 1) * MLA_V] = (acc / denom[hd]).astype(bf16)


def _smixer_call(qm, km, vm, cckv, ckpe, w_ukv, qr, kr, vr, gate, s0, t):
    db, past = cckv.shape[0], cckv.shape[1]
    tile = lambda w: pl.BlockSpec((t, w), lambda b: (b, 0))
    meta = lambda w: pl.BlockSpec((N_META, w), lambda b: (db * t // N_META, 0))
    const = lambda a: pl.BlockSpec(a.shape, lambda b: (0,) * a.ndim)
    per = lambda *tail: pl.BlockSpec((1,) + tail, lambda b: (b,) + (0,) * len(tail))
    return pl.pallas_call(
        _smixer_kernel,
        grid=(db,),
        in_specs=[tile(MLA_QW), tile(MLA_QW), tile(MLA_VW), meta(MLA_QW), meta(MLA_VW),
                  per(past, MLA_KV_LORA), per(past, MLA_ROPE), const(w_ukv),
                  tile(RET_W), tile(RET_W), tile(RET_W), tile(RET_W),
                  per(RET_HEADS, RET_DK, RET_DV)],
        out_specs=[tile(D_MODEL), per(RET_HEADS, RET_DK, RET_DV)],
        out_shape=[jax.ShapeDtypeStruct((db * t, D_MODEL), bf16),
                   jax.ShapeDtypeStruct((db, RET_HEADS, RET_DK, RET_DV), f32)],
        compiler_params=pltpu.CompilerParams(dimension_semantics=("arbitrary",),
                                             vmem_limit_bytes=VMEM_LIMIT),
        name="smixer",
    )(qm, km, vm, km, vm, cckv, ckpe, w_ukv, qr, kr, vr, gate, s0)


def _tail_kernel(sub, mix_ref, h_ref, w_mo_ref, ln2_g_ref, ln2_b_ref, w_in_ref, w_out_ref, ln3_g_ref, ln3_b_ref,
                 y_ref):
    tiles = [slice(j * sub, (j + 1) * sub) for j in range(mix_ref.shape[0] // sub)]
    mixes = [_dot(mix_ref[r, :], w_mo_ref[...]) for r in tiles]
    h2s, ys = [], []
    for r, mix in zip(tiles, mixes):
        h2 = _layer_norm(DN_ALPHA * h_ref[r, :] + mix, ln2_g_ref[...], ln2_b_ref[...])
        h2s.append(h2)
        ys.append(_swiglu(h2, w_in_ref, w_out_ref))
    for r, h2, y in zip(tiles, h2s, ys):
        y_ref[r, :] = _ffn_norm(h2, y, ln3_g_ref, ln3_b_ref)


def _tail_call(mix, h, wts, tm, sub):
    n = mix.shape[0]
    row = lambda w: pl.BlockSpec((tm, w), lambda i: (i, 0))
    return pl.pallas_call(
        functools.partial(_tail_kernel, sub),
        grid=(n // tm,),
        in_specs=[row(D_MODEL), row(D_MODEL)] + [_const_spec(w.shape) for w in wts],
        out_specs=row(D_MODEL),
        out_shape=jax.ShapeDtypeStruct((n, D_MODEL), f32),
        compiler_params=pltpu.CompilerParams(dimension_semantics=("arbitrary",),
                                             vmem_limit_bytes=VMEM_LIMIT),
        name="tail",
    )(mix, h, *wts)


def _rope_tables(pos):
    posf = pos.astype(f32)[:, None]
    inv_r = ROPE_BASE ** (-jnp.arange(0, RET_DK, 2, dtype=f32) / RET_DK)
    ang = posf * inv_r[None, :]
    inv_m = ROPE_BASE ** (-jnp.arange(0, MLA_ROPE, 2, dtype=f32) / MLA_ROPE)
    angm = posf * inv_m[None, :]
    return jnp.cos(ang), jnp.sin(ang), jnp.cos(angm), jnp.sin(angm)


def kernel(x_prompt, x_sample, cache_mla_ckv, cache_mla_kpe, state_ret, meta_tokens,
           ffn1_w_in, ffn1_w_out, ln1_g, ln1_b, w_mix_in, ret_gn_g, mla_q_norm_g, mla_w_uq,
           mla_kv_norm_g, mla_w_ukv, w_mix_out, ln2_g, ln2_b, ffn2_w_in, ffn2_w_out, ln3_g, ln3_b):
    B, S, D = x_prompt.shape
    DB, T, _ = x_sample.shape
    P = cache_mla_ckv.shape[1]
    assert S % ATT_TQ == 0 and S % FRONT_TM == 0 and (B * S) % TAIL_TM == 0 and ATT_TQ % RET_C == 0
    row = lambda a: a.reshape(1, -1).astype(f32)

    w_mix = w_mix_in.astype(bf16)
    w_uq = mla_w_uq.reshape(MLA_Q_LORA, MLA_HEADS, MLA_NOPE + MLA_ROPE)
    w_uq = jnp.pad(w_uq, ((0, 0), (0, 0), (0, MLA_QK_PAD - MLA_NOPE - MLA_ROPE)))
    w_uq = w_uq.reshape(MLA_Q_LORA, MLA_QW).astype(bf16)
    w_ukv = mla_w_ukv.reshape(MLA_KV_LORA, MLA_HEADS, MLA_NOPE + MLA_V)
    w_uk = w_ukv[:, :, :MLA_NOPE].reshape(MLA_KV_LORA, -1).astype(bf16)
    w_uv = w_ukv[:, :, MLA_NOPE:].reshape(MLA_KV_LORA, -1).astype(bf16)
    shared_w = [ffn1_w_in.astype(bf16), ffn1_w_out.astype(bf16), row(ln1_g), row(ln1_b), w_mix,
                row(ret_gn_g), row(mla_q_norm_g)]
    front_w = shared_w + [w_uq, row(mla_kv_norm_g), w_uk, w_uv]
    front_t_w = shared_w + [w_uq.T, row(mla_kv_norm_g), w_uk, w_uv.T]
    tail_w = [w_mix_out.astype(bf16), row(ln2_g), row(ln2_b), ffn2_w_in.astype(bf16),
              ffn2_w_out.astype(bf16), row(ln3_g), row(ln3_b)]

    tabs_p = _rope_tables(N_META + jnp.arange(FRONT_TM))
    steps_p = _rope_tables(FRONT_TM * jnp.arange(S // FRONT_TM))
    tabs_t_p = (jnp.concatenate([tabs_p[2].T, tabs_p[3].T], axis=0),
                jnp.concatenate([steps_p[2], steps_p[3]], axis=1)[:, :, None])
    (h_p, qr_p, kr_p, vr_p, gr_p, qt_p, km_p, vt_p, ckv_p, kpe_p) = _front_call(
        x_prompt.reshape(B * S, D), tabs_p, jnp.concatenate(steps_p, axis=1)[:, None, :], tabs_t_p, front_t_w,
        FRONT_TM, FRONT_SUB)
    n_s = DB * T
    x_small = jnp.concatenate([x_sample.reshape(n_s, D), meta_tokens.astype(x_sample.dtype)], axis=0)
    pos_small = jnp.concatenate([jnp.tile(N_META + P + jnp.arange(T), DB), jnp.arange(N_META)])
    (h_s, qr_s, kr_s, vr_s, gr_s, qm_s, km_s, vm_s, ckv_s, kpe_s) = _front_call(
        x_small, _rope_tables(pos_small), jnp.concatenate(_rope_tables(jnp.zeros((1,), jnp.int32)), axis=1)[:, None, :],
        None, front_w, n_s + N_META, n_s + N_META)

    mix_p, p_state = _mixer_call(qt_p, km_p, vt_p, km_s, vm_s, qr_p, kr_p, vr_p, gr_p, kr_s, vr_s, B, S)
    mix_s, s_state = _smixer_call(qm_s, km_s, vm_s, cache_mla_ckv, cache_mla_kpe,
                                  jnp.concatenate([w_uk, w_uv], axis=1), qr_s, kr_s, vr_s, gr_s, state_ret, T)

    y_p = _tail_call(mix_p, h_p, tail_w, TAIL_TM, FRONT_SUB)
    y_s = _tail_call(mix_s, h_s, tail_w, n_s, n_s // 2)

    meta_ckv = jnp.broadcast_to(ckv_s[n_s:][None], (B, N_META, MLA_KV_LORA))
    meta_kpe = jnp.broadcast_to(kpe_s[n_s:][None], (B, N_META, MLA_ROPE))
    p_ckv = jnp.concatenate([meta_ckv, ckv_p.reshape(B, S, MLA_KV_LORA)], axis=1)
    p_kpe = jnp.concatenate([meta_kpe, kpe_p.reshape(B, S, MLA_ROPE)], axis=1)
    return (y_p.reshape(B, S, D), y_s.reshape(DB, T, D), p_ckv, p_kpe, p_state.astype(x_prompt.dtype),
            ckv_s[:n_s].reshape(DB, T, MLA_KV_LORA), kpe_s[:n_s].reshape(DB, T, MLA_ROPE),
            s_state.astype(state_ret.dtype))
```

```python
import functools
import math

import jax
import jax.numpy as jnp
from jax import lax
from jax.experimental import pallas as pl
from jax.experimental.pallas import tpu as pltpu

D_MODEL = 1024
DEPTH = 1
CHUNK = 64
N_META = 16
RET_HEADS = 4
RET_DK = 128
RET_DV = 128
MLA_HEADS = 4
MLA_NOPE = 128
MLA_ROPE = 64
MLA_V = 128
MLA_Q_LORA = 256
MLA_KV_LORA = 128
D_FF = 2816
ROPE_BASE = 10000.0
LN_EPS = 1e-5
RMS_EPS = 1e-6
DN_ALPHA = (2 * DEPTH) ** 0.25

OFF_KR = RET_HEADS * RET_DK
OFF_VR = 2 * RET_HEADS * RET_DK
OFF_GR = OFF_VR + RET_HEADS * RET_DV
OFF_CQ = OFF_GR + RET_HEADS * RET_DV
OFF_CKV = OFF_CQ + MLA_Q_LORA
OFF_KPE = OFF_CKV + MLA_KV_LORA
D_IN = OFF_KPE + MLA_ROPE

LANES = 128
MLA_QK_PAD = 2 * LANES
RET_W = RET_HEADS * RET_DK
MLA_QW = MLA_HEADS * MLA_QK_PAD
MLA_VW = MLA_HEADS * MLA_V
BF16_SUBLANES = 16
MLA_VT_ROWS = MLA_V + BF16_SUBLANES

FRONT_SUB = 256
FRONT_TM = 2 * FRONT_SUB
TAIL_TM = 4 * FRONT_SUB
ATT_TQ = FRONT_TM
ATT_TK = FRONT_TM
ATT_KB = 512
ATT_AHEAD = 4
RET_C = 256
VMEM_LIMIT = 56 * 1024 * 1024

MLA_SCALE = (MLA_NOPE + MLA_ROPE) ** -0.5
MLA_SCALE_LOG2E = MLA_SCALE * math.log2(math.e)
RET_LOG_GAMMA = tuple(math.log(1.0 - 2.0 ** (-5.0 - h)) for h in range(RET_HEADS))

f32 = jnp.float32
bf16 = jnp.bfloat16


def _dot(a, b):
    return jnp.dot(a, b, preferred_element_type=f32)


def _dot_nt(a, b):
    return lax.dot_general(a, b, (((1,), (1,)), ((), ())), preferred_element_type=f32)


def _dot_tn(a, b):
    return lax.dot_general(a, b, (((0,), (0,)), ((), ())), preferred_element_type=f32)


def _layer_norm(x, g, b):
    mu = jnp.mean(x, axis=-1, keepdims=True)
    xc = x - mu
    var = jnp.mean(xc * xc, axis=-1, keepdims=True)
    return xc * lax.rsqrt(var + LN_EPS) * g + b


def _rms_norm(x, g):
    return x * lax.rsqrt(jnp.mean(x * x, axis=-1, keepdims=True) + RMS_EPS) * g


def _silu(x):
    return x / (1.0 + jnp.exp(-x))


def _swiglu(x, w_in_ref, w_out_ref):
    xb = x.astype(bf16)
    hg = _dot(xb, w_in_ref[:, :D_FF])
    hu = _dot(xb, w_in_ref[:, D_FF:])
    return _dot((_silu(hg) * hu).astype(bf16), w_out_ref[...])


def _ffn_norm(x, y, g_ref, b_ref):
    return _layer_norm(DN_ALPHA * x + 0.5 * y, g_ref[...], b_ref[...])


def _rotate(cb, sb, ca, sa):
    return cb * ca - sb * sa, sb * ca + cb * sa


def _front_kernel(transposed, sub, x_ref, cr_ref, sr_ref, cm_ref, sm_ref, step_ref, *refs):
    if transposed:
        tab_t_ref, step_t_ref, refs = refs[0], refs[1], refs[2:]
    (w_in_ref, w_out_ref, ln_g_ref, ln_b_ref, w_mix_ref, gn_ref, qn_g_ref, w_uq_ref, kvn_g_ref,
     w_uk_ref, w_uv_ref,
     h_ref, qr_ref, kr_ref, vr_ref, gate_ref, qm_ref, km_ref, vm_ref, ckv_ref, kpe_ref) = refs
    n_sub = x_ref.shape[0] // sub
    tiles = [slice(j * sub, (j + 1) * sub) for j in range(n_sub)]

    xs = [x_ref[r, :] for r in tiles]
    ys = [_swiglu(x, w_in_ref, w_out_ref) for x in xs]
    ps = []
    for r, x, y in zip(tiles, xs, ys):
        h = _ffn_norm(x, y, ln_g_ref, ln_b_ref)
        h_ref[r, :] = h
        hb = h.astype(bf16)
        p_kpe = _dot(hb, w_mix_ref[:, OFF_KPE:])
        ps.append((_dot(hb, w_mix_ref[:, :OFF_KPE]),
                   jnp.concatenate([p_kpe, jnp.zeros_like(p_kpe)], axis=1)))

    for j, (r, (p, p_kpe)) in enumerate(zip(tiles, ps)):
        half_r, half_m = RET_DK // 2, MLA_ROPE // 2
        c, s = _rotate(cr_ref[r, :], sr_ref[r, :], step_ref[0, :, :half_r], step_ref[0, :, half_r:2 * half_r])
        cm, sm = _rotate(cm_ref[r, :], sm_ref[r, :], step_ref[0, :, 2 * half_r:2 * half_r + half_m],
                         step_ref[0, :, 2 * half_r + half_m:])
        z_q, z_h = jnp.zeros_like(cm), jnp.zeros_like(c)
        c_r = jnp.concatenate([c, c], axis=1)
        s_r = jnp.concatenate([-s, s], axis=1)
        c_m = jnp.concatenate([cm, cm, z_h], axis=1)
        s_ma = jnp.concatenate([-sm, z_q, z_h], axis=1)
        s_mb = jnp.concatenate([z_q, sm, z_h], axis=1)

        def rope_ret(xh):
            return xh * c_r + pltpu.roll(xh, 64, 1) * s_r

        def rope_mla(xh):
            return xh * c_m + pltpu.roll(xh, 96, 1) * s_ma + pltpu.roll(xh, 32, 1) * s_mb

        cq = _rms_norm(p[:, OFF_CQ:OFF_CKV], qn_g_ref[...]).astype(bf16)
        ckv = _rms_norm(p[:, OFF_CKV:OFF_KPE], kvn_g_ref[...])
        ckv_ref[r, :] = ckv
        ckv_b = ckv.astype(bf16)
        k_nope = _dot(ckv_b, w_uk_ref[...])
        if transposed:
            q_t = _dot_nt(w_uq_ref[...], cq) * MLA_SCALE_LOG2E
            v_t = _dot_nt(w_uv_ref[...], ckv_b)
        else:
            q = _dot(cq, w_uq_ref[...])
            v = _dot(ckv_b, w_uv_ref[...])

        for hd in range(RET_HEADS):
            lo = hd * RET_DK
            qr_ref[r, lo:lo + RET_DK] = rope_ret(p[:, lo:lo + RET_DK]).astype(bf16)
            kr_ref[r, lo:lo + RET_DK] = (rope_ret(p[:, OFF_KR + lo:OFF_KR + lo + RET_DK])
                                         * RET_DK ** -0.5).astype(bf16)
        vr_ref[r, :] = p[:, OFF_VR:OFF_GR].astype(bf16)
        gate_ref[r, :] = gn_ref[...] * _silu(p[:, OFF_GR:OFF_CQ])
        kpe = rope_mla(p_kpe)
        kpe_ref[r, :] = kpe[:, :MLA_ROPE]
        kpe_b = kpe.astype(bf16)
        for hd in range(MLA_HEADS):
            lo = hd * MLA_QK_PAD
            km_ref[r, lo:lo + LANES] = k_nope[:, hd * MLA_NOPE:(hd + 1) * MLA_NOPE].astype(bf16)
            km_ref[r, lo + LANES:lo + 2 * LANES] = kpe_b

        if transposed:
            half = MLA_ROPE // 2
            c_t, s_t = _rotate(tab_t_ref[:half, r], tab_t_ref[half:, r], step_t_ref[0, :half, :], step_t_ref[0, half:, :])
            for hd in range(MLA_HEADS):
                lo = hd * MLA_QK_PAD
                r0 = lo + MLA_NOPE
                x1 = q_t[r0:r0 + half, :]
                x2 = q_t[r0 + half:r0 + 2 * half, :]
                qm_ref[0, lo:r0, r] = q_t[lo:r0, :].astype(bf16)
                qm_ref[0, r0:r0 + half, r] = (x1 * c_t - x2 * s_t).astype(bf16)
                qm_ref[0, r0 + half:r0 + 2 * half, r] = (x1 * s_t + x2 * c_t).astype(bf16)
                qm_ref[0, r0 + 2 * half:lo + MLA_QK_PAD, r] = q_t[r0 + 2 * half:lo + MLA_QK_PAD, :].astype(bf16)
            ones = jnp.ones((MLA_VT_ROWS - MLA_V, v_t.shape[1]), bf16)
            for hd in range(MLA_HEADS):
                vm_ref[0, hd * MLA_VT_ROWS:hd * MLA_VT_ROWS + MLA_V, r] = v_t[hd * MLA_V:(hd + 1) * MLA_V, :].astype(bf16)
                vm_ref[0, hd * MLA_VT_ROWS + MLA_V:(hd + 1) * MLA_VT_ROWS, r] = ones
        else:
            for hd in range(MLA_HEADS):
                lo = hd * MLA_QK_PAD
                qm_ref[r, lo:lo + LANES] = q[:, lo:lo + LANES].astype(bf16)
                qm_ref[r, lo + LANES:lo + 2 * LANES] = rope_mla(q[:, lo + LANES:lo + 2 * LANES]).astype(bf16)
            vm_ref[r, :] = v.astype(bf16)


def _const_spec(shape):
    nd = len(shape)
    return pl.BlockSpec(shape, lambda *_: (0,) * nd, pipeline_mode=pl.Buffered(1))


def _front_call(x, tabs, steps, tabs_t, wts, tm, sub):
    n = x.shape[0]
    transposed = tabs_t is not None
    period = steps.shape[0]
    row = lambda w: pl.BlockSpec((tm, w), lambda i: (i, 0))
    col = lambda w: pl.BlockSpec((1, w, tm), lambda i: (i, 0, 0))
    in_specs = [row(D_MODEL)] + [pl.BlockSpec((tm, t.shape[1]), lambda i: (0, 0)) for t in tabs]
    in_specs.append(pl.BlockSpec((1,) + steps.shape[1:], lambda i: (i % period, 0, 0)))
    args = [x, *tabs, steps]
    if transposed:
        in_specs.append(pl.BlockSpec((MLA_ROPE, tm), lambda i: (0, 0)))
        in_specs.append(pl.BlockSpec((1,) + tabs_t[1].shape[1:], lambda i: (i % period, 0, 0)))
        args += list(tabs_t)
    in_specs += [_const_spec(w.shape) for w in wts]
    rows = lambda w, dt: (row(w), jax.ShapeDtypeStruct((n, w), dt))
    cols = lambda w, dt: (col(w), jax.ShapeDtypeStruct((n // tm, w, tm), dt))
    qv = cols if transposed else rows
    outs = [rows(D_MODEL, f32), rows(RET_W, bf16), rows(RET_W, bf16), rows(RET_W, bf16), rows(RET_W, f32),
            qv(MLA_QW, bf16), rows(MLA_QW, bf16),
            cols(MLA_HEADS * MLA_VT_ROWS, bf16) if transposed else rows(MLA_VW, bf16),
            rows(MLA_KV_LORA, f32), rows(MLA_ROPE, f32)]
    return pl.pallas_call(
        functools.partial(_front_kernel, transposed, sub),
        grid=(n // tm,),
        in_specs=in_specs,
        out_specs=[o[0] for o in outs],
        out_shape=[o[1] for o in outs],
        compiler_params=pltpu.CompilerParams(dimension_semantics=("arbitrary",),
                                             vmem_limit_bytes=VMEM_LIMIT),
        name="front_t" if transposed else "front",
    )(*args, *wts)


def _decay_mask(c, lg):
    r = lax.broadcasted_iota(jnp.int32, (c, c), 0)
    k = lax.broadcasted_iota(jnp.int32, (c, c), 1)
    rel = (r - k).astype(f32)
    return jnp.where(rel >= 0.0, jnp.exp(lg * jnp.maximum(rel, 0.0)), 0.0)


def _row_pow(c, lg, offset, sign):
    i = lax.broadcasted_iota(jnp.int32, (c, LANES), 0).astype(f32)
    return jnp.exp(lg * (offset + sign * i))


def _retention_tables(c, hd):
    lg = RET_LOG_GAMMA[hd]
    return _decay_mask(c, lg), _row_pow(c, lg, 1.0, 1.0), _row_pow(c, lg, c - 1.0, -1.0)


def _retention_chunk(qr_ref, kr_ref, vr_ref, gate_ref, mix_ref, rows, states, tables):
    c = rows.stop - rows.start
    heads = range(RET_HEADS)
    cols = [slice(hd * RET_DK, (hd + 1) * RET_DK) for hd in heads]
    q = [qr_ref[rows, cols[hd]] for hd in heads]
    k = [kr_ref[rows, cols[hd]] for hd in heads]
    v = [vr_ref[rows, cols[hd]] for hd in heads]
    scores = [_dot_nt(q[hd], k[hd]) for hd in heads]
    cross = [_dot(q[hd], states[hd].astype(bf16)) for hd in heads]
    kv = [_dot_tn((k[hd].astype(f32) * tables[hd][2][...]).astype(bf16), v[hd]) for hd in heads]
    new_states = [math.exp(RET_LOG_GAMMA[hd] * c) * states[hd] + kv[hd] for hd in heads]

    def finish():
        inner = [_dot((scores[hd] * tables[hd][0][...]).astype(bf16), v[hd]) for hd in heads]
        for hd in heads:
            o = inner[hd] + cross[hd] * tables[hd][1][...]
            mu = jnp.mean(o, axis=-1, keepdims=True)
            oc = o - mu
            var = jnp.mean(oc * oc, axis=-1, keepdims=True)
            y = oc * lax.rsqrt(var + LN_EPS) * gate_ref[rows, cols[hd]]
            mix_ref[rows, cols[hd]] = y.astype(bf16)

    return new_states, finish


def _att_update(m_ref, acc_ref, hd, lane0, s, v_t, ok):
    if ok is not None:
        s = jnp.where(ok, s, -jnp.inf)
    m = m_ref[hd, :, lane0:]
    m_new = jnp.maximum(m, jnp.max(s, axis=0, keepdims=True))
    p = jnp.exp2(s - m_new)
    m_ref[hd, :, lane0:] = m_new
    acc_ref[hd, :, lane0:] = jnp.exp2(m - m_new) * acc_ref[hd, :, lane0:] + _dot(v_t, p.astype(bf16))


def _mixer_kernel(qt_ref, km_ref, vt_ref, kmeta_ref, vmeta_ref, qr_ref, kr_ref, vr_ref, gate_ref,
                  krmeta_ref, vrmeta_ref, mix_ref, state_ref,
                  m_ref, acc_ref, s_ref, dmask_ref, qdec_ref, kdec_ref):
    qi = pl.program_id(1)

    @pl.when((pl.program_id(0) == 0) & (qi == 0))
    def _():
        for hd in range(RET_HEADS):
            dmask_ref[hd], qdec_ref[hd], kdec_ref[hd] = _retention_tables(RET_C, hd)

    @pl.when(qi == 0)
    def _():
        for hd in range(RET_HEADS):
            lo = hd * RET_DK
            lg = RET_LOG_GAMMA[hd]
            kd = (krmeta_ref[:, lo:lo + RET_DK].astype(f32)
                  * _row_pow(N_META, lg, N_META - 1.0, -1.0)).astype(bf16)
            state_ref[0, hd] = _dot_tn(kd, vrmeta_ref[:, lo:lo + RET_DV])

    heads = range(MLA_HEADS)
    units = [(kb, hd) for kb in range(ATT_TK // ATT_KB) for hd in heads]

    def scores(start, kb, hd, lane0=0):
        k = km_ref[0, pl.ds(start + kb * ATT_KB, ATT_KB), hd * MLA_QK_PAD:(hd + 1) * MLA_QK_PAD]
        return _dot(k, qt_ref[0, hd * MLA_QK_PAD:(hd + 1) * MLA_QK_PAD, lane0:])

    def v_block(kt, kb, hd):
        return vt_ref[kt, hd * MLA_VT_ROWS:(hd + 1) * MLA_VT_ROWS, kb * ATT_KB:(kb + 1) * ATT_KB]

    first_scores = [functools.partial(scores, 0, *units[d]) for d in range(ATT_AHEAD)]

    def trace_first_scores(n):
        for _ in range(min(n, len(first_scores))):
            d = ATT_AHEAD - len(first_scores)
            s_ref[d] = first_scores.pop(0)()

    tables = [(dmask_ref.at[hd], qdec_ref.at[hd], kdec_ref.at[hd]) for hd in range(RET_HEADS)]
    states = [state_ref[0, hd] for hd in range(RET_HEADS)]
    finishes = []
    for c in range(ATT_TQ // RET_C):
        states, finish_chunk = _retention_chunk(qr_ref, kr_ref, vr_ref, gate_ref, mix_ref,
                                                slice(c * RET_C, (c + 1) * RET_C), states, tables)
        finishes.append(finish_chunk)
    for hd in range(RET_HEADS):
        state_ref[0, hd] = states[hd]
    per_stage = -(-ATT_AHEAD // len(finishes))
    for finish_chunk in finishes:
        trace_first_scores(per_stage)
        finish_chunk()
    trace_first_scores(ATT_AHEAD)
    s_meta = [_dot(kmeta_ref[:, hd * MLA_QK_PAD:(hd + 1) * MLA_QK_PAD],
                   qt_ref[0, hd * MLA_QK_PAD:(hd + 1) * MLA_QK_PAD, :]) for hd in heads]

    p_meta = []
    for hd in heads:
        m = jnp.max(s_meta[hd], axis=0, keepdims=True)
        m_ref[hd] = m
        p_meta.append(jnp.exp2(s_meta[hd] - m).astype(bf16))
    ones = jnp.ones((N_META, MLA_VT_ROWS - MLA_V), bf16)
    for hd in heads:
        v_ext = jnp.concatenate([vmeta_ref[:, hd * MLA_V:(hd + 1) * MLA_V], ones], axis=1)
        acc_ref[hd] = _dot_tn(v_ext, p_meta[hd])

    def full_step(kt, _):
        start = pl.multiple_of(kt * ATT_TK, ATT_TK)
        pending = [s_ref[d] for d in range(ATT_AHEAD)]
        for u, (kb, hd) in enumerate(units):
            ahead = u + ATT_AHEAD
            if ahead < len(units):
                pending.append(scores(start, *units[ahead]))
            else:
                pending.append(scores(pl.multiple_of(start + ATT_TK, ATT_TK), *units[ahead - len(units)]))
            _att_update(m_ref, acc_ref, hd, 0, pending.pop(0), v_block(kt, kb, hd), None)
        for d in range(ATT_AHEAD):
            s_ref[d] = pending[d]
        return 0

    lax.fori_loop(0, qi, full_step, 0)

    start = pl.multiple_of(qi * ATT_TQ, ATT_TQ)
    pending = [s_ref[d] for d in range(ATT_AHEAD)]
    for u, (kb, hd) in enumerate(units):
        lane0 = kb * ATT_KB
        if u + ATT_AHEAD < len(units):
            nxt = units[u + ATT_AHEAD]
            pending.append(scores(start, *nxt, lane0=nxt[0] * ATT_KB))
        s_cur = pending.pop(0)
        k_chunk = lax.broadcasted_iota(jnp.int32, s_cur.shape, 0) // CHUNK
        q_chunk = lax.broadcasted_iota(jnp.int32, s_cur.shape, 1) // CHUNK
        _att_update(m_ref, acc_ref, hd, lane0, s_cur, v_block(qi, kb, hd), k_chunk <= q_chunk)
    for hd in heads:
        out = (acc_ref[hd, :MLA_V, :] / acc_ref[hd, MLA_V:MLA_V + 1, :]).T
        mix_ref[:, RET_W + hd * MLA_V:RET_W + (hd + 1) * MLA_V] = out.astype(bf16)


def _mixer_call(qt, km, vt, kmeta, vmeta, qr, kr, vr, gate, krmeta, vrmeta, batch, seq):
    nq = seq // ATT_TQ
    tile = lambda w: pl.BlockSpec((ATT_TQ, w), lambda b, i: (b * nq + i, 0))
    const = lambda a: pl.BlockSpec((N_META, a.shape[1]), lambda b, i: (a.shape[0] // N_META - 1, 0))
    return pl.pallas_call(
        _mixer_kernel,
        grid=(batch, nq),
        in_specs=[pl.BlockSpec((1, MLA_QW, ATT_TQ), lambda b, i: (b * nq + i, 0, 0)),
                  pl.BlockSpec((1, seq, MLA_QW), lambda b, i: (b, 0, 0)),
                  pl.BlockSpec((seq // ATT_TK, MLA_HEADS * MLA_VT_ROWS, ATT_TK), lambda b, i: (b, 0, 0)),
                  const(kmeta), const(vmeta),
                  tile(RET_W), tile(RET_W), tile(RET_W), tile(RET_W),
                  const(krmeta), const(vrmeta)],
        out_specs=[tile(D_MODEL),
                   pl.BlockSpec((1, RET_HEADS, RET_DK, RET_DV), lambda b, i: (b, 0, 0, 0))],
        out_shape=[jax.ShapeDtypeStruct((batch * seq, D_MODEL), bf16),
                   jax.ShapeDtypeStruct((batch, RET_HEADS, RET_DK, RET_DV), f32)],
        scratch_shapes=[pltpu.VMEM((MLA_HEADS, 1, ATT_TQ), f32),
                        pltpu.VMEM((MLA_HEADS, MLA_VT_ROWS, ATT_TQ), f32),
                        pltpu.VMEM((ATT_AHEAD, ATT_KB, ATT_TQ), f32),
                        pltpu.VMEM((RET_HEADS, RET_C, RET_C), f32),
                        pltpu.VMEM((RET_HEADS, RET_C, LANES), f32),
                        pltpu.VMEM((RET_HEADS, RET_C, LANES), f32)],
        compiler_params=pltpu.CompilerParams(dimension_semantics=("arbitrary", "arbitrary"),
                                             vmem_limit_bytes=VMEM_LIMIT),
        name="mixer",
    )(qt, km.reshape(batch, seq, MLA_QW), vt, kmeta, vmeta, qr, kr, vr, gate, krmeta, vrmeta)


def _smixer_kernel(qm_ref, knew_ref, vnew_ref, kmeta_ref, vmeta_ref, cckv_ref, ckpe_ref, w_ukv_ref,
                   qr_ref, kr_ref, vr_ref, gate_ref, s0_ref, mix_ref, state_ref):
    t = qr_ref.shape[0]
    states, finish_retention = _retention_chunk(qr_ref, kr_ref, vr_ref, gate_ref, mix_ref, slice(0, t),
                                                [s0_ref[0, hd] for hd in range(RET_HEADS)],
                                                [_retention_tables(t, hd) for hd in range(RET_HEADS)])
    for hd in range(RET_HEADS):
        state_ref[0, hd] = states[hd]
    finish_retention()

    heads = range(MLA_HEADS)
    kcols = [slice(hd * MLA_QK_PAD, (hd + 1) * MLA_QK_PAD) for hd in heads]
    vcols = [slice(hd * MLA_V, (hd + 1) * MLA_V) for hd in heads]
    kv_c = _dot(cckv_ref[0].astype(bf16), w_ukv_ref[...])
    kpe_c = ckpe_ref[0].astype(bf16)
    q = [qm_ref[:, kcols[hd]] for hd in heads]
    s_m = [_dot_nt(q[hd], kmeta_ref[:, kcols[hd]]) for hd in heads]
    s_n = [_dot_nt(q[hd], knew_ref[:, kcols[hd]]) for hd in heads]
    s_r = [_dot_nt(q[hd][:, LANES:LANES + MLA_ROPE], kpe_c) for hd in heads]
    s_c = [_dot_nt(q[hd][:, :LANES], kv_c[:, hd * MLA_NOPE:(hd + 1) * MLA_NOPE].astype(bf16)) + s_r[hd]
           for hd in heads]
    probs, denom = [], []
    for hd in heads:
        parts = [s_m[hd] * MLA_SCALE, s_c[hd] * MLA_SCALE, s_n[hd] * MLA_SCALE]
        m = functools.reduce(jnp.maximum, [jnp.max(s, axis=-1, keepdims=True) for s in parts])
        ps = [jnp.exp(s - m) for s in parts]
        denom.append(functools.reduce(jnp.add, [jnp.sum(p, axis=-1, keepdims=True) for p in ps]))
        probs.append([p.astype(bf16) for p in ps])
    for hd in heads:
        v_c = kv_c[:, MLA_HEADS * MLA_NOPE + hd * MLA_V:MLA_HEADS * MLA_NOPE + (hd + 1) * MLA_V].astype(bf16)
        acc = (_dot(probs[hd][0], vmeta_ref[:, vcols[hd]]) + _dot(probs[hd][1], v_c)
               + _dot(probs[hd][2], vnew_ref[:, vcols[hd]]))
        mix_ref[:, RET_W + hd * MLA_V:RET_W + (hd + 1) * MLA_V] = (acc / denom[hd]).astype(bf16)


def _smixer_call(qm, km, vm, cckv, ckpe, w_ukv, qr, kr, vr, gate, s0, t):
    db, past = cckv.shape[0], cckv.shape[1]
    tile = lambda w: pl.BlockSpec((t, w), lambda b: (b, 0))
    meta = lambda w: pl.BlockSpec((N_META, w), lambda b: (db * t // N_META, 0))
    const = lambda a: pl.BlockSpec(a.shape, lambda b: (0,) * a.ndim)
    per = lambda *tail: pl.BlockSpec((1,) + tail, lambda b: (b,) + (0,) * len(tail))
    return pl.pallas_call(
        _smixer_kernel,
        grid=(db,),
        in_specs=[tile(MLA_QW), tile(MLA_QW), tile(MLA_VW), meta(MLA_QW), meta(MLA_VW),
                  per(past, MLA_KV_LORA), per(past, MLA_ROPE), const(w_ukv),
                  tile(RET_W), tile(RET_W), tile(RET_W), tile(RET_W),
                  per(RET_HEADS, RET_DK, RET_DV)],
        out_specs=[tile(D_MODEL), per(RET_HEADS, RET_DK, RET_DV)],
        out_shape=[jax.ShapeDtypeStruct((db * t, D_MODEL), bf16),
                   jax.ShapeDtypeStruct((db, RET_HEADS, RET_DK, RET_DV), f32)],
        compiler_params=pltpu.CompilerParams(dimension_semantics=("arbitrary",),
                                             vmem_limit_bytes=VMEM_LIMIT),
        name="smixer",
    )(qm, km, vm, km, vm, cckv, ckpe, w_ukv, qr, kr, vr, gate, s0)


def _tail_kernel(sub, mix_ref, h_ref, w_mo_ref, ln2_g_ref, ln2_b_ref, w_in_ref, w_out_ref, ln3_g_ref, ln3_b_ref,
                 y_ref):
    tiles = [slice(j * sub, (j + 1) * sub) for j in range(mix_ref.shape[0] // sub)]
    mixes = [_dot(mix_ref[r, :], w_mo_ref[...]) for r in tiles]
    h2s, ys = [], []
    for r, mix in zip(tiles, mixes):
        h2 = _layer_norm(DN_ALPHA * h_ref[r, :] + mix, ln2_g_ref[...], ln2_b_ref[...])
        h2s.append(h2)
        ys.append(_swiglu(h2, w_in_ref, w_out_ref))
    for r, h2, y in zip(tiles, h2s, ys):
        y_ref[r, :] = _ffn_norm(h2, y, ln3_g_ref, ln3_b_ref)


def _tail_call(mix, h, wts, tm, sub):
    n = mix.shape[0]
    row = lambda w: pl.BlockSpec((tm, w), lambda i: (i, 0))
    return pl.pallas_call(
        functools.partial(_tail_kernel, sub),
        grid=(n // tm,),
        in_specs=[row(D_MODEL), row(D_MODEL)] + [_const_spec(w.shape) for w in wts],
        out_specs=row(D_MODEL),
        out_shape=jax.ShapeDtypeStruct((n, D_MODEL), f32),
        compiler_params=pltpu.CompilerParams(dimension_semantics=("arbitrary",),
                                             vmem_limit_bytes=VMEM_LIMIT),
        name="tail",
    )(mix, h, *wts)


def _rope_tables(pos):
    posf = pos.astype(f32)[:, None]
    inv_r = ROPE_BASE ** (-jnp.arange(0, RET_DK, 2, dtype=f32) / RET_DK)
    ang = posf * inv_r[None, :]
    inv_m = ROPE_BASE ** (-jnp.arange(0, MLA_ROPE, 2, dtype=f32) / MLA_ROPE)
    angm = posf * inv_m[None, :]
    return jnp.cos(ang), jnp.sin(ang), jnp.cos(angm), jnp.sin(angm)


def kernel(x_prompt, x_sample, cache_mla_ckv, cache_mla_kpe, state_ret, meta_tokens,
           ffn1_w_in, ffn1_w_out, ln1_g, ln1_b, w_mix_in, ret_gn_g, mla_q_norm_g, mla_w_uq,
           mla_kv_norm_g, mla_w_ukv, w_mix_out, ln2_g, ln2_b, ffn2_w_in, ffn2_w_out, ln3_g, ln3_b):
    B, S, D = x_prompt.shape
    DB, T, _ = x_sample.shape
    P = cache_mla_ckv.shape[1]
    assert S % ATT_TQ == 0 and S % FRONT_TM == 0 and (B * S) % TAIL_TM == 0 and ATT_TQ % RET_C == 0
    row = lambda a: a.reshape(1, -1).astype(f32)

    w_mix = w_mix_in.astype(bf16)
    w_uq = mla_w_uq.reshape(MLA_Q_LORA, MLA_HEADS, MLA_NOPE + MLA_ROPE)
    w_uq = jnp.pad(w_uq, ((0, 0), (0, 0), (0, MLA_QK_PAD - MLA_NOPE - MLA_ROPE)))
    w_uq = w_uq.reshape(MLA_Q_LORA, MLA_QW).astype(bf16)
    w_ukv = mla_w_ukv.reshape(MLA_KV_LORA, MLA_HEADS, MLA_NOPE + MLA_V)
    w_uk = w_ukv[:, :, :MLA_NOPE].reshape(MLA_KV_LORA, -1).astype(bf16)
    w_uv = w_ukv[:, :, MLA_NOPE:].reshape(MLA_KV_LORA, -1).astype(bf16)
    shared_w = [ffn1_w_in.astype(bf16), ffn1_w_out.astype(bf16), row(ln1_g), row(ln1_b), w_mix,
                row(ret_gn_g), row(mla_q_norm_g)]
    front_w = shared_w + [w_uq, row(mla_kv_norm_g), w_uk, w_uv]
    front_t_w = shared_w + [w_uq.T, row(mla_kv_norm_g), w_uk, w_uv.T]
    tail_w = [w_mix_out.astype(bf16), row(ln2_g), row(ln2_b), ffn2_w_in.astype(bf16),
              ffn2_w_out.astype(bf16), row(ln3_g), row(ln3_b)]

    tabs_p = _rope_tables(N_META + jnp.arange(FRONT_TM))
    steps_p = _rope_tables(FRONT_TM * jnp.arange(S // FRONT_TM))
    tabs_t_p = (jnp.concatenate([tabs_p[2].T, tabs_p[3].T], axis=0),
                jnp.concatenate([steps_p[2], steps_p[3]], axis=1)[:, :, None])
    (h_p, qr_p, kr_p, vr_p, gr_p, qt_p, km_p, vt_p, ckv_p, kpe_p) = _front_call(
        x_prompt.reshape(B * S, D), tabs_p, jnp.concatenate(steps_p, axis=1)[:, None, :], tabs_t_p, front_t_w,
        FRONT_TM, FRONT_SUB)
    n_s = DB * T
    x_small = jnp.concatenate([x_sample.reshape(n_s, D), meta_tokens.astype(x_sample.dtype)], axis=0)
    pos_small = jnp.concatenate([jnp.tile(N_META + P + jnp.arange(T), DB), jnp.arange(N_META)])
    (h_s, qr_s, kr_s, vr_s, gr_s, qm_s, km_s, vm_s, ckv_s, kpe_s) = _front_call(
        x_small, _rope_tables(pos_small), jnp.concatenate(_rope_tables(jnp.zeros((1,), jnp.int32)), axis=1)[:, None, :],
        None, front_w, n_s + N_META, n_s + N_META)

    mix_p, p_state = _mixer_call(qt_p, km_p, vt_p, km_s, vm_s, qr_p, kr_p, vr_p, gr_p, kr_s, vr_s, B, S)
    mix_s, s_state = _smixer_call(qm_s, km_s, vm_s, cache_mla_ckv, cache_mla_kpe,
                                  jnp.concatenate([w_uk, w_uv], axis=1), qr_s, kr_s, vr_s, gr_s, state_ret, T)

    y_p = _tail_call(mix_p, h_p, tail_w, TAIL_TM, FRONT_SUB)
    y_s = _tail_call(mix_s, h_s, tail_w, n_s, n_s // 2)

    meta_ckv = jnp.broadcast_to(ckv_s[n_s:][None], (B, N_META, MLA_KV_LORA))
    meta_kpe = jnp.broadcast_to(kpe_s[n_s:][None], (B, N_META, MLA_ROPE))
    p_ckv = jnp.concatenate([meta_ckv, ckv_p.reshape(B, S, MLA_KV_LORA)], axis=1)
    p_kpe = jnp.concatenate([meta_kpe, kpe_p.reshape(B, S, MLA_ROPE)], axis=1)
    return (y_p.reshape(B, S, D), y_s.reshape(DB, T, D), p_ckv, p_kpe, p_state.astype(x_prompt.dtype),
            ckv_s[:n_s].reshape(DB, T, MLA_KV_LORA), kpe_s[:n_s].reshape(DB, T, MLA_ROPE),
            s_state.astype(state_ret.dtype))
```

```python
import functools
import math

import jax
import jax.numpy as jnp
from jax import lax
from jax.experimental import pallas as pl
from jax.experimental.pallas import tpu as pltpu

D_MODEL = 1024
DEPTH = 1
CHUNK = 64
N_META = 16
RET_HEADS = 4
RET_DK = 128
RET_DV = 128
MLA_HEADS = 4
MLA_NOPE = 128
MLA_ROPE = 64
MLA_V = 128
MLA_Q_LORA = 256
MLA_KV_LORA = 128
D_FF = 2816
ROPE_BASE = 10000.0
LN_EPS = 1e-5
RMS_EPS = 1e-6
DN_ALPHA = (2 * DEPTH) ** 0.25

OFF_KR = RET_HEADS * RET_DK
OFF_VR = 2 * RET_HEADS * RET_DK
OFF_GR = OFF_VR + RET_HEADS * RET_DV
OFF_CQ = OFF_GR + RET_HEADS * RET_DV
OFF_CKV = OFF_CQ + MLA_Q_LORA
OFF_KPE = OFF_CKV + MLA_KV_LORA
D_IN = OFF_KPE + MLA_ROPE

LANES = 128
MLA_QK_PAD = 2 * LANES
RET_W = RET_HEADS * RET_DK
MLA_QW = MLA_HEADS * MLA_QK_PAD
MLA_VW = MLA_HEADS * MLA_V
BF16_SUBLANES = 16
MLA_VT_ROWS = MLA_V + BF16_SUBLANES

FRONT_SUB = 256
FRONT_TM = 2 * FRONT_SUB
TAIL_TM = 4 * FRONT_SUB
ATT_TQ = FRONT_TM
ATT_TK = FRONT_TM
ATT_KB = 512
ATT_AHEAD = 4
RET_C = 256
VMEM_LIMIT = 56 * 1024 * 1024

MLA_SCALE = (MLA_NOPE + MLA_ROPE) ** -0.5
MLA_SCALE_LOG2E = MLA_SCALE * math.log2(math.e)
RET_LOG_GAMMA = tuple(math.log(1.0 - 2.0 ** (-5.0 - h)) for h in range(RET_HEADS))

f32 = jnp.float32
bf16 = jnp.bfloat16


def _dot(a, b):
    return jnp.dot(a, b, preferred_element_type=f32)


def _dot_nt(a, b):
    return lax.dot_general(a, b, (((1,), (1,)), ((), ())), preferred_element_type=f32)


def _dot_tn(a, b):
    return lax.dot_general(a, b, (((0,), (0,)), ((), ())), preferred_element_type=f32)


def _layer_norm(x, g, b):
    mu = jnp.mean(x, axis=-1, keepdims=True)
    xc = x - mu
    var = jnp.mean(xc * xc, axis=-1, keepdims=True)
    return xc * lax.rsqrt(var + LN_EPS) * g + b


def _rms_norm(x, g):
    return x * lax.rsqrt(jnp.mean(x * x, axis=-1, keepdims=True) + RMS_EPS) * g


def _silu(x):
    return x / (1.0 + jnp.exp(-x))


def _swiglu(x, w_in_ref, w_out_ref):
    xb = x.astype(bf16)
    hg = _dot(xb, w_in_ref[:, :D_FF])
    hu = _dot(xb, w_in_ref[:, D_FF:])
    return _dot((_silu(hg) * hu).astype(bf16), w_out_ref[...])


def _ffn_norm(x, y, g_ref, b_ref):
    return _layer_norm(DN_ALPHA * x + 0.5 * y, g_ref[...], b_ref[...])


def _rotate(cb, sb, ca, sa):
    return cb * ca - sb * sa, sb * ca + cb * sa


def _front_kernel(transposed, sub, x_ref, cr_ref, sr_ref, cm_ref, sm_ref, step_ref, *refs):
    if transposed:
        tab_t_ref, step_t_ref, refs = refs[0], refs[1], refs[2:]
    (w_in_ref, w_out_ref, ln_g_ref, ln_b_ref, w_mix_ref, gn_ref, qn_g_ref, w_uq_ref, kvn_g_ref,
     w_uk_ref, w_uv_ref,
     h_ref, qr_ref, kr_ref, vr_ref, gate_ref, qm_ref, km_ref, vm_ref, ckv_ref, kpe_ref) = refs
    n_sub = x_ref.shape[0] // sub
    tiles = [slice(j * sub, (j + 1) * sub) for j in range(n_sub)]

    xs = [x_ref[r, :] for r in tiles]
    ys = [_swiglu(x, w_in_ref, w_out_ref) for x in xs]
    ps = []
    for r, x, y in zip(tiles, xs, ys):
        h = _ffn_norm(x, y, ln_g_ref, ln_b_ref)
        h_ref[r, :] = h
        hb = h.astype(bf16)
        p_kpe = _dot(hb, w_mix_ref[:, OFF_KPE:])
        ps.append((_dot(hb, w_mix_ref[:, :OFF_KPE]),
                   jnp.concatenate([p_kpe, jnp.zeros_like(p_kpe)], axis=1)))

    for j, (r, (p, p_kpe)) in enumerate(zip(tiles, ps)):
        half_r, half_m = RET_DK // 2, MLA_ROPE // 2
        c, s = _rotate(cr_ref[r, :], sr_ref[r, :], step_ref[0, :, :half_r], step_ref[0, :, half_r:2 * half_r])
        cm, sm = _rotate(cm_ref[r, :], sm_ref[r, :], step_ref[0, :, 2 * half_r:2 * half_r + half_m],
                         step_ref[0, :, 2 * half_r + half_m:])
        z_q, z_h = jnp.zeros_like(cm), jnp.zeros_like(c)
        c_r = jnp.concatenate([c, c], axis=1)
        s_r = jnp.concatenate([-s, s], axis=1)
        c_m = jnp.concatenate([cm, cm, z_h], axis=1)
        s_ma = jnp.concatenate([-sm, z_q, z_h], axis=1)
        s_mb = jnp.concatenate([z_q, sm, z_h], axis=1)

        def rope_ret(xh):
            return xh * c_r + pltpu.roll(xh, 64, 1) * s_r

        def rope_mla(xh):
            return xh * c_m + pltpu.roll(xh, 96, 1) * s_ma + pltpu.roll(xh, 32, 1) * s_mb

        cq = _rms_norm(p[:, OFF_CQ:OFF_CKV], qn_g_ref[...]).astype(bf16)
        ckv = _rms_norm(p[:, OFF_CKV:OFF_KPE], kvn_g_ref[...])
        ckv_ref[r, :] = ckv
        ckv_b = ckv.astype(bf16)
        k_nope = _dot(ckv_b, w_uk_ref[...])
        if transposed:
            q_t = _dot_nt(w_uq_ref[...], cq) * MLA_SCALE_LOG2E
            v_t = _dot_nt(w_uv_ref[...], ckv_b)
        else:
            q = _dot(cq, w_uq_ref[...])
            v = _dot(ckv_b, w_uv_ref[...])

        for hd in range(RET_HEADS):
            lo = hd * RET_DK
            qr_ref[r, lo:lo + RET_DK] = rope_ret(p[:, lo:lo + RET_DK]).astype(bf16)
            kr_ref[r, lo:lo + RET_DK] = (rope_ret(p[:, OFF_KR + lo:OFF_KR + lo + RET_DK])
                                         * RET_DK ** -0.5).astype(bf16)
        vr_ref[r, :] = p[:, OFF_VR:OFF_GR].astype(bf16)
        gate_ref[r, :] = gn_ref[...] * _silu(p[:, OFF_GR:OFF_CQ])
        kpe = rope_mla(p_kpe)
        kpe_ref[r, :] = kpe[:, :MLA_ROPE]
        kpe_b = kpe.astype(bf16)
        for hd in range(MLA_HEADS):
            lo = hd * MLA_QK_PAD
            km_ref[r, lo:lo + LANES] = k_nope[:, hd * MLA_NOPE:(hd + 1) * MLA_NOPE].astype(bf16)
            km_ref[r, lo + LANES:lo + 2 * LANES] = kpe_b

        if transposed:
            half = MLA_ROPE // 2
            c_t, s_t = _rotate(tab_t_ref[:half, r], tab_t_ref[half:, r], step_t_ref[0, :half, :], step_t_ref[0, half:, :])
            for hd in range(MLA_HEADS):
                lo = hd * MLA_QK_PAD
                r0 = lo + MLA_NOPE
                x1 = q_t[r0:r0 + half, :]
                x2 = q_t[r0 + half:r0 + 2 * half, :]
                qm_ref[0, lo:r0, r] = q_t[lo:r0, :].astype(bf16)
                qm_ref[0, r0:r0 + half, r] = (x1 * c_t - x2 * s_t).astype(bf16)
                qm_ref[0, r0 + half:r0 + 2 * half, r] = (x1 * s_t + x2 * c_t).astype(bf16)
                qm_ref[0, r0 + 2 * half:lo + MLA_QK_PAD, r] = q_t[r0 + 2 * half:lo + MLA_QK_PAD, :].astype(bf16)
            ones = jnp.ones((MLA_VT_ROWS - MLA_V, v_t.shape[1]), bf16)
            for hd in range(MLA_HEADS):
                vm_ref[0, hd * MLA_VT_ROWS:hd * MLA_VT_ROWS + MLA_V, r] = v_t[hd * MLA_V:(hd + 1) * MLA_V, :].astype(bf16)
                vm_ref[0, hd * MLA_VT_ROWS + MLA_V:(hd + 1) * MLA_VT_ROWS, r] = ones
        else:
            for hd in range(MLA_HEADS):
                lo = hd * MLA_QK_PAD
                qm_ref[r, lo:lo + LANES] = q[:, lo:lo + LANES].astype(bf16)
                qm_ref[r, lo + LANES:lo + 2 * LANES] = rope_mla(q[:, lo + LANES:lo + 2 * LANES]).astype(bf16)
            vm_ref[r, :] = v.astype(bf16)


def _const_spec(shape):
    nd = len(shape)
    return pl.BlockSpec(shape, lambda *_: (0,) * nd, pipeline_mode=pl.Buffered(1))


def _front_call(x, tabs, steps, tabs_t, wts, tm, sub):
    n = x.shape[0]
    transposed = tabs_t is not None
    period = steps.shape[0]
    row = lambda w: pl.BlockSpec((tm, w), lambda i: (i, 0))
    col = lambda w: pl.BlockSpec((1, w, tm), lambda i: (i, 0, 0))
    in_specs = [row(D_MODEL)] + [pl.BlockSpec((tm, t.shape[1]), lambda i: (0, 0)) for t in tabs]
    in_specs.append(pl.BlockSpec((1,) + steps.shape[1:], lambda i: (i % period, 0, 0)))
    args = [x, *tabs, steps]
    if transposed:
        in_specs.append(pl.BlockSpec((MLA_ROPE, tm), lambda i: (0, 0)))
        in_specs.append(pl.BlockSpec((1,) + tabs_t[1].shape[1:], lambda i: (i % period, 0, 0)))
        args += list(tabs_t)
    in_specs += [_const_spec(w.shape) for w in wts]
    rows = lambda w, dt: (row(w), jax.ShapeDtypeStruct((n, w), dt))
    cols = lambda w, dt: (col(w), jax.ShapeDtypeStruct((n // tm, w, tm), dt))
    qv = cols if transposed else rows
    outs = [rows(D_MODEL, f32), rows(RET_W, bf16), rows(RET_W, bf16), rows(RET_W, bf16), rows(RET_W, f32),
            qv(MLA_QW, bf16), rows(MLA_QW, bf16),
            cols(MLA_HEADS * MLA_VT_ROWS, bf16) if transposed else rows(MLA_VW, bf16),
            rows(MLA_KV_LORA, f32), rows(MLA_ROPE, f32)]
    return pl.pallas_call(
        functools.partial(_front_kernel, transposed, sub),
        grid=(n // tm,),
        in_specs=in_specs,
        out_specs=[o[0] for o in outs],
        out_shape=[o[1] for o in outs],
        compiler_params=pltpu.CompilerParams(dimension_semantics=("arbitrary",),
                                             vmem_limit_bytes=VMEM_LIMIT),
        name="front_t" if transposed else "front",
    )(*args, *wts)


def _decay_mask(c, lg):
    r = lax.broadcasted_iota(jnp.int32, (c, c), 0)
    k = lax.broadcasted_iota(jnp.int32, (c, c), 1)
    rel = (r - k).astype(f32)
    return jnp.where(rel >= 0.0, jnp.exp(lg * jnp.maximum(rel, 0.0)), 0.0)


def _row_pow(c, lg, offset, sign):
    i = lax.broadcasted_iota(jnp.int32, (c, LANES), 0).astype(f32)
    return jnp.exp(lg * (offset + sign * i))


def _retention_tables(c, hd):
    lg = RET_LOG_GAMMA[hd]
    return _decay_mask(c, lg), _row_pow(c, lg, 1.0, 1.0), _row_pow(c, lg, c - 1.0, -1.0)


def _retention_chunk(qr_ref, kr_ref, vr_ref, gate_ref, mix_ref, rows, states, tables):
    c = rows.stop - rows.start
    heads = range(RET_HEADS)
    cols = [slice(hd * RET_DK, (hd + 1) * RET_DK) for hd in heads]
    q = [qr_ref[rows, cols[hd]] for hd in heads]
    k = [kr_ref[rows, cols[hd]] for hd in heads]
    v = [vr_ref[rows, cols[hd]] for hd in heads]
    scores = [_dot_nt(q[hd], k[hd]) for hd in heads]
    cross = [_dot(q[hd], states[hd].astype(bf16)) for hd in heads]
    kv = [_dot_tn((k[hd].astype(f32) * tables[hd][2][...]).astype(bf16), v[hd]) for hd in heads]
    new_states = [math.exp(RET_LOG_GAMMA[hd] * c) * states[hd] + kv[hd] for hd in heads]

    def finish():
        inner = [_dot((scores[hd] * tables[hd][0][...]).astype(bf16), v[hd]) for hd in heads]
        for hd in heads:
            o = inner[hd] + cross[hd] * tables[hd][1][...]
            mu = jnp.mean(o, axis=-1, keepdims=True)
            oc = o - mu
            var = jnp.mean(oc * oc, axis=-1, keepdims=True)
            y = oc * lax.rsqrt(var + LN_EPS) * gate_ref[rows, cols[hd]]
            mix_ref[rows, cols[hd]] = y.astype(bf16)

    return new_states, finish


def _att_update(m_ref, acc_ref, hd, lane0, s, v_t, ok):
    if ok is not None:
        s = jnp.where(ok, s, -jnp.inf)
    m = m_ref[hd, :, lane0:]
    m_new = jnp.maximum(m, jnp.max(s, axis=0, keepdims=True))
    p = jnp.exp2(s - m_new)
    m_ref[hd, :, lane0:] = m_new
    acc_ref[hd, :, lane0:] = jnp.exp2(m - m_new) * acc_ref[hd, :, lane0:] + _dot(v_t, p.astype(bf16))


def _mixer_kernel(qt_ref, km_ref, vt_ref, kmeta_ref, vmeta_ref, qr_ref, kr_ref, vr_ref, gate_ref,
                  krmeta_ref, vrmeta_ref, w_f32_ref, mix_ref, state_ref, w_bf16_ref,
                  m_ref, acc_ref, s_ref, dmask_ref, qdec_ref, kdec_ref):
    qi = pl.program_id(1)

    w_bf16_ref[...] = w_f32_ref[...].astype(bf16)

    @pl.when((pl.program_id(0) == 0) & (qi == 0))
    def _():
        for hd in range(RET_HEADS):
            dmask_ref[hd], qdec_ref[hd], kdec_ref[hd] = _retention_tables(RET_C, hd)

    @pl.when(qi == 0)
    def _():
        for hd in range(RET_HEADS):
            lo = hd * RET_DK
            lg = RET_LOG_GAMMA[hd]
            kd = (krmeta_ref[:, lo:lo + RET_DK].astype(f32)
                  * _row_pow(N_META, lg, N_META - 1.0, -1.0)).astype(bf16)
            state_ref[0, hd] = _dot_tn(kd, vrmeta_ref[:, lo:lo + RET_DV])

    heads = range(MLA_HEADS)
    units = [(kb, hd) for kb in range(ATT_TK // ATT_KB) for hd in heads]

    def scores(start, kb, hd, lane0=0):
        k = km_ref[0, pl.ds(start + kb * ATT_KB, ATT_KB), hd * MLA_QK_PAD:(hd + 1) * MLA_QK_PAD]
        return _dot(k, qt_ref[0, hd * MLA_QK_PAD:(hd + 1) * MLA_QK_PAD, lane0:])

    def v_block(kt, kb, hd):
        return vt_ref[kt, hd * MLA_VT_ROWS:(hd + 1) * MLA_VT_ROWS, kb * ATT_KB:(kb + 1) * ATT_KB]

    first_scores = [functools.partial(scores, 0, *units[d]) for d in range(ATT_AHEAD)]

    def trace_first_scores(n):
        for _ in range(min(n, len(first_scores))):
            d = ATT_AHEAD - len(first_scores)
            s_ref[d] = first_scores.pop(0)()

    tables = [(dmask_ref.at[hd], qdec_ref.at[hd], kdec_ref.at[hd]) for hd in range(RET_HEADS)]
    states = [state_ref[0, hd] for hd in range(RET_HEADS)]
    finishes = []
    for c in range(ATT_TQ // RET_C):
        states, finish_chunk = _retention_chunk(qr_ref, kr_ref, vr_ref, gate_ref, mix_ref,
                                                slice(c * RET_C, (c + 1) * RET_C), states, tables)
        finishes.append(finish_chunk)
    for hd in range(RET_HEADS):
        state_ref[0, hd] = states[hd]
    per_stage = -(-ATT_AHEAD // len(finishes))
    for finish_chunk in finishes:
        trace_first_scores(per_stage)
        finish_chunk()
    trace_first_scores(ATT_AHEAD)
    s_meta = [_dot(kmeta_ref[:, hd * MLA_QK_PAD:(hd + 1) * MLA_QK_PAD],
                   qt_ref[0, hd * MLA_QK_PAD:(hd + 1) * MLA_QK_PAD, :]) for hd in heads]

    p_meta = []
    for hd in heads:
        m = jnp.max(s_meta[hd], axis=0, keepdims=True)
        m_ref[hd] = m
        p_meta.append(jnp.exp2(s_meta[hd] - m).astype(bf16))
    ones = jnp.ones((N_META, MLA_VT_ROWS - MLA_V), bf16)
    for hd in heads:
        v_ext = jnp.concatenate([vmeta_ref[:, hd * MLA_V:(hd + 1) * MLA_V], ones], axis=1)
        acc_ref[hd] = _dot_tn(v_ext, p_meta[hd])

    def full_step(kt, _):
        start = pl.multiple_of(kt * ATT_TK, ATT_TK)
        pending = [s_ref[d] for d in range(ATT_AHEAD)]
        for u, (kb, hd) in enumerate(units):
            ahead = u + ATT_AHEAD
            if ahead < len(units):
                pending.append(scores(start, *units[ahead]))
            else:
                pending.append(scores(pl.multiple_of(start + ATT_TK, ATT_TK), *units[ahead - len(units)]))
            _att_update(m_ref, acc_ref, hd, 0, pending.pop(0), v_block(kt, kb, hd), None)
        for d in range(ATT_AHEAD):
            s_ref[d] = pending[d]
        return 0

    lax.fori_loop(0, qi, full_step, 0)

    start = pl.multiple_of(qi * ATT_TQ, ATT_TQ)
    pending = [s_ref[d] for d in range(ATT_AHEAD)]
    for u, (kb, hd) in enumerate(units):
        lane0 = kb * ATT_KB
        if u + ATT_AHEAD < len(units):
            nxt = units[u + ATT_AHEAD]
            pending.append(scores(start, *nxt, lane0=nxt[0] * ATT_KB))
        s_cur = pending.pop(0)
        k_chunk = lax.broadcasted_iota(jnp.int32, s_cur.shape, 0) // CHUNK
        q_chunk = lax.broadcasted_iota(jnp.int32, s_cur.shape, 1) // CHUNK
        _att_update(m_ref, acc_ref, hd, lane0, s_cur, v_block(qi, kb, hd), k_chunk <= q_chunk)
    for hd in heads:
        out = (acc_ref[hd, :MLA_V, :] / acc_ref[hd, MLA_V:MLA_V + 1, :]).T
        mix_ref[:, RET_W + hd * MLA_V:RET_W + (hd + 1) * MLA_V] = out.astype(bf16)


def _mixer_call(qt, km, vt, kmeta, vmeta, qr, kr, vr, gate, krmeta, vrmeta, w_f32, batch, seq):
    nq = seq // ATT_TQ
    tile = lambda w: pl.BlockSpec((ATT_TQ, w), lambda b, i: (b * nq + i, 0))
    const = lambda a: pl.BlockSpec((N_META, a.shape[1]), lambda b, i: (a.shape[0] // N_META - 1, 0))
    w_rows, rem = divmod(w_f32.shape[0], batch * nq)
    assert rem == 0 and w_rows % BF16_SUBLANES == 0
    w_slice = pl.BlockSpec((w_rows, w_f32.shape[1]), lambda b, i: (b * nq + i, 0))
    return pl.pallas_call(
        _mixer_kernel,
        grid=(batch, nq),
        in_specs=[pl.BlockSpec((1, MLA_QW, ATT_TQ), lambda b, i: (b * nq + i, 0, 0)),
                  pl.BlockSpec((1, seq, MLA_QW), lambda b, i: (b, 0, 0)),
                  pl.BlockSpec((seq // ATT_TK, MLA_HEADS * MLA_VT_ROWS, ATT_TK), lambda b, i: (b, 0, 0)),
                  const(kmeta), const(vmeta),
                  tile(RET_W), tile(RET_W), tile(RET_W), tile(RET_W),
                  const(krmeta), const(vrmeta), w_slice],
        out_specs=[tile(D_MODEL),
                   pl.BlockSpec((1, RET_HEADS, RET_DK, RET_DV), lambda b, i: (b, 0, 0, 0)), w_slice],
        out_shape=[jax.ShapeDtypeStruct((batch * seq, D_MODEL), bf16),
                   jax.ShapeDtypeStruct((batch, RET_HEADS, RET_DK, RET_DV), f32),
                   jax.ShapeDtypeStruct(w_f32.shape, bf16)],
        scratch_shapes=[pltpu.VMEM((MLA_HEADS, 1, ATT_TQ), f32),
                        pltpu.VMEM((MLA_HEADS, MLA_VT_ROWS, ATT_TQ), f32),
                        pltpu.VMEM((ATT_AHEAD, ATT_KB, ATT_TQ), f32),
                        pltpu.VMEM((RET_HEADS, RET_C, RET_C), f32),
                        pltpu.VMEM((RET_HEADS, RET_C, LANES), f32),
                        pltpu.VMEM((RET_HEADS, RET_C, LANES), f32)],
        compiler_params=pltpu.CompilerParams(dimension_semantics=("arbitrary", "arbitrary"),
                                             vmem_limit_bytes=VMEM_LIMIT),
        name="mixer",
    )(qt, km.reshape(batch, seq, MLA_QW), vt, kmeta, vmeta, qr, kr, vr, gate, krmeta, vrmeta, w_f32)


def _smixer_kernel(qm_ref, knew_ref, vnew_ref, kmeta_ref, vmeta_ref, cckv_ref, ckpe_ref, w_ukv_ref,
                   qr_ref, kr_ref, vr_ref, gate_ref, s0_ref, wa_f32_ref, wb_f32_ref,
                   mix_ref, state_ref, wa_bf16_ref, wb_bf16_ref):
    wa_bf16_ref[...] = wa_f32_ref[...].astype(bf16)
    wb_bf16_ref[...] = wb_f32_ref[...].astype(bf16)

    t = qr_ref.shape[0]
    states, finish_retention = _retention_chunk(qr_ref, kr_ref, vr_ref, gate_ref, mix_ref, slice(0, t),
                                                [s0_ref[0, hd] for hd in range(RET_HEADS)],
                                                [_retention_tables(t, hd) for hd in range(RET_HEADS)])
    for hd in range(RET_HEADS):
        state_ref[0, hd] = states[hd]
    finish_retention()

    heads = range(MLA_HEADS)
    kcols = [slice(hd * MLA_QK_PAD, (hd + 1) * MLA_QK_PAD) for hd in heads]
    vcols = [slice(hd * MLA_V, (hd + 1) * MLA_V) for hd in heads]
    kv_c = _dot(cckv_ref[0].astype(bf16), w_ukv_ref[...])
    kpe_c = ckpe_ref[0].astype(bf16)
    q = [qm_ref[:, kcols[hd]] for hd in heads]
    s_m = [_dot_nt(q[hd], kmeta_ref[:, kcols[hd]]) for hd in heads]
    s_n = [_dot_nt(q[hd], knew_ref[:, kcols[hd]]) for hd in heads]
    s_r = [_dot_nt(q[hd][:, LANES:LANES + MLA_ROPE], kpe_c) for hd in heads]
    s_c = [_dot_nt(q[hd][:, :LANES], kv_c[:, hd * MLA_NOPE:(hd + 1) * MLA_NOPE].astype(bf16)) + s_r[hd]
           for hd in heads]
    probs, denom = [], []
    for hd in heads:
        parts = [s_m[hd] * MLA_SCALE, s_c[hd] * MLA_SCALE, s_n[hd] * MLA_SCALE]
        m = functools.reduce(jnp.maximum, [jnp.max(s, axis=-1, keepdims=True) for s in parts])
        ps = [jnp.exp(s - m) for s in parts]
        denom.append(functools.reduce(jnp.add, [jnp.sum(p, axis=-1, keepdims=True) for p in ps]))
        probs.append([p.astype(bf16) for p in ps])
    for hd in heads:
        v_c = kv_c[:, MLA_HEADS * MLA_NOPE + hd * MLA_V:MLA_HEADS * MLA_NOPE + (hd + 1) * MLA_V].astype(bf16)
        acc = (_dot(probs[hd][0], vmeta_ref[:, vcols[hd]]) + _dot(probs[hd][1], v_c)
               + _dot(probs[hd][2], vnew_ref[:, vcols[hd]]))
        mix_ref[:, RET_W + hd * MLA_V:RET_W + (hd + 1) * MLA_V] = (acc / denom[hd]).astype(bf16)


def _smixer_call(qm, km, vm, cckv, ckpe, w_ukv, qr, kr, vr, gate, s0, wa_f32, wb_f32, t):
    db, past = cckv.shape[0], cckv.shape[1]
    tile = lambda w: pl.BlockSpec((t, w), lambda b: (b, 0))
    meta = lambda w: pl.BlockSpec((N_META, w), lambda b: (db * t // N_META, 0))
    const = lambda a: pl.BlockSpec(a.shape, lambda b: (0,) * a.ndim)
    per = lambda *tail: pl.BlockSpec((1,) + tail, lambda b: (b,) + (0,) * len(tail))

    def w_slice(w):
        rows, rem = divmod(w.shape[0], db)
        assert rem == 0 and rows % BF16_SUBLANES == 0
        return pl.BlockSpec((rows, w.shape[1]), lambda b: (b, 0))

    return pl.pallas_call(
        _smixer_kernel,
        grid=(db,),
        in_specs=[tile(MLA_QW), tile(MLA_QW), tile(MLA_VW), meta(MLA_QW), meta(MLA_VW),
                  per(past, MLA_KV_LORA), per(past, MLA_ROPE), const(w_ukv),
                  tile(RET_W), tile(RET_W), tile(RET_W), tile(RET_W),
                  per(RET_HEADS, RET_DK, RET_DV), w_slice(wa_f32), w_slice(wb_f32)],
        out_specs=[tile(D_MODEL), per(RET_HEADS, RET_DK, RET_DV), w_slice(wa_f32), w_slice(wb_f32)],
        out_shape=[jax.ShapeDtypeStruct((db * t, D_MODEL), bf16),
                   jax.ShapeDtypeStruct((db, RET_HEADS, RET_DK, RET_DV), f32),
                   jax.ShapeDtypeStruct(wa_f32.shape, bf16), jax.ShapeDtypeStruct(wb_f32.shape, bf16)],
        compiler_params=pltpu.CompilerParams(dimension_semantics=("arbitrary",),
                                             vmem_limit_bytes=VMEM_LIMIT),
        name="smixer",
    )(qm, km, vm, km, vm, cckv, ckpe, w_ukv, qr, kr, vr, gate, s0, wa_f32, wb_f32)


def _tail_kernel(sub, mix_ref, h_ref, w_mo_ref, ln2_g_ref, ln2_b_ref, w_in_ref, w_out_ref, ln3_g_ref, ln3_b_ref,
                 y_ref):
    tiles = [slice(j * sub, (j + 1) * sub) for j in range(mix_ref.shape[0] // sub)]
    mixes = [_dot(mix_ref[r, :], w_mo_ref[...]) for r in tiles]
    h2s, ys = [], []
    for r, mix in zip(tiles, mixes):
        h2 = _layer_norm(DN_ALPHA * h_ref[r, :] + mix, ln2_g_ref[...], ln2_b_ref[...])
        h2s.append(h2)
        ys.append(_swiglu(h2, w_in_ref, w_out_ref))
    for r, h2, y in zip(tiles, h2s, ys):
        y_ref[r, :] = _ffn_norm(h2, y, ln3_g_ref, ln3_b_ref)


def _tail_call(mix, h, wts, tm, sub):
    n = mix.shape[0]
    row = lambda w: pl.BlockSpec((tm, w), lambda i: (i, 0))
    return pl.pallas_call(
        functools.partial(_tail_kernel, sub),
        grid=(n // tm,),
        in_specs=[row(D_MODEL), row(D_MODEL)] + [_const_spec(w.shape) for w in wts],
        out_specs=row(D_MODEL),
        out_shape=jax.ShapeDtypeStruct((n, D_MODEL), f32),
        compiler_params=pltpu.CompilerParams(dimension_semantics=("arbitrary",),
                                             vmem_limit_bytes=VMEM_LIMIT),
        name="tail",
    )(mix, h, *wts)


def _rope_tables(pos):
    posf = pos.astype(f32)[:, None]
    inv_r = ROPE_BASE ** (-jnp.arange(0, RET_DK, 2, dtype=f32) / RET_DK)
    ang = posf * inv_r[None, :]
    inv_m = ROPE_BASE ** (-jnp.arange(0, MLA_ROPE, 2, dtype=f32) / MLA_ROPE)
    angm = posf * inv_m[None, :]
    return jnp.cos(ang), jnp.sin(ang), jnp.cos(angm), jnp.sin(angm)


def kernel(x_prompt, x_sample, cache_mla_ckv, cache_mla_kpe, state_ret, meta_tokens,
           ffn1_w_in, ffn1_w_out, ln1_g, ln1_b, w_mix_in, ret_gn_g, mla_q_norm_g, mla_w_uq,
           mla_kv_norm_g, mla_w_ukv, w_mix_out, ln2_g, ln2_b, ffn2_w_in, ffn2_w_out, ln3_g, ln3_b):
    B, S, D = x_prompt.shape
    DB, T, _ = x_sample.shape
    P = cache_mla_ckv.shape[1]
    assert S % ATT_TQ == 0 and S % FRONT_TM == 0 and (B * S) % TAIL_TM == 0 and ATT_TQ % RET_C == 0
    row = lambda a: a.reshape(1, -1).astype(f32)

    w_mix = w_mix_in.astype(bf16)
    w_uq = mla_w_uq.reshape(MLA_Q_LORA, MLA_HEADS, MLA_NOPE + MLA_ROPE)
    w_uq = jnp.pad(w_uq, ((0, 0), (0, 0), (0, MLA_QK_PAD - MLA_NOPE - MLA_ROPE)))
    w_uq = w_uq.reshape(MLA_Q_LORA, MLA_QW).astype(bf16)
    w_ukv = mla_w_ukv.reshape(MLA_KV_LORA, MLA_HEADS, MLA_NOPE + MLA_V)
    w_uk = w_ukv[:, :, :MLA_NOPE].reshape(MLA_KV_LORA, -1).astype(bf16)
    w_uv = w_ukv[:, :, MLA_NOPE:].reshape(MLA_KV_LORA, -1).astype(bf16)
    shared_w = [ffn1_w_in.astype(bf16), ffn1_w_out.astype(bf16), row(ln1_g), row(ln1_b), w_mix,
                row(ret_gn_g), row(mla_q_norm_g)]
    front_w = shared_w + [w_uq, row(mla_kv_norm_g), w_uk, w_uv]
    front_t_w = shared_w + [w_uq.T, row(mla_kv_norm_g), w_uk, w_uv.T]

    tabs_p = _rope_tables(N_META + jnp.arange(FRONT_TM))
    steps_p = _rope_tables(FRONT_TM * jnp.arange(S // FRONT_TM))
    tabs_t_p = (jnp.concatenate([tabs_p[2].T, tabs_p[3].T], axis=0),
                jnp.concatenate([steps_p[2], steps_p[3]], axis=1)[:, :, None])
    (h_p, qr_p, kr_p, vr_p, gr_p, qt_p, km_p, vt_p, ckv_p, kpe_p) = _front_call(
        x_prompt.reshape(B * S, D), tabs_p, jnp.concatenate(steps_p, axis=1)[:, None, :], tabs_t_p, front_t_w,
        FRONT_TM, FRONT_SUB)
    n_s = DB * T
    x_small = jnp.concatenate([x_sample.reshape(n_s, D), meta_tokens.astype(x_sample.dtype)], axis=0)
    pos_small = jnp.concatenate([jnp.tile(N_META + P + jnp.arange(T), DB), jnp.arange(N_META)])
    (h_s, qr_s, kr_s, vr_s, gr_s, qm_s, km_s, vm_s, ckv_s, kpe_s) = _front_call(
        x_small, _rope_tables(pos_small), jnp.concatenate(_rope_tables(jnp.zeros((1,), jnp.int32)), axis=1)[:, None, :],
        None, front_w, n_s + N_META, n_s + N_META)

    mix_p, p_state, w2_in = _mixer_call(qt_p, km_p, vt_p, km_s, vm_s, qr_p, kr_p, vr_p, gr_p, kr_s, vr_s,
                                        ffn2_w_in, B, S)
    mix_s, s_state, w2_out, w_mo = _smixer_call(qm_s, km_s, vm_s, cache_mla_ckv, cache_mla_kpe,
                                                jnp.concatenate([w_uk, w_uv], axis=1), qr_s, kr_s, vr_s, gr_s,
                                                state_ret, ffn2_w_out, w_mix_out, T)

    tail_w = [w_mo, row(ln2_g), row(ln2_b), w2_in, w2_out, row(ln3_g), row(ln3_b)]
    y_p = _tail_call(mix_p, h_p, tail_w, TAIL_TM, FRONT_SUB)
    y_s = _tail_call(mix_s, h_s, tail_w, n_s, n_s // 2)

    meta_ckv = jnp.broadcast_to(ckv_s[n_s:][None], (B, N_META, MLA_KV_LORA))
    meta_kpe = jnp.broadcast_to(kpe_s[n_s:][None], (B, N_META, MLA_ROPE))
    p_ckv = jnp.concatenate([meta_ckv, ckv_p.reshape(B, S, MLA_KV_LORA)], axis=1)
    p_kpe = jnp.concatenate([meta_kpe, kpe_p.reshape(B, S, MLA_ROPE)], axis=1)
    return (y_p.reshape(B, S, D), y_s.reshape(DB, T, D), p_ckv, p_kpe, p_state.astype(x_prompt.dtype),
            ckv_s[:n_s].reshape(DB, T, MLA_KV_LORA), kpe_s[:n_s].reshape(DB, T, MLA_ROPE),
            s_state.astype(state_ret.dtype))
```

```python
import functools
import math

import jax
import jax.numpy as jnp
from jax import lax
from jax.experimental import pallas as pl
from jax.experimental.pallas import tpu as pltpu

D_MODEL = 1024
DEPTH = 1
CHUNK = 64
N_META = 16
RET_HEADS = 4
RET_DK = 128
RET_DV = 128
MLA_HEADS = 4
MLA_NOPE = 128
MLA_ROPE = 64
MLA_V = 128
MLA_Q_LORA = 256
MLA_KV_LORA = 128
D_FF = 2816
ROPE_BASE = 10000.0
LN_EPS = 1e-5
RMS_EPS = 1e-6
DN_ALPHA = (2 * DEPTH) ** 0.25

OFF_KR = RET_HEADS * RET_DK
OFF_VR = 2 * RET_HEADS * RET_DK
OFF_GR = OFF_VR + RET_HEADS * RET_DV
OFF_CQ = OFF_GR + RET_HEADS * RET_DV
OFF_CKV = OFF_CQ + MLA_Q_LORA
OFF_KPE = OFF_CKV + MLA_KV_LORA
D_IN = OFF_KPE + MLA_ROPE

LANES = 128
MLA_QK_PAD = 2 * LANES
RET_W = RET_HEADS * RET_DK
MLA_QW = MLA_HEADS * MLA_QK_PAD
MLA_VW = MLA_HEADS * MLA_V
BF16_SUBLANES = 16
MLA_VT_ROWS = MLA_V + BF16_SUBLANES

FRONT_SUB = 256
FRONT_TM = 2 * FRONT_SUB
TAIL_TM = 4 * FRONT_SUB
ATT_TQ = FRONT_TM
ATT_TK = FRONT_TM
ATT_KB = 512
ATT_AHEAD = 4
RET_C = 256
VMEM_LIMIT = 56 * 1024 * 1024

MLA_SCALE = (MLA_NOPE + MLA_ROPE) ** -0.5
MLA_SCALE_LOG2E = MLA_SCALE * math.log2(math.e)
RET_LOG_GAMMA = tuple(math.log(1.0 - 2.0 ** (-5.0 - h)) for h in range(RET_HEADS))

f32 = jnp.float32
bf16 = jnp.bfloat16


def _dot(a, b):
    return jnp.dot(a, b, preferred_element_type=f32)


def _dot_nt(a, b):
    return lax.dot_general(a, b, (((1,), (1,)), ((), ())), preferred_element_type=f32)


def _dot_tn(a, b):
    return lax.dot_general(a, b, (((0,), (0,)), ((), ())), preferred_element_type=f32)


def _layer_norm(x, g, b):
    mu = jnp.mean(x, axis=-1, keepdims=True)
    xc = x - mu
    var = jnp.mean(xc * xc, axis=-1, keepdims=True)
    return xc * lax.rsqrt(var + LN_EPS) * g + b


def _rms_norm(x, g):
    return x * lax.rsqrt(jnp.mean(x * x, axis=-1, keepdims=True) + RMS_EPS) * g


def _silu(x):
    return x / (1.0 + jnp.exp(-x))


def _swiglu(x, w_in_ref, w_out_ref):
    xb = x.astype(bf16)
    hg = _dot(xb, w_in_ref[:, :D_FF])
    hu = _dot(xb, w_in_ref[:, D_FF:])
    return _dot((_silu(hg) * hu).astype(bf16), w_out_ref[...])


def _ffn_norm(x, y, g_ref, b_ref):
    return _layer_norm(DN_ALPHA * x + 0.5 * y, g_ref[...], b_ref[...])


def _rotate(cb, sb, ca, sa):
    return cb * ca - sb * sa, sb * ca + cb * sa


def _front_kernel(transposed, sub, x_ref, cr_ref, sr_ref, cm_ref, sm_ref, step_ref, *refs):
    if transposed:
        tab_t_ref, step_t_ref, refs = refs[0], refs[1], refs[2:]
    (w_in_ref, w_out_ref, ln_g_ref, ln_b_ref, w_mix_ref, gn_ref, qn_g_ref, w_uq_ref, kvn_g_ref,
     w_uk_ref, w_uv_ref,
     h_ref, qr_ref, kr_ref, vr_ref, gate_ref, qm_ref, km_ref, vm_ref, ckv_ref, kpe_ref) = refs
    n_sub = x_ref.shape[0] // sub
    tiles = [slice(j * sub, (j + 1) * sub) for j in range(n_sub)]

    xs = [x_ref[r, :] for r in tiles]
    ys = [_swiglu(x, w_in_ref, w_out_ref) for x in xs]
    ps = []
    for r, x, y in zip(tiles, xs, ys):
        h = _ffn_norm(x, y, ln_g_ref, ln_b_ref)
        h_ref[r, :] = h
        hb = h.astype(bf16)
        p_kpe = _dot(hb, w_mix_ref[:, OFF_KPE:])
        ps.append((_dot(hb, w_mix_ref[:, :OFF_KPE]),
                   jnp.concatenate([p_kpe, jnp.zeros_like(p_kpe)], axis=1)))

    for j, (r, (p, p_kpe)) in enumerate(zip(tiles, ps)):
        half_r, half_m = RET_DK // 2, MLA_ROPE // 2
        c, s = _rotate(cr_ref[r, :], sr_ref[r, :], step_ref[0, :, :half_r], step_ref[0, :, half_r:2 * half_r])
        cm, sm = _rotate(cm_ref[r, :], sm_ref[r, :], step_ref[0, :, 2 * half_r:2 * half_r + half_m],
                         step_ref[0, :, 2 * half_r + half_m:])
        z_q, z_h = jnp.zeros_like(cm), jnp.zeros_like(c)
        c_r = jnp.concatenate([c, c], axis=1)
        s_r = jnp.concatenate([-s, s], axis=1)
        c_m = jnp.concatenate([cm, cm, z_h], axis=1)
        s_ma = jnp.concatenate([-sm, z_q, z_h], axis=1)
        s_mb = jnp.concatenate([z_q, sm, z_h], axis=1)

        def rope_ret(xh):
            return xh * c_r + pltpu.roll(xh, 64, 1) * s_r

        def rope_mla(xh):
            return xh * c_m + pltpu.roll(xh, 96, 1) * s_ma + pltpu.roll(xh, 32, 1) * s_mb

        cq = _rms_norm(p[:, OFF_CQ:OFF_CKV], qn_g_ref[...]).astype(bf16)
        ckv = _rms_norm(p[:, OFF_CKV:OFF_KPE], kvn_g_ref[...])
        ckv_ref[r, :] = ckv
        ckv_b = ckv.astype(bf16)
        k_nope = _dot(ckv_b, w_uk_ref[...])
        if transposed:
            q_t = _dot_nt(w_uq_ref[...], cq) * MLA_SCALE_LOG2E
            v_t = _dot_nt(w_uv_ref[...], ckv_b)
        else:
            q = _dot(cq, w_uq_ref[...])
            v = _dot(ckv_b, w_uv_ref[...])

        for hd in range(RET_HEADS):
            lo = hd * RET_DK
            qr_ref[r, lo:lo + RET_DK] = rope_ret(p[:, lo:lo + RET_DK]).astype(bf16)
            kr_ref[r, lo:lo + RET_DK] = (rope_ret(p[:, OFF_KR + lo:OFF_KR + lo + RET_DK])
                                         * RET_DK ** -0.5).astype(bf16)
        vr_ref[r, :] = p[:, OFF_VR:OFF_GR].astype(bf16)
        gate_ref[r, :] = gn_ref[...] * _silu(p[:, OFF_GR:OFF_CQ])
        kpe = rope_mla(p_kpe)
        kpe_ref[r, :] = kpe[:, :MLA_ROPE]
        kpe_b = kpe.astype(bf16)
        for hd in range(MLA_HEADS):
            lo = hd * MLA_QK_PAD
            km_ref[r, lo:lo + LANES] = k_nope[:, hd * MLA_NOPE:(hd + 1) * MLA_NOPE].astype(bf16)
            km_ref[r, lo + LANES:lo + 2 * LANES] = kpe_b

        if transposed:
            half = MLA_ROPE // 2
            c_t, s_t = _rotate(tab_t_ref[:half, r], tab_t_ref[half:, r], step_t_ref[0, :half, :], step_t_ref[0, half:, :])
            for hd in range(MLA_HEADS):
                lo = hd * MLA_QK_PAD
                r0 = lo + MLA_NOPE
                x1 = q_t[r0:r0 + half, :]
                x2 = q_t[r0 + half:r0 + 2 * half, :]
                qm_ref[0, lo:r0, r] = q_t[lo:r0, :].astype(bf16)
                qm_ref[0, r0:r0 + half, r] = (x1 * c_t - x2 * s_t).astype(bf16)
                qm_ref[0, r0 + half:r0 + 2 * half, r] = (x1 * s_t + x2 * c_t).astype(bf16)
                qm_ref[0, r0 + 2 * half:lo + MLA_QK_PAD, r] = q_t[r0 + 2 * half:lo + MLA_QK_PAD, :].astype(bf16)
            ones = jnp.ones((MLA_VT_ROWS - MLA_V, v_t.shape[1]), bf16)
            for hd in range(MLA_HEADS):
                vm_ref[0, hd * MLA_VT_ROWS:hd * MLA_VT_ROWS + MLA_V, r] = v_t[hd * MLA_V:(hd + 1) * MLA_V, :].astype(bf16)
                vm_ref[0, hd * MLA_VT_ROWS + MLA_V:(hd + 1) * MLA_VT_ROWS, r] = ones
        else:
            for hd in range(MLA_HEADS):
                lo = hd * MLA_QK_PAD
                qm_ref[r, lo:lo + LANES] = q[:, lo:lo + LANES].astype(bf16)
                qm_ref[r, lo + LANES:lo + 2 * LANES] = rope_mla(q[:, lo + LANES:lo + 2 * LANES]).astype(bf16)
            vm_ref[r, :] = v.astype(bf16)


def _const_spec(shape):
    nd = len(shape)
    return pl.BlockSpec(shape, lambda *_: (0,) * nd, pipeline_mode=pl.Buffered(1))


def _front_call(x, tabs, steps, tabs_t, wts, tm, sub):
    n = x.shape[0]
    transposed = tabs_t is not None
    period = steps.shape[0]
    row = lambda w: pl.BlockSpec((tm, w), lambda i: (i, 0))
    col = lambda w: pl.BlockSpec((1, w, tm), lambda i: (i, 0, 0))
    in_specs = [row(D_MODEL)] + [pl.BlockSpec((tm, t.shape[1]), lambda i: (0, 0)) for t in tabs]
    in_specs.append(pl.BlockSpec((1,) + steps.shape[1:], lambda i: (i % period, 0, 0)))
    args = [x, *tabs, steps]
    if transposed:
        in_specs.append(pl.BlockSpec((MLA_ROPE, tm), lambda i: (0, 0)))
        in_specs.append(pl.BlockSpec((1,) + tabs_t[1].shape[1:], lambda i: (i % period, 0, 0)))
        args += list(tabs_t)
    in_specs += [_const_spec(w.shape) for w in wts]
    rows = lambda w, dt: (row(w), jax.ShapeDtypeStruct((n, w), dt))
    cols = lambda w, dt: (col(w), jax.ShapeDtypeStruct((n // tm, w, tm), dt))
    qv = cols if transposed else rows
    outs = [rows(D_MODEL, f32), rows(RET_W, bf16), rows(RET_W, bf16), rows(RET_W, bf16), rows(RET_W, f32),
            qv(MLA_QW, bf16), rows(MLA_QW, bf16),
            cols(MLA_HEADS * MLA_VT_ROWS, bf16) if transposed else rows(MLA_VW, bf16),
            rows(MLA_KV_LORA, f32), rows(MLA_ROPE, f32)]
    return pl.pallas_call(
        functools.partial(_front_kernel, transposed, sub),
        grid=(n // tm,),
        in_specs=in_specs,
        out_specs=[o[0] for o in outs],
        out_shape=[o[1] for o in outs],
        compiler_params=pltpu.CompilerParams(dimension_semantics=("arbitrary",),
                                             vmem_limit_bytes=VMEM_LIMIT),
        name="front_t" if transposed else "front",
    )(*args, *wts)


def _decay_mask(c, lg):
    r = lax.broadcasted_iota(jnp.int32, (c, c), 0)
    k = lax.broadcasted_iota(jnp.int32, (c, c), 1)
    rel = (r - k).astype(f32)
    return jnp.where(rel >= 0.0, jnp.exp(lg * jnp.maximum(rel, 0.0)), 0.0)


def _row_pow(c, lg, offset, sign):
    i = lax.broadcasted_iota(jnp.int32, (c, LANES), 0).astype(f32)
    return jnp.exp(lg * (offset + sign * i))


def _retention_tables(c, hd):
    lg = RET_LOG_GAMMA[hd]
    return _decay_mask(c, lg), _row_pow(c, lg, 1.0, 1.0), _row_pow(c, lg, c - 1.0, -1.0)


def _retention_chunk(qr_ref, kr_ref, vr_ref, gate_ref, mix_ref, rows, states, tables):
    c = rows.stop - rows.start
    heads = range(RET_HEADS)
    cols = [slice(hd * RET_DK, (hd + 1) * RET_DK) for hd in heads]
    q = [qr_ref[rows, cols[hd]] for hd in heads]
    k = [kr_ref[rows, cols[hd]] for hd in heads]
    v = [vr_ref[rows, cols[hd]] for hd in heads]
    scores = [_dot_nt(q[hd], k[hd]) for hd in heads]
    cross = [_dot(q[hd], states[hd].astype(bf16)) for hd in heads]
    kv = [_dot_tn((k[hd].astype(f32) * tables[hd][2][...]).astype(bf16), v[hd]) for hd in heads]
    new_states = [math.exp(RET_LOG_GAMMA[hd] * c) * states[hd] + kv[hd] for hd in heads]

    def finish():
        inner = [_dot((scores[hd] * tables[hd][0][...]).astype(bf16), v[hd]) for hd in heads]
        for hd in heads:
            o = inner[hd] + cross[hd] * tables[hd][1][...]
            mu = jnp.mean(o, axis=-1, keepdims=True)
            oc = o - mu
            var = jnp.mean(oc * oc, axis=-1, keepdims=True)
            y = oc * lax.rsqrt(var + LN_EPS) * gate_ref[rows, cols[hd]]
            mix_ref[rows, cols[hd]] = y.astype(bf16)

    return new_states, finish


def _att_update(m_ref, acc_ref, hd, lane0, s, v_t, ok):
    if ok is not None:
        s = jnp.where(ok, s, -jnp.inf)
    m = m_ref[hd, :, lane0:]
    m_new = jnp.maximum(m, jnp.max(s, axis=0, keepdims=True))
    p = jnp.exp2(s - m_new)
    m_ref[hd, :, lane0:] = m_new
    acc_ref[hd, :, lane0:] = jnp.exp2(m - m_new) * acc_ref[hd, :, lane0:] + _dot(v_t, p.astype(bf16))


def _mixer_kernel(qt_ref, km_ref, vt_ref, kmeta_ref, vmeta_ref, qr_ref, kr_ref, vr_ref, gate_ref,
                  krmeta_ref, vrmeta_ref, w_f32_ref, mix_ref, state_ref, w_bf16_ref,
                  m_ref, acc_ref, s_ref, dmask_ref, qdec_ref, kdec_ref):
    qi = pl.program_id(1)

    w_bf16_ref[...] = w_f32_ref[...].astype(bf16)

    @pl.when((pl.program_id(0) == 0) & (qi == 0))
    def _():
        for hd in range(RET_HEADS):
            dmask_ref[hd], qdec_ref[hd], kdec_ref[hd] = _retention_tables(RET_C, hd)

    @pl.when(qi == 0)
    def _():
        for hd in range(RET_HEADS):
            lo = hd * RET_DK
            lg = RET_LOG_GAMMA[hd]
            kd = (krmeta_ref[:, lo:lo + RET_DK].astype(f32)
                  * _row_pow(N_META, lg, N_META - 1.0, -1.0)).astype(bf16)
            state_ref[0, hd] = _dot_tn(kd, vrmeta_ref[:, lo:lo + RET_DV])

    heads = range(MLA_HEADS)
    units = [(kb, hd) for kb in range(ATT_TK // ATT_KB) for hd in heads]

    def scores(start, kb, hd, lane0=0):
        k = km_ref[0, pl.ds(start + kb * ATT_KB, ATT_KB), hd * MLA_QK_PAD:(hd + 1) * MLA_QK_PAD]
        return _dot(k, qt_ref[0, hd * MLA_QK_PAD:(hd + 1) * MLA_QK_PAD, lane0:])

    def v_block(kt, kb, hd):
        return vt_ref[kt, hd * MLA_VT_ROWS:(hd + 1) * MLA_VT_ROWS, kb * ATT_KB:(kb + 1) * ATT_KB]

    first_scores = [functools.partial(scores, 0, *units[d]) for d in range(ATT_AHEAD)]

    def trace_first_scores(n):
        for _ in range(min(n, len(first_scores))):
            d = ATT_AHEAD - len(first_scores)
            s_ref[d] = first_scores.pop(0)()

    tables = [(dmask_ref.at[hd], qdec_ref.at[hd], kdec_ref.at[hd]) for hd in range(RET_HEADS)]
    states = [state_ref[0, hd] for hd in range(RET_HEADS)]
    finishes = []
    for c in range(ATT_TQ // RET_C):
        states, finish_chunk = _retention_chunk(qr_ref, kr_ref, vr_ref, gate_ref, mix_ref,
                                                slice(c * RET_C, (c + 1) * RET_C), states, tables)
        finishes.append(finish_chunk)
    for hd in range(RET_HEADS):
        state_ref[0, hd] = states[hd]
    per_stage = -(-ATT_AHEAD // len(finishes))
    for finish_chunk in finishes:
        trace_first_scores(per_stage)
        finish_chunk()
    trace_first_scores(ATT_AHEAD)
    s_meta = [_dot(kmeta_ref[:, hd * MLA_QK_PAD:(hd + 1) * MLA_QK_PAD],
                   qt_ref[0, hd * MLA_QK_PAD:(hd + 1) * MLA_QK_PAD, :]) for hd in heads]

    p_meta = []
    for hd in heads:
        m = jnp.max(s_meta[hd], axis=0, keepdims=True)
        m_ref[hd] = m
        p_meta.append(jnp.exp2(s_meta[hd] - m).astype(bf16))
    ones = jnp.ones((N_META, MLA_VT_ROWS - MLA_V), bf16)
    for hd in heads:
        v_ext = jnp.concatenate([vmeta_ref[:, hd * MLA_V:(hd + 1) * MLA_V], ones], axis=1)
        acc_ref[hd] = _dot_tn(v_ext, p_meta[hd])

    def full_step(kt, _):
        start = pl.multiple_of(kt * ATT_TK, ATT_TK)
        pending = [s_ref[d] for d in range(ATT_AHEAD)]
        for u, (kb, hd) in enumerate(units):
            ahead = u + ATT_AHEAD
            if ahead < len(units):
                pending.append(scores(start, *units[ahead]))
            else:
                pending.append(scores(pl.multiple_of(start + ATT_TK, ATT_TK), *units[ahead - len(units)]))
            _att_update(m_ref, acc_ref, hd, 0, pending.pop(0), v_block(kt, kb, hd), None)
        for d in range(ATT_AHEAD):
            s_ref[d] = pending[d]
        return 0

    lax.fori_loop(0, qi, full_step, 0)

    start = pl.multiple_of(qi * ATT_TQ, ATT_TQ)
    pending = [s_ref[d] for d in range(ATT_AHEAD)]
    for u, (kb, hd) in enumerate(units):
        lane0 = kb * ATT_KB
        if u + ATT_AHEAD < len(units):
            nxt = units[u + ATT_AHEAD]
            pending.append(scores(start, *nxt, lane0=nxt[0] * ATT_KB))
        s_cur = pending.pop(0)
        k_chunk = lax.broadcasted_iota(jnp.int32, s_cur.shape, 0) // CHUNK
        q_chunk = lax.broadcasted_iota(jnp.int32, s_cur.shape, 1) // CHUNK
        _att_update(m_ref, acc_ref, hd, lane0, s_cur, v_block(qi, kb, hd), k_chunk <= q_chunk)
    for hd in heads:
        out = (acc_ref[hd, :MLA_V, :] / acc_ref[hd, MLA_V:MLA_V + 1, :]).T
        mix_ref[:, RET_W + hd * MLA_V:RET_W + (hd + 1) * MLA_V] = out.astype(bf16)


def _mixer_call(qt, km, vt, kmeta, vmeta, qr, kr, vr, gate, krmeta, vrmeta, w_f32, batch, seq):
    nq = seq // ATT_TQ
    tile = lambda w: pl.BlockSpec((ATT_TQ, w), lambda b, i: (b * nq + i, 0))
    const = lambda a: pl.BlockSpec((N_META, a.shape[1]), lambda b, i: (a.shape[0] // N_META - 1, 0))
    w_rows, rem = divmod(w_f32.shape[0], batch * nq)
    assert rem == 0 and w_rows % BF16_SUBLANES == 0
    w_slice = pl.BlockSpec((w_rows, w_f32.shape[1]), lambda b, i: (b * nq + i, 0))
    return pl.pallas_call(
        _mixer_kernel,
        grid=(batch, nq),
        in_specs=[pl.BlockSpec((1, MLA_QW, ATT_TQ), lambda b, i: (b * nq + i, 0, 0)),
                  pl.BlockSpec((1, seq, MLA_QW), lambda b, i: (b, 0, 0)),
                  pl.BlockSpec((seq // ATT_TK, MLA_HEADS * MLA_VT_ROWS, ATT_TK), lambda b, i: (b, 0, 0)),
                  const(kmeta), const(vmeta),
                  tile(RET_W), tile(RET_W), tile(RET_W), tile(RET_W),
                  const(krmeta), const(vrmeta), w_slice],
        out_specs=[tile(D_MODEL),
                   pl.BlockSpec((1, RET_HEADS, RET_DK, RET_DV), lambda b, i: (b, 0, 0, 0)), w_slice],
        out_shape=[jax.ShapeDtypeStruct((batch * seq, D_MODEL), bf16),
                   jax.ShapeDtypeStruct((batch, RET_HEADS, RET_DK, RET_DV), f32),
                   jax.ShapeDtypeStruct(w_f32.shape, bf16)],
        scratch_shapes=[pltpu.VMEM((MLA_HEADS, 1, ATT_TQ), f32),
                        pltpu.VMEM((MLA_HEADS, MLA_VT_ROWS, ATT_TQ), f32),
                        pltpu.VMEM((ATT_AHEAD, ATT_KB, ATT_TQ), f32),
                        pltpu.VMEM((RET_HEADS, RET_C, RET_C), f32),
                        pltpu.VMEM((RET_HEADS, RET_C, LANES), f32),
                        pltpu.VMEM((RET_HEADS, RET_C, LANES), f32)],
        compiler_params=pltpu.CompilerParams(dimension_semantics=("arbitrary", "arbitrary"),
                                             vmem_limit_bytes=VMEM_LIMIT),
        name="mixer",
    )(qt, km.reshape(batch, seq, MLA_QW), vt, kmeta, vmeta, qr, kr, vr, gate, krmeta, vrmeta, w_f32)


def _smixer_kernel(qm_ref, knew_ref, vnew_ref, kmeta_ref, vmeta_ref, cckv_ref, ckpe_ref, w_ukv_ref,
                   qr_ref, kr_ref, vr_ref, gate_ref, s0_ref, wa_f32_ref, wb_f32_ref,
                   mix_ref, state_ref, wa_bf16_ref, wb_bf16_ref):
    wa_bf16_ref[...] = wa_f32_ref[...].astype(bf16)
    wb_bf16_ref[...] = wb_f32_ref[...].astype(bf16)

    t = qr_ref.shape[0]
    states, finish_retention = _retention_chunk(qr_ref, kr_ref, vr_ref, gate_ref, mix_ref, slice(0, t),
                                                [s0_ref[0, hd] for hd in range(RET_HEADS)],
                                                [_retention_tables(t, hd) for hd in range(RET_HEADS)])
    for hd in range(RET_HEADS):
        state_ref[0, hd] = states[hd]
    finish_retention()

    heads = range(MLA_HEADS)
    kcols = [slice(hd * MLA_QK_PAD, (hd + 1) * MLA_QK_PAD) for hd in heads]
    vcols = [slice(hd * MLA_V, (hd + 1) * MLA_V) for hd in heads]
    kv_c = _dot(cckv_ref[0].astype(bf16), w_ukv_ref[...])
    kpe_c = ckpe_ref[0].astype(bf16)
    q = [qm_ref[:, kcols[hd]] for hd in heads]
    s_m = [_dot_nt(q[hd], kmeta_ref[:, kcols[hd]]) for hd in heads]
    s_n = [_dot_nt(q[hd], knew_ref[:, kcols[hd]]) for hd in heads]
    s_r = [_dot_nt(q[hd][:, LANES:LANES + MLA_ROPE], kpe_c) for hd in heads]
    s_c = [_dot_nt(q[hd][:, :LANES], kv_c[:, hd * MLA_NOPE:(hd + 1) * MLA_NOPE].astype(bf16)) + s_r[hd]
           for hd in heads]
    probs, denom = [], []
    for hd in heads:
        parts = [s_m[hd] * MLA_SCALE, s_c[hd] * MLA_SCALE, s_n[hd] * MLA_SCALE]
        m = functools.reduce(jnp.maximum, [jnp.max(s, axis=-1, keepdims=True) for s in parts])
        ps = [jnp.exp(s - m) for s in parts]
        denom.append(functools.reduce(jnp.add, [jnp.sum(p, axis=-1, keepdims=True) for p in ps]))
        probs.append([p.astype(bf16) for p in ps])
    for hd in heads:
        v_c = kv_c[:, MLA_HEADS * MLA_NOPE + hd * MLA_V:MLA_HEADS * MLA_NOPE + (hd + 1) * MLA_V].astype(bf16)
        acc = (_dot(probs[hd][0], vmeta_ref[:, vcols[hd]]) + _dot(probs[hd][1], v_c)
               + _dot(probs[hd][2], vnew_ref[:, vcols[hd]]))
        mix_ref[:, RET_W + hd * MLA_V:RET_W + (hd + 1) * MLA_V] = (acc / denom[hd]).astype(bf16)


def _smixer_call(qm, km, vm, cckv, ckpe, w_ukv, qr, kr, vr, gate, s0, wa_f32, wb_f32, t):
    db, past = cckv.shape[0], cckv.shape[1]
    tile = lambda w: pl.BlockSpec((t, w), lambda b: (b, 0))
    meta = lambda w: pl.BlockSpec((N_META, w), lambda b: (db * t // N_META, 0))
    const = lambda a: pl.BlockSpec(a.shape, lambda b: (0,) * a.ndim)
    per = lambda *tail: pl.BlockSpec((1,) + tail, lambda b: (b,) + (0,) * len(tail))

    def w_slice(w):
        rows, rem = divmod(w.shape[0], db)
        assert rem == 0 and rows % BF16_SUBLANES == 0
        return pl.BlockSpec((rows, w.shape[1]), lambda b: (b, 0))

    return pl.pallas_call(
        _smixer_kernel,
        grid=(db,),
        in_specs=[tile(MLA_QW), tile(MLA_QW), tile(MLA_VW), meta(MLA_QW), meta(MLA_VW),
                  per(past, MLA_KV_LORA), per(past, MLA_ROPE), const(w_ukv),
                  tile(RET_W), tile(RET_W), tile(RET_W), tile(RET_W),
                  per(RET_HEADS, RET_DK, RET_DV), w_slice(wa_f32), w_slice(wb_f32)],
        out_specs=[tile(D_MODEL), per(RET_HEADS, RET_DK, RET_DV), w_slice(wa_f32), w_slice(wb_f32)],
        out_shape=[jax.ShapeDtypeStruct((db * t, D_MODEL), bf16),
                   jax.ShapeDtypeStruct((db, RET_HEADS, RET_DK, RET_DV), f32),
                   jax.ShapeDtypeStruct(wa_f32.shape, bf16), jax.ShapeDtypeStruct(wb_f32.shape, bf16)],
        compiler_params=pltpu.CompilerParams(dimension_semantics=("arbitrary",),
                                             vmem_limit_bytes=VMEM_LIMIT),
        name="smixer",
    )(qm, km, vm, km, vm, cckv, ckpe, w_ukv, qr, kr, vr, gate, s0, wa_f32, wb_f32)


def _tail_kernel(sub, mix_ref, h_ref, w_mo_ref, ln2_g_ref, ln2_b_ref, w_in_ref, w_out_ref, ln3_g_ref, ln3_b_ref,
                 y_ref):
    tiles = [slice(j * sub, (j + 1) * sub) for j in range(mix_ref.shape[0] // sub)]
    mixes = [_dot(mix_ref[r, :], w_mo_ref[...]) for r in tiles]
    h2s, ys = [], []
    for r, mix in zip(tiles, mixes):
        h2 = _layer_norm(DN_ALPHA * h_ref[r, :] + mix, ln2_g_ref[...], ln2_b_ref[...])
        h2s.append(h2)
        ys.append(_swiglu(h2, w_in_ref, w_out_ref))
    for r, h2, y in zip(tiles, h2s, ys):
        y_ref[r, :] = _ffn_norm(h2, y, ln3_g_ref, ln3_b_ref)


def _tail_call(mix, h, wts, tm, sub):
    n = mix.shape[0]
    row = lambda w: pl.BlockSpec((tm, w), lambda i: (i, 0))
    return pl.pallas_call(
        functools.partial(_tail_kernel, sub),
        grid=(n // tm,),
        in_specs=[row(D_MODEL), row(D_MODEL)] + [_const_spec(w.shape) for w in wts],
        out_specs=row(D_MODEL),
        out_shape=jax.ShapeDtypeStruct((n, D_MODEL), f32),
        compiler_params=pltpu.CompilerParams(dimension_semantics=("arbitrary",),
                                             vmem_limit_bytes=VMEM_LIMIT),
        name="tail",
    )(mix, h, *wts)


def _rope_tables(pos):
    posf = pos.astype(f32)[:, None]
    inv_r = ROPE_BASE ** (-jnp.arange(0, RET_DK, 2, dtype=f32) / RET_DK)
    ang = posf * inv_r[None, :]
    inv_m = ROPE_BASE ** (-jnp.arange(0, MLA_ROPE, 2, dtype=f32) / MLA_ROPE)
    angm = posf * inv_m[None, :]
    return jnp.cos(ang), jnp.sin(ang), jnp.cos(angm), jnp.sin(angm)


def kernel(x_prompt, x_sample, cache_mla_ckv, cache_mla_kpe, state_ret, meta_tokens,
           ffn1_w_in, ffn1_w_out, ln1_g, ln1_b, w_mix_in, ret_gn_g, mla_q_norm_g, mla_w_uq,
           mla_kv_norm_g, mla_w_ukv, w_mix_out, ln2_g, ln2_b, ffn2_w_in, ffn2_w_out, ln3_g, ln3_b):
    B, S, D = x_prompt.shape
    DB, T, _ = x_sample.shape
    P = cache_mla_ckv.shape[1]
    assert S % ATT_TQ == 0 and S % FRONT_TM == 0 and (B * S) % TAIL_TM == 0 and ATT_TQ % RET_C == 0
    row = lambda a: a.reshape(1, -1).astype(f32)

    w_mix = w_mix_in.astype(bf16)
    w_uq = mla_w_uq.reshape(MLA_Q_LORA, MLA_HEADS, MLA_NOPE + MLA_ROPE)
    w_uq = jnp.pad(w_uq, ((0, 0), (0, 0), (0, MLA_QK_PAD - MLA_NOPE - MLA_ROPE)))
    w_uq = w_uq.reshape(MLA_Q_LORA, MLA_QW).astype(bf16)
    w_ukv = mla_w_ukv.reshape(MLA_KV_LORA, MLA_HEADS, MLA_NOPE + MLA_V)
    w_uk = w_ukv[:, :, :MLA_NOPE].reshape(MLA_KV_LORA, -1).astype(bf16)
    w_uv = w_ukv[:, :, MLA_NOPE:].reshape(MLA_KV_LORA, -1).astype(bf16)
    shared_w = [ffn1_w_in.astype(bf16), ffn1_w_out.astype(bf16), row(ln1_g), row(ln1_b), w_mix,
                row(ret_gn_g), row(mla_q_norm_g)]
    front_w = shared_w + [w_uq, row(mla_kv_norm_g), w_uk, w_uv]
    front_t_w = shared_w + [w_uq.T, row(mla_kv_norm_g), w_uk, w_uv.T]

    tabs_p = _rope_tables(N_META + jnp.arange(FRONT_TM))
    steps_p = _rope_tables(FRONT_TM * jnp.arange(S // FRONT_TM))
    tabs_t_p = (jnp.concatenate([tabs_p[2].T, tabs_p[3].T], axis=0),
                jnp.concatenate([steps_p[2], steps_p[3]], axis=1)[:, :, None])
    (h_p, qr_p, kr_p, vr_p, gr_p, qt_p, km_p, vt_p, ckv_p, kpe_p) = _front_call(
        x_prompt.reshape(B * S, D), tabs_p, jnp.concatenate(steps_p, axis=1)[:, None, :], tabs_t_p, front_t_w,
        FRONT_TM, FRONT_SUB)
    n_s = DB * T
    x_small = jnp.concatenate([x_sample.reshape(n_s, D), meta_tokens.astype(x_sample.dtype)], axis=0)
    pos_small = jnp.concatenate([jnp.tile(N_META + P + jnp.arange(T), DB), jnp.arange(N_META)])
    (h_s, qr_s, kr_s, vr_s, gr_s, qm_s, km_s, vm_s, ckv_s, kpe_s) = _front_call(
        x_small, _rope_tables(pos_small), jnp.concatenate(_rope_tables(jnp.zeros((1,), jnp.int32)), axis=1)[:, None, :],
        None, front_w, n_s + N_META, n_s + N_META)

    mix_p, p_state, w2_in = _mixer_call(qt_p, km_p, vt_p, km_s, vm_s, qr_p, kr_p, vr_p, gr_p, kr_s, vr_s,
                                        ffn2_w_in, B, S)
    mix_s, s_state, w2_out, w_mo = _smixer_call(qm_s, km_s, vm_s, cache_mla_ckv, cache_mla_kpe,
                                                jnp.concatenate([w_uk, w_uv], axis=1), qr_s, kr_s, vr_s, gr_s,
                                                state_ret, ffn2_w_out, w_mix_out, T)

    tail_w = [w_mo, row(ln2_g), row(ln2_b), w2_in, w2_out, row(ln3_g), row(ln3_b)]
    y_p = _tail_call(mix_p, h_p, tail_w, TAIL_TM, FRONT_SUB)
    y_p, mix_s = lax.optimization_barrier((y_p, mix_s))
    y_s = _tail_call(mix_s, h_s, tail_w, n_s, n_s // 2)

    meta_ckv = jnp.broadcast_to(ckv_s[n_s:][None], (B, N_META, MLA_KV_LORA))
    meta_kpe = jnp.broadcast_to(kpe_s[n_s:][None], (B, N_META, MLA_ROPE))
    p_ckv = jnp.concatenate([meta_ckv, ckv_p.reshape(B, S, MLA_KV_LORA)], axis=1)
    p_kpe = jnp.concatenate([meta_kpe, kpe_p.reshape(B, S, MLA_ROPE)], axis=1)
    return (y_p.reshape(B, S, D), y_s.reshape(DB, T, D), p_ckv, p_kpe, p_state.astype(x_prompt.dtype),
            ckv_s[:n_s].reshape(DB, T, MLA_KV_LORA), kpe_s[:n_s].reshape(DB, T, MLA_ROPE),
            s_state.astype(state_ret.dtype))
```

```python
import functools
import math

import jax
import jax.numpy as jnp
from jax import lax
from jax.experimental import pallas as pl
from jax.experimental.pallas import tpu as pltpu

D_MODEL = 1024
DEPTH = 1
CHUNK = 64
N_META = 16
RET_HEADS = 4
RET_DK = 128
RET_DV = 128
MLA_HEADS = 4
MLA_NOPE = 128
MLA_ROPE = 64
MLA_V = 128
MLA_Q_LORA = 256
MLA_KV_LORA = 128
D_FF = 2816
ROPE_BASE = 10000.0
LN_EPS = 1e-5
RMS_EPS = 1e-6
DN_ALPHA = (2 * DEPTH) ** 0.25

OFF_KR = RET_HEADS * RET_DK
OFF_VR = 2 * RET_HEADS * RET_DK
OFF_GR = OFF_VR + RET_HEADS * RET_DV
OFF_CQ = OFF_GR + RET_HEADS * RET_DV
OFF_CKV = OFF_CQ + MLA_Q_LORA
OFF_KPE = OFF_CKV + MLA_KV_LORA
D_IN = OFF_KPE + MLA_ROPE

LANES = 128
MLA_QK_PAD = 2 * LANES
RET_W = RET_HEADS * RET_DK
MLA_QW = MLA_HEADS * MLA_QK_PAD
MLA_VW = MLA_HEADS * MLA_V
BF16_SUBLANES = 16
MLA_VT_ROWS = MLA_V + BF16_SUBLANES

FRONT_SUB = 256
FRONT_TM = 2 * FRONT_SUB
TAIL_TM = 4 * FRONT_SUB
FF_CHUNK = 256
ATT_TQ = FRONT_TM
ATT_TK = FRONT_TM
ATT_KB = 512
ATT_AHEAD = 4
SMIX_STREAMS = 1
RET_C = 256
VMEM_LIMIT = 56 * 1024 * 1024

MLA_SCALE = (MLA_NOPE + MLA_ROPE) ** -0.5
MLA_SCALE_LOG2E = MLA_SCALE * math.log2(math.e)
RET_LOG_GAMMA = tuple(math.log(1.0 - 2.0 ** (-5.0 - h)) for h in range(RET_HEADS))

f32 = jnp.float32
bf16 = jnp.bfloat16


def _dot(a, b):
    return jnp.dot(a, b, preferred_element_type=f32)


def _dot_nt(a, b):
    return lax.dot_general(a, b, (((1,), (1,)), ((), ())), preferred_element_type=f32)


def _dot_tn(a, b):
    return lax.dot_general(a, b, (((0,), (0,)), ((), ())), preferred_element_type=f32)


def _layer_norm(x, g, b):
    mu = jnp.mean(x, axis=-1, keepdims=True)
    xc = x - mu
    var = jnp.mean(xc * xc, axis=-1, keepdims=True)
    return xc * lax.rsqrt(var + LN_EPS) * g + b


def _rms_norm(x, g):
    return x * lax.rsqrt(jnp.mean(x * x, axis=-1, keepdims=True) + RMS_EPS) * g


def _silu(x):
    return x / (1.0 + jnp.exp(-x))


def _swiglu(x, w_gate, w_up, w_out):
    xb = x.astype(bf16)
    hg = _dot(xb, w_gate[...])
    hu = _dot(xb, w_up[...])
    return _dot((_silu(hg) * hu).astype(bf16), w_out[...])


def _ffn_norm(x, y, g_ref, b_ref):
    return _layer_norm(DN_ALPHA * x + 0.5 * y, g_ref[...], b_ref[...])


def _rotate(cb, sb, ca, sa):
    return cb * ca - sb * sa, sb * ca + cb * sa


def _front_kernel(sub, x_ref, cr_ref, sr_ref, cm_ref, sm_ref, step_ref, tab_t_ref, step_t_ref,
                  w_gate_ref, w_up_ref, w_out_ref, *refs):
    n_sub = x_ref.shape[0] // sub
    tiles = [slice(j * sub, (j + 1) * sub) for j in range(n_sub)]
    xs = [x_ref[r, :] for r in tiles]
    ys = [_swiglu(x, w_gate_ref, w_up_ref, w_out_ref) for x in xs]
    _front_rest((tab_t_ref, step_t_ref), tiles, xs, ys, (cr_ref, sr_ref, cm_ref, sm_ref, step_ref), refs)


def _front_rest(tabs_t, tiles, xs, ys, tabs, refs):
    cr_ref, sr_ref, cm_ref, sm_ref, step_ref = tabs
    transposed = tabs_t is not None
    if transposed:
        tab_t_ref, step_t_ref = tabs_t
    (ln_g_ref, ln_b_ref, w_mix_ref, gn_ref, qn_g_ref, w_uq_ref, kvn_g_ref, w_uk_ref, w_uv_ref,
     h_ref, qr_ref, kr_ref, vr_ref, gate_ref, qm_ref, km_ref, vm_ref, ckv_ref, kpe_ref) = refs
    ps = []
    for r, x, y in zip(tiles, xs, ys):
        h = _ffn_norm(x, y, ln_g_ref, ln_b_ref)
        h_ref[r, :] = h
        hb = h.astype(bf16)
        p_kpe = _dot(hb, w_mix_ref[:, OFF_KPE:])
        ps.append((_dot(hb, w_mix_ref[:, :OFF_KPE]),
                   jnp.concatenate([p_kpe, jnp.zeros_like(p_kpe)], axis=1)))

    for j, (r, (p, p_kpe)) in enumerate(zip(tiles, ps)):
        half_r, half_m = RET_DK // 2, MLA_ROPE // 2
        c, s = _rotate(cr_ref[r, :], sr_ref[r, :], step_ref[0, :, :half_r], step_ref[0, :, half_r:2 * half_r])
        cm, sm = _rotate(cm_ref[r, :], sm_ref[r, :], step_ref[0, :, 2 * half_r:2 * half_r + half_m],
                         step_ref[0, :, 2 * half_r + half_m:])
        z_q, z_h = jnp.zeros_like(cm), jnp.zeros_like(c)
        c_r = jnp.concatenate([c, c], axis=1)
        s_r = jnp.concatenate([-s, s], axis=1)
        c_m = jnp.concatenate([cm, cm, z_h], axis=1)
        s_ma = jnp.concatenate([-sm, z_q, z_h], axis=1)
        s_mb = jnp.concatenate([z_q, sm, z_h], axis=1)

        def rope_ret(xh):
            return xh * c_r + pltpu.roll(xh, 64, 1) * s_r

        def rope_mla(xh):
            return xh * c_m + pltpu.roll(xh, 96, 1) * s_ma + pltpu.roll(xh, 32, 1) * s_mb

        cq = _rms_norm(p[:, OFF_CQ:OFF_CKV], qn_g_ref[...]).astype(bf16)
        ckv = _rms_norm(p[:, OFF_CKV:OFF_KPE], kvn_g_ref[...])
        ckv_ref[r, :] = ckv
        ckv_b = ckv.astype(bf16)
        k_nope = _dot(ckv_b, w_uk_ref[...])
        if transposed:
            q_t = _dot_nt(w_uq_ref[...], cq) * MLA_SCALE_LOG2E
            v_t = _dot_nt(w_uv_ref[...], ckv_b)
        else:
            q = _dot(cq, w_uq_ref[...])
            v = _dot(ckv_b, w_uv_ref[...])

        for hd in range(RET_HEADS):
            lo = hd * RET_DK
            qr_ref[r, lo:lo + RET_DK] = rope_ret(p[:, lo:lo + RET_DK]).astype(bf16)
            kr_ref[r, lo:lo + RET_DK] = (rope_ret(p[:, OFF_KR + lo:OFF_KR + lo + RET_DK])
                                         * RET_DK ** -0.5).astype(bf16)
        vr_ref[r, :] = p[:, OFF_VR:OFF_GR].astype(bf16)
        gate_ref[r, :] = gn_ref[...] * _silu(p[:, OFF_GR:OFF_CQ])
        kpe = rope_mla(p_kpe)
        kpe_ref[r, :] = kpe[:, :MLA_ROPE]
        kpe_b = kpe.astype(bf16)
        for hd in range(MLA_HEADS):
            lo = hd * MLA_QK_PAD
            km_ref[r, lo:lo + LANES] = k_nope[:, hd * MLA_NOPE:(hd + 1) * MLA_NOPE].astype(bf16)
            km_ref[r, lo + LANES:lo + 2 * LANES] = kpe_b

        if transposed:
            half = MLA_ROPE // 2
            c_t, s_t = _rotate(tab_t_ref[:half, r], tab_t_ref[half:, r], step_t_ref[0, :half, :], step_t_ref[0, half:, :])
            for hd in range(MLA_HEADS):
                lo = hd * MLA_QK_PAD
                r0 = lo + MLA_NOPE
                x1 = q_t[r0:r0 + half, :]
                x2 = q_t[r0 + half:r0 + 2 * half, :]
                qm_ref[0, lo:r0, r] = q_t[lo:r0, :].astype(bf16)
                qm_ref[0, r0:r0 + half, r] = (x1 * c_t - x2 * s_t).astype(bf16)
                qm_ref[0, r0 + half:r0 + 2 * half, r] = (x1 * s_t + x2 * c_t).astype(bf16)
                qm_ref[0, r0 + 2 * half:lo + MLA_QK_PAD, r] = q_t[r0 + 2 * half:lo + MLA_QK_PAD, :].astype(bf16)
            ones = jnp.ones((MLA_VT_ROWS - MLA_V, v_t.shape[1]), bf16)
            for hd in range(MLA_HEADS):
                vm_ref[0, hd * MLA_VT_ROWS:hd * MLA_VT_ROWS + MLA_V, r] = v_t[hd * MLA_V:(hd + 1) * MLA_V, :].astype(bf16)
                vm_ref[0, hd * MLA_VT_ROWS + MLA_V:(hd + 1) * MLA_VT_ROWS, r] = ones
        else:
            for hd in range(MLA_HEADS):
                lo = hd * MLA_QK_PAD
                qm_ref[r, lo:lo + LANES] = q[:, lo:lo + LANES].astype(bf16)
                qm_ref[r, lo + LANES:lo + 2 * LANES] = rope_mla(q[:, lo + LANES:lo + 2 * LANES]).astype(bf16)
            vm_ref[r, :] = v.astype(bf16)


def _front_small_kernel(x_ref, cr_ref, sr_ref, cm_ref, sm_ref, step_ref, wg_f32_ref, wu_f32_ref, wo_f32_ref, *refs):
    *rest, wg_ref, wu_ref, wo_ref, y_ref = refs
    c = pl.program_id(0)
    wg = wg_f32_ref[...].astype(bf16)
    wu = wu_f32_ref[...].astype(bf16)
    wo = wo_f32_ref[...].astype(bf16)
    wg_ref[...] = wg
    wu_ref[...] = wu
    wo_ref[...] = wo
    x = x_ref[...]
    part = _swiglu(x, wg, wu, wo)

    @pl.when(c == 0)
    def _():
        y_ref[...] = part

    @pl.when(c > 0)
    def _():
        y_ref[...] += part

    @pl.when(c == pl.num_programs(0) - 1)
    def _():
        _front_rest(None, [slice(0, x.shape[0])], [x], [y_ref[...]], (cr_ref, sr_ref, cm_ref, sm_ref, step_ref), rest)


def _const_spec(shape):
    nd = len(shape)
    return pl.BlockSpec(shape, lambda *_: (0,) * nd, pipeline_mode=pl.Buffered(1))


def _front_small_call(x, tabs, steps, w_in, w_out, wts):
    n = x.shape[0]
    n_c = D_FF // FF_CHUNK
    full = lambda a: pl.BlockSpec(a.shape, lambda c: (0,) * a.ndim)
    gate_cols = pl.BlockSpec((D_MODEL, FF_CHUNK), lambda c: (0, c))
    up_cols = pl.BlockSpec((D_MODEL, FF_CHUNK), lambda c: (0, n_c + c))
    out_rows = pl.BlockSpec((FF_CHUNK, D_MODEL), lambda c: (c, 0))
    widths = [(D_MODEL, f32), (RET_W, bf16), (RET_W, bf16), (RET_W, bf16), (RET_W, f32), (MLA_QW, bf16),
              (MLA_QW, bf16), (MLA_VW, bf16), (MLA_KV_LORA, f32), (MLA_ROPE, f32)]
    return pl.pallas_call(
        _front_small_kernel,
        grid=(n_c,),
        in_specs=([full(x)] + [full(t) for t in tabs] + [full(steps), gate_cols, up_cols, out_rows]
                  + [_const_spec(w.shape) for w in wts]),
        out_specs=[pl.BlockSpec((n, w), lambda c: (0, 0)) for w, _ in widths] + [gate_cols, gate_cols, out_rows],
        out_shape=([jax.ShapeDtypeStruct((n, w), dt) for w, dt in widths]
                   + [jax.ShapeDtypeStruct((D_MODEL, D_FF), bf16), jax.ShapeDtypeStruct((D_MODEL, D_FF), bf16),
                      jax.ShapeDtypeStruct((D_FF, D_MODEL), bf16)]),
        scratch_shapes=[pltpu.VMEM((n, D_MODEL), f32)],
        compiler_params=pltpu.CompilerParams(dimension_semantics=("arbitrary",),
                                             vmem_limit_bytes=VMEM_LIMIT),
        name="front",
    )(x, *tabs, steps, w_in, w_in, w_out, *wts)


def _front_call(x, tabs, steps, tabs_t, wts, tm, sub):
    n = x.shape[0]
    period = steps.shape[0]
    row = lambda w: pl.BlockSpec((tm, w), lambda i: (i, 0))
    col = lambda w: pl.BlockSpec((1, w, tm), lambda i: (i, 0, 0))
    in_specs = [row(D_MODEL)] + [pl.BlockSpec((tm, t.shape[1]), lambda i: (0, 0)) for t in tabs]
    in_specs.append(pl.BlockSpec((1,) + steps.shape[1:], lambda i: (i % period, 0, 0)))
    in_specs.append(pl.BlockSpec((MLA_ROPE, tm), lambda i: (0, 0)))
    in_specs.append(pl.BlockSpec((1,) + tabs_t[1].shape[1:], lambda i: (i % period, 0, 0)))
    in_specs += [_const_spec(w.shape) for w in wts]
    rows = lambda w, dt: (row(w), jax.ShapeDtypeStruct((n, w), dt))
    cols = lambda w, dt: (col(w), jax.ShapeDtypeStruct((n // tm, w, tm), dt))
    outs = [rows(D_MODEL, f32), rows(RET_W, bf16), rows(RET_W, bf16), rows(RET_W, bf16), rows(RET_W, f32),
            cols(MLA_QW, bf16), rows(MLA_QW, bf16), cols(MLA_HEADS * MLA_VT_ROWS, bf16),
            rows(MLA_KV_LORA, f32), rows(MLA_ROPE, f32)]
    return pl.pallas_call(
        functools.partial(_front_kernel, sub),
        grid=(n // tm,),
        in_specs=in_specs,
        out_specs=[o[0] for o in outs],
        out_shape=[o[1] for o in outs],
        compiler_params=pltpu.CompilerParams(dimension_semantics=("arbitrary",),
                                             vmem_limit_bytes=VMEM_LIMIT),
        name="front_t",
    )(x, *tabs, steps, *tabs_t, *wts)


def _decay_mask(c, lg):
    r = lax.broadcasted_iota(jnp.int32, (c, c), 0)
    k = lax.broadcasted_iota(jnp.int32, (c, c), 1)
    rel = (r - k).astype(f32)
    return jnp.where(rel >= 0.0, jnp.exp(lg * jnp.maximum(rel, 0.0)), 0.0)


def _row_pow(c, lg, offset, sign):
    i = lax.broadcasted_iota(jnp.int32, (c, LANES), 0).astype(f32)
    return jnp.exp(lg * (offset + sign * i))


def _retention_tables(c, hd):
    lg = RET_LOG_GAMMA[hd]
    return _decay_mask(c, lg), _row_pow(c, lg, 1.0, 1.0), _row_pow(c, lg, c - 1.0, -1.0)


def _retention_chunk(qr_ref, kr_ref, vr_ref, gate_ref, mix_ref, rows, states, tables):
    c = rows.stop - rows.start
    heads = range(RET_HEADS)
    cols = [slice(hd * RET_DK, (hd + 1) * RET_DK) for hd in heads]
    q = [qr_ref[rows, cols[hd]] for hd in heads]
    k = [kr_ref[rows, cols[hd]] for hd in heads]
    v = [vr_ref[rows, cols[hd]] for hd in heads]
    scores = [_dot_nt(q[hd], k[hd]) for hd in heads]
    cross = [_dot(q[hd], states[hd].astype(bf16)) for hd in heads]
    kv = [_dot_tn((k[hd].astype(f32) * tables[hd][2][...]).astype(bf16), v[hd]) for hd in heads]
    new_states = [math.exp(RET_LOG_GAMMA[hd] * c) * states[hd] + kv[hd] for hd in heads]

    def finish():
        inner = [_dot((scores[hd] * tables[hd][0][...]).astype(bf16), v[hd]) for hd in heads]
        for hd in heads:
            o = inner[hd] + cross[hd] * tables[hd][1][...]
            mu = jnp.mean(o, axis=-1, keepdims=True)
            oc = o - mu
            var = jnp.mean(oc * oc, axis=-1, keepdims=True)
            y = oc * lax.rsqrt(var + LN_EPS) * gate_ref[rows, cols[hd]]
            mix_ref[rows, cols[hd]] = y.astype(bf16)

    return new_states, finish


def _att_update(m_ref, acc_ref, hd, lane0, s, v_t, ok):
    if ok is not None:
        s = jnp.where(ok, s, -jnp.inf)
    m = m_ref[hd, :, lane0:]
    m_new = jnp.maximum(m, jnp.max(s, axis=0, keepdims=True))
    p = jnp.exp2(s - m_new)
    m_ref[hd, :, lane0:] = m_new
    acc_ref[hd, :, lane0:] = jnp.exp2(m - m_new) * acc_ref[hd, :, lane0:] + _dot(v_t, p.astype(bf16))


def _mixer_kernel(qt_ref, km_ref, vt_ref, kmeta_ref, vmeta_ref, qr_ref, kr_ref, vr_ref, gate_ref,
                  krmeta_ref, vrmeta_ref, w_f32_ref, mix_ref, state_ref, w_bf16_ref,
                  m_ref, acc_ref, s_ref, dmask_ref, qdec_ref, kdec_ref):
    qi = pl.program_id(1)

    w_bf16_ref[...] = w_f32_ref[...].astype(bf16)

    @pl.when((pl.program_id(0) == 0) & (qi == 0))
    def _():
        for hd in range(RET_HEADS):
            dmask_ref[hd], qdec_ref[hd], kdec_ref[hd] = _retention_tables(RET_C, hd)

    @pl.when(qi == 0)
    def _():
        for hd in range(RET_HEADS):
            lo = hd * RET_DK
            lg = RET_LOG_GAMMA[hd]
            kd = (krmeta_ref[:, lo:lo + RET_DK].astype(f32)
                  * _row_pow(N_META, lg, N_META - 1.0, -1.0)).astype(bf16)
            state_ref[0, hd] = _dot_tn(kd, vrmeta_ref[:, lo:lo + RET_DV])

    heads = range(MLA_HEADS)
    units = [(kb, hd) for kb in range(ATT_TK // ATT_KB) for hd in heads]

    def scores(start, kb, hd, lane0=0):
        k = km_ref[0, pl.ds(start + kb * ATT_KB, ATT_KB), hd * MLA_QK_PAD:(hd + 1) * MLA_QK_PAD]
        return _dot(k, qt_ref[0, hd * MLA_QK_PAD:(hd + 1) * MLA_QK_PAD, lane0:])

    def v_block(kt, kb, hd):
        return vt_ref[kt, hd * MLA_VT_ROWS:(hd + 1) * MLA_VT_ROWS, kb * ATT_KB:(kb + 1) * ATT_KB]

    first_scores = [functools.partial(scores, 0, *units[d]) for d in range(ATT_AHEAD)]

    def trace_first_scores(n):
        for _ in range(min(n, len(first_scores))):
            d = ATT_AHEAD - len(first_scores)
            s_ref[d] = first_scores.pop(0)()

    tables = [(dmask_ref.at[hd], qdec_ref.at[hd], kdec_ref.at[hd]) for hd in range(RET_HEADS)]
    states = [state_ref[0, hd] for hd in range(RET_HEADS)]
    finishes = []
    for c in range(ATT_TQ // RET_C):
        states, finish_chunk = _retention_chunk(qr_ref, kr_ref, vr_ref, gate_ref, mix_ref,
                                                slice(c * RET_C, (c + 1) * RET_C), states, tables)
        finishes.append(finish_chunk)
    for hd in range(RET_HEADS):
        state_ref[0, hd] = states[hd]
    per_stage = -(-ATT_AHEAD // len(finishes))
    for finish_chunk in finishes:
        trace_first_scores(per_stage)
        finish_chunk()
    trace_first_scores(ATT_AHEAD)
    s_meta = [_dot(kmeta_ref[:, hd * MLA_QK_PAD:(hd + 1) * MLA_QK_PAD],
                   qt_ref[0, hd * MLA_QK_PAD:(hd + 1) * MLA_QK_PAD, :]) for hd in heads]

    p_meta = []
    for hd in heads:
        m = jnp.max(s_meta[hd], axis=0, keepdims=True)
        m_ref[hd] = m
        p_meta.append(jnp.exp2(s_meta[hd] - m).astype(bf16))
    ones = jnp.ones((N_META, MLA_VT_ROWS - MLA_V), bf16)
    for hd in heads:
        v_ext = jnp.concatenate([vmeta_ref[:, hd * MLA_V:(hd + 1) * MLA_V], ones], axis=1)
        acc_ref[hd] = _dot_tn(v_ext, p_meta[hd])

    def full_step(kt, _):
        start = pl.multiple_of(kt * ATT_TK, ATT_TK)
        pending = [s_ref[d] for d in range(ATT_AHEAD)]
        for u, (kb, hd) in enumerate(units):
            ahead = u + ATT_AHEAD
            if ahead < len(units):
                pending.append(scores(start, *units[ahead]))
            else:
                pending.append(scores(pl.multiple_of(start + ATT_TK, ATT_TK), *units[ahead - len(units)]))
            _att_update(m_ref, acc_ref, hd, 0, pending.pop(0), v_block(kt, kb, hd), None)
        for d in range(ATT_AHEAD):
            s_ref[d] = pending[d]
        return 0

    lax.fori_loop(0, qi, full_step, 0)

    start = pl.multiple_of(qi * ATT_TQ, ATT_TQ)
    pending = [s_ref[d] for d in range(ATT_AHEAD)]
    for u, (kb, hd) in enumerate(units):
        lane0 = kb * ATT_KB
        if u + ATT_AHEAD < len(units):
            nxt = units[u + ATT_AHEAD]
            pending.append(scores(start, *nxt, lane0=nxt[0] * ATT_KB))
        s_cur = pending.pop(0)
        k_chunk = lax.broadcasted_iota(jnp.int32, s_cur.shape, 0) // CHUNK
        q_chunk = lax.broadcasted_iota(jnp.int32, s_cur.shape, 1) // CHUNK
        _att_update(m_ref, acc_ref, hd, lane0, s_cur, v_block(qi, kb, hd), k_chunk <= q_chunk)
    for hd in heads:
        out = (acc_ref[hd, :MLA_V, :] / acc_ref[hd, MLA_V:MLA_V + 1, :]).T
        mix_ref[:, RET_W + hd * MLA_V:RET_W + (hd + 1) * MLA_V] = out.astype(bf16)


def _mixer_call(qt, km, vt, kmeta, vmeta, qr, kr, vr, gate, krmeta, vrmeta, w_f32, batch, seq):
    nq = seq // ATT_TQ
    tile = lambda w: pl.BlockSpec((ATT_TQ, w), lambda b, i: (b * nq + i, 0))
    const = lambda a: pl.BlockSpec((N_META, a.shape[1]), lambda b, i: (a.shape[0] // N_META - 1, 0))
    w_rows, rem = divmod(w_f32.shape[0], batch * nq)
    assert rem == 0 and w_rows % BF16_SUBLANES == 0
    w_slice = pl.BlockSpec((w_rows, w_f32.shape[1]), lambda b, i: (b * nq + i, 0))
    return pl.pallas_call(
        _mixer_kernel,
        grid=(batch, nq),
        in_specs=[pl.BlockSpec((1, MLA_QW, ATT_TQ), lambda b, i: (b * nq + i, 0, 0)),
                  pl.BlockSpec((1, seq, MLA_QW), lambda b, i: (b, 0, 0)),
                  pl.BlockSpec((seq // ATT_TK, MLA_HEADS * MLA_VT_ROWS, ATT_TK), lambda b, i: (b, 0, 0)),
                  const(kmeta), const(vmeta),
                  tile(RET_W), tile(RET_W), tile(RET_W), tile(RET_W),
                  const(krmeta), const(vrmeta), w_slice],
        out_specs=[tile(D_MODEL),
                   pl.BlockSpec((1, RET_HEADS, RET_DK, RET_DV), lambda b, i: (b, 0, 0, 0)), w_slice],
        out_shape=[jax.ShapeDtypeStruct((batch * seq, D_MODEL), bf16),
                   jax.ShapeDtypeStruct((batch, RET_HEADS, RET_DK, RET_DV), f32),
                   jax.ShapeDtypeStruct(w_f32.shape, bf16)],
        scratch_shapes=[pltpu.VMEM((MLA_HEADS, 1, ATT_TQ), f32),
                        pltpu.VMEM((MLA_HEADS, MLA_VT_ROWS, ATT_TQ), f32),
                        pltpu.VMEM((ATT_AHEAD, ATT_KB, ATT_TQ), f32),
                        pltpu.VMEM((RET_HEADS, RET_C, RET_C), f32),
                        pltpu.VMEM((RET_HEADS, RET_C, LANES), f32),
                        pltpu.VMEM((RET_HEADS, RET_C, LANES), f32)],
        compiler_params=pltpu.CompilerParams(dimension_semantics=("arbitrary", "arbitrary"),
                                             vmem_limit_bytes=VMEM_LIMIT),
        name="mixer",
    )(qt, km.reshape(batch, seq, MLA_QW), vt, kmeta, vmeta, qr, kr, vr, gate, krmeta, vrmeta, w_f32)


def _smixer_kernel(qm_ref, knew_ref, vnew_ref, kmeta_ref, vmeta_ref, cckv_ref, ckpe_ref, w_ukv_ref,
                   qr_ref, kr_ref, vr_ref, gate_ref, s0_ref, wa_f32_ref, wb_f32_ref,
                   mix_ref, state_ref, wa_bf16_ref, wb_bf16_ref):
    wa_bf16_ref[...] = wa_f32_ref[...].astype(bf16)
    wb_bf16_ref[...] = wb_f32_ref[...].astype(bf16)

    n_str = s0_ref.shape[0]
    t = qr_ref.shape[0] // n_str
    rows = [slice(s * t, (s + 1) * t) for s in range(n_str)]
    tables = [_retention_tables(t, hd) for hd in range(RET_HEADS)]
    finishes = []
    for s in range(n_str):
        states, finish_retention = _retention_chunk(qr_ref, kr_ref, vr_ref, gate_ref, mix_ref, rows[s],
                                                    [s0_ref[s, hd] for hd in range(RET_HEADS)], tables)
        for hd in range(RET_HEADS):
            state_ref[s, hd] = states[hd]
        finishes.append(finish_retention)
    for finish_retention in finishes:
        finish_retention()

    units = [(s, hd) for s in range(n_str) for hd in range(MLA_HEADS)]
    kcols = [slice(hd * MLA_QK_PAD, (hd + 1) * MLA_QK_PAD) for hd in range(MLA_HEADS)]
    vcols = [slice(hd * MLA_V, (hd + 1) * MLA_V) for hd in range(MLA_HEADS)]
    kv_c = [_dot(cckv_ref[s].astype(bf16), w_ukv_ref[...]) for s in range(n_str)]
    kpe_c = [ckpe_ref[s].astype(bf16) for s in range(n_str)]
    q = {u: qm_ref[rows[u[0]], kcols[u[1]]] for u in units}
    s_m = {u: _dot_nt(q[u], kmeta_ref[:, kcols[u[1]]]) for u in units}
    s_n = {u: _dot_nt(q[u], knew_ref[rows[u[0]], kcols[u[1]]]) for u in units}
    s_r = {u: _dot_nt(q[u][:, LANES:LANES + MLA_ROPE], kpe_c[u[0]]) for u in units}
    s_c = {u: _dot_nt(q[u][:, :LANES], kv_c[u[0]][:, u[1] * MLA_NOPE:(u[1] + 1) * MLA_NOPE].astype(bf16)) + s_r[u]
           for u in units}
    probs, denom = {}, {}
    for u in units:
        parts = [s_m[u] * MLA_SCALE, s_c[u] * MLA_SCALE, s_n[u] * MLA_SCALE]
        m = functools.reduce(jnp.maximum, [jnp.max(x, axis=-1, keepdims=True) for x in parts])
        ps = [jnp.exp(x - m) for x in parts]
        denom[u] = functools.reduce(jnp.add, [jnp.sum(p, axis=-1, keepdims=True) for p in ps])
        probs[u] = [p.astype(bf16) for p in ps]
    for s, hd in units:
        v_lo = MLA_HEADS * MLA_NOPE + hd * MLA_V
        v_c = kv_c[s][:, v_lo:v_lo + MLA_V].astype(bf16)
        acc = (_dot(probs[s, hd][0], vmeta_ref[:, vcols[hd]]) + _dot(probs[s, hd][1], v_c)
               + _dot(probs[s, hd][2], vnew_ref[rows[s], vcols[hd]]))
        mix_ref[rows[s], RET_W + hd * MLA_V:RET_W + (hd + 1) * MLA_V] = (acc / denom[s, hd]).astype(bf16)


def _smixer_call(qm, km, vm, cckv, ckpe, w_ukv, qr, kr, vr, gate, s0, wa_f32, wb_f32, t):
    db, past = cckv.shape[0], cckv.shape[1]
    steps, rem = divmod(db, SMIX_STREAMS)
    assert rem == 0
    tile = lambda w: pl.BlockSpec((SMIX_STREAMS * t, w), lambda b: (b, 0))
    meta = lambda w: pl.BlockSpec((N_META, w), lambda b: (db * t // N_META, 0))
    const = lambda a: pl.BlockSpec(a.shape, lambda b: (0,) * a.ndim)
    per = lambda *tail: pl.BlockSpec((SMIX_STREAMS,) + tail, lambda b: (b,) + (0,) * len(tail))

    def w_slice(w):
        rows, rem = divmod(w.shape[0], steps)
        assert rem == 0 and rows % BF16_SUBLANES == 0
        return pl.BlockSpec((rows, w.shape[1]), lambda b: (b, 0))

    return pl.pallas_call(
        _smixer_kernel,
        grid=(steps,),
        in_specs=[tile(MLA_QW), tile(MLA_QW), tile(MLA_VW), meta(MLA_QW), meta(MLA_VW),
                  per(past, MLA_KV_LORA), per(past, MLA_ROPE), const(w_ukv),
                  tile(RET_W), tile(RET_W), tile(RET_W), tile(RET_W),
                  per(RET_HEADS, RET_DK, RET_DV), w_slice(wa_f32), w_slice(wb_f32)],
        out_specs=[tile(D_MODEL), per(RET_HEADS, RET_DK, RET_DV), w_slice(wa_f32), w_slice(wb_f32)],
        out_shape=[jax.ShapeDtypeStruct((db * t, D_MODEL), bf16),
                   jax.ShapeDtypeStruct((db, RET_HEADS, RET_DK, RET_DV), f32),
                   jax.ShapeDtypeStruct(wa_f32.shape, bf16), jax.ShapeDtypeStruct(wb_f32.shape, bf16)],
        compiler_params=pltpu.CompilerParams(dimension_semantics=("arbitrary",),
                                             vmem_limit_bytes=VMEM_LIMIT),
        name="smixer",
    )(qm, km, vm, km, vm, cckv, ckpe, w_ukv, qr, kr, vr, gate, s0, wa_f32, wb_f32)


def _tail_kernel(sub, mix_ref, h_ref, w_mo_ref, ln2_g_ref, ln2_b_ref, w_in_ref, w_out_ref, ln3_g_ref, ln3_b_ref,
                 y_ref):
    tiles = [slice(j * sub, (j + 1) * sub) for j in range(mix_ref.shape[0] // sub)]
    mixes = [_dot(mix_ref[r, :], w_mo_ref[...]) for r in tiles]
    h2s, ys = [], []
    for r, mix in zip(tiles, mixes):
        h2 = _layer_norm(DN_ALPHA * h_ref[r, :] + mix, ln2_g_ref[...], ln2_b_ref[...])
        h2s.append(h2)
        ys.append(_swiglu(h2, w_in_ref.at[:, :D_FF], w_in_ref.at[:, D_FF:], w_out_ref))
    for r, h2, y in zip(tiles, h2s, ys):
        y_ref[r, :] = _ffn_norm(h2, y, ln3_g_ref, ln3_b_ref)


def _tail_call(mix, h, wts, tm, sub):
    n = mix.shape[0]
    row = lambda w: pl.BlockSpec((tm, w), lambda i: (i, 0))
    return pl.pallas_call(
        functools.partial(_tail_kernel, sub),
        grid=(n // tm,),
        in_specs=[row(D_MODEL), row(D_MODEL)] + [_const_spec(w.shape) for w in wts],
        out_specs=row(D_MODEL),
        out_shape=jax.ShapeDtypeStruct((n, D_MODEL), f32),
        compiler_params=pltpu.CompilerParams(dimension_semantics=("arbitrary",),
                                             vmem_limit_bytes=VMEM_LIMIT),
        name="tail",
    )(mix, h, *wts)


def _rope_tables(pos):
    posf = pos.astype(f32)[:, None]
    inv_r = ROPE_BASE ** (-jnp.arange(0, RET_DK, 2, dtype=f32) / RET_DK)
    ang = posf * inv_r[None, :]
    inv_m = ROPE_BASE ** (-jnp.arange(0, MLA_ROPE, 2, dtype=f32) / MLA_ROPE)
    angm = posf * inv_m[None, :]
    return jnp.cos(ang), jnp.sin(ang), jnp.cos(angm), jnp.sin(angm)


def kernel(x_prompt, x_sample, cache_mla_ckv, cache_mla_kpe, state_ret, meta_tokens,
           ffn1_w_in, ffn1_w_out, ln1_g, ln1_b, w_mix_in, ret_gn_g, mla_q_norm_g, mla_w_uq,
           mla_kv_norm_g, mla_w_ukv, w_mix_out, ln2_g, ln2_b, ffn2_w_in, ffn2_w_out, ln3_g, ln3_b):
    B, S, D = x_prompt.shape
    DB, T, _ = x_sample.shape
    P = cache_mla_ckv.shape[1]
    assert S % ATT_TQ == 0 and S % FRONT_TM == 0 and (B * S) % TAIL_TM == 0 and ATT_TQ % RET_C == 0
    row = lambda a: a.reshape(1, -1).astype(f32)

    w_mix = w_mix_in.astype(bf16)
    w_uq = mla_w_uq.reshape(MLA_Q_LORA, MLA_HEADS, MLA_NOPE + MLA_ROPE)
    w_uq = jnp.pad(w_uq, ((0, 0), (0, 0), (0, MLA_QK_PAD - MLA_NOPE - MLA_ROPE)))
    w_uq = w_uq.reshape(MLA_Q_LORA, MLA_QW).astype(bf16)
    w_ukv = mla_w_ukv.reshape(MLA_KV_LORA, MLA_HEADS, MLA_NOPE + MLA_V)
    w_uk = w_ukv[:, :, :MLA_NOPE].reshape(MLA_KV_LORA, -1).astype(bf16)
    w_uv = w_ukv[:, :, MLA_NOPE:].reshape(MLA_KV_LORA, -1).astype(bf16)
    shared_w = [row(ln1_g), row(ln1_b), w_mix, row(ret_gn_g), row(mla_q_norm_g)]

    n_s = DB * T
    x_small = jnp.concatenate([x_sample.reshape(n_s, D), meta_tokens.astype(x_sample.dtype)], axis=0)
    pos_small = jnp.concatenate([jnp.tile(N_META + P + jnp.arange(T), DB), jnp.arange(N_META)])
    (h_s, qr_s, kr_s, vr_s, gr_s, qm_s, km_s, vm_s, ckv_s, kpe_s, w1_gate, w1_up, w1_out) = _front_small_call(
        x_small, _rope_tables(pos_small), jnp.concatenate(_rope_tables(jnp.zeros((1,), jnp.int32)), axis=1)[:, None, :],
        ffn1_w_in, ffn1_w_out, shared_w + [w_uq, row(mla_kv_norm_g), w_uk, w_uv])

    tabs_p = _rope_tables(N_META + jnp.arange(FRONT_TM))
    steps_p = _rope_tables(FRONT_TM * jnp.arange(S // FRONT_TM))
    tabs_t_p = (jnp.concatenate([tabs_p[2].T, tabs_p[3].T], axis=0),
                jnp.concatenate([steps_p[2], steps_p[3]], axis=1)[:, :, None])
    (h_p, qr_p, kr_p, vr_p, gr_p, qt_p, km_p, vt_p, ckv_p, kpe_p) = _front_call(
        x_prompt.reshape(B * S, D), tabs_p, jnp.concatenate(steps_p, axis=1)[:, None, :], tabs_t_p,
        [w1_gate, w1_up, w1_out] + shared_w + [w_uq.T, row(mla_kv_norm_g), w_uk, w_uv.T], FRONT_TM, FRONT_SUB)

    mix_p, p_state, w2_in = _mixer_call(qt_p, km_p, vt_p, km_s, vm_s, qr_p, kr_p, vr_p, gr_p, kr_s, vr_s,
                                        ffn2_w_in, B, S)
    mix_s, s_state, w2_out, w_mo = _smixer_call(qm_s, km_s, vm_s, cache_mla_ckv, cache_mla_kpe,
                                                jnp.concatenate([w_uk, w_uv], axis=1), qr_s, kr_s, vr_s, gr_s,
                                                state_ret, ffn2_w_out, w_mix_out, T)

    tail_w = [w_mo, row(ln2_g), row(ln2_b), w2_in, w2_out, row(ln3_g), row(ln3_b)]
    y_p = _tail_call(mix_p, h_p, tail_w, TAIL_TM, FRONT_SUB)
    y_p, mix_s = lax.optimization_barrier((y_p, mix_s))
    y_s = _tail_call(mix_s, h_s, tail_w, n_s, n_s // 2)

    meta_ckv = jnp.broadcast_to(ckv_s[n_s:][None], (B, N_META, MLA_KV_LORA))
    meta_kpe = jnp.broadcast_to(kpe_s[n_s:][None], (B, N_META, MLA_ROPE))
    p_ckv = jnp.concatenate([meta_ckv, ckv_p.reshape(B, S, MLA_KV_LORA)], axis=1)
    p_kpe = jnp.concatenate([meta_kpe, kpe_p.reshape(B, S, MLA_ROPE)], axis=1)
    return (y_p.reshape(B, S, D), y_s.reshape(DB, T, D), p_ckv, p_kpe, p_state.astype(x_prompt.dtype),
            ckv_s[:n_s].reshape(DB, T, MLA_KV_LORA), kpe_s[:n_s].reshape(DB, T, MLA_ROPE),
            s_state.astype(state_ret.dtype))
```

```python
import functools
import math

import jax
import jax.numpy as jnp
from jax import lax
from jax.experimental import pallas as pl
from jax.experimental.pallas import tpu as pltpu

D_MODEL = 1024
DEPTH = 1
CHUNK = 64
N_META = 16
RET_HEADS = 4
RET_DK = 128
RET_DV = 128
MLA_HEADS = 4
MLA_NOPE = 128
MLA_ROPE = 64
MLA_V = 128
MLA_Q_LORA = 256
MLA_KV_LORA = 128
D_FF = 2816
ROPE_BASE = 10000.0
LN_EPS = 1e-5
RMS_EPS = 1e-6
DN_ALPHA = (2 * DEPTH) ** 0.25

OFF_KR = RET_HEADS * RET_DK
OFF_VR = 2 * RET_HEADS * RET_DK
OFF_GR = OFF_VR + RET_HEADS * RET_DV
OFF_CQ = OFF_GR + RET_HEADS * RET_DV
OFF_CKV = OFF_CQ + MLA_Q_LORA
OFF_KPE = OFF_CKV + MLA_KV_LORA
D_IN = OFF_KPE + MLA_ROPE

LANES = 128
MLA_QK_PAD = 2 * LANES
RET_W = RET_HEADS * RET_DK
MLA_QW = MLA_HEADS * MLA_QK_PAD
MLA_VW = MLA_HEADS * MLA_V
BF16_SUBLANES = 16
MLA_VT_ROWS = MLA_V + BF16_SUBLANES

FRONT_SUB = 256
FRONT_TM = 2 * FRONT_SUB
TAIL_TM = 4 * FRONT_SUB
FF_CHUNK = 256
ATT_TQ = FRONT_TM
ATT_TK = FRONT_TM
ATT_KB = 512
ATT_AHEAD = 4
RET_C = 256
VMEM_LIMIT = 56 * 1024 * 1024

MLA_SCALE = (MLA_NOPE + MLA_ROPE) ** -0.5
MLA_SCALE_LOG2E = MLA_SCALE * math.log2(math.e)
RET_LOG_GAMMA = tuple(math.log(1.0 - 2.0 ** (-5.0 - h)) for h in range(RET_HEADS))

f32 = jnp.float32
bf16 = jnp.bfloat16


def _dot(a, b):
    return jnp.dot(a, b, preferred_element_type=f32)


def _dot_nt(a, b):
    return lax.dot_general(a, b, (((1,), (1,)), ((), ())), preferred_element_type=f32)


def _dot_tn(a, b):
    return lax.dot_general(a, b, (((0,), (0,)), ((), ())), preferred_element_type=f32)


def _layer_norm(x, g, b):
    mu = jnp.mean(x, axis=-1, keepdims=True)
    xc = x - mu
    var = jnp.mean(xc * xc, axis=-1, keepdims=True)
    return xc * lax.rsqrt(var + LN_EPS) * g + b


def _rms_norm(x, g):
    return x * lax.rsqrt(jnp.mean(x * x, axis=-1, keepdims=True) + RMS_EPS) * g


def _silu(x):
    return x / (1.0 + jnp.exp(-x))


def _swiglu(x, w_gate, w_up, w_out):
    xb = x.astype(bf16)
    hg = _dot(xb, w_gate[...])
    hu = _dot(xb, w_up[...])
    return _dot((_silu(hg) * hu).astype(bf16), w_out[...])


def _ffn_norm(x, y, g_ref, b_ref):
    return _layer_norm(DN_ALPHA * x + 0.5 * y, g_ref[...], b_ref[...])


def _rotate(cb, sb, ca, sa):
    return cb * ca - sb * sa, sb * ca + cb * sa


def _front_kernel(sub, x_ref, cr_ref, sr_ref, cm_ref, sm_ref, step_ref, tab_t_ref, step_t_ref,
                  w_gate_ref, w_up_ref, w_out_ref, *refs):
    n_sub = x_ref.shape[0] // sub
    tiles = [slice(j * sub, (j + 1) * sub) for j in range(n_sub)]
    xs = [x_ref[r, :] for r in tiles]
    ys = [_swiglu(x, w_gate_ref, w_up_ref, w_out_ref) for x in xs]
    _front_rest((tab_t_ref, step_t_ref), tiles, xs, ys, (cr_ref, sr_ref, cm_ref, sm_ref, step_ref), refs)


def _front_rest(tabs_t, tiles, xs, ys, tabs, refs):
    cr_ref, sr_ref, cm_ref, sm_ref, step_ref = tabs
    transposed = tabs_t is not None
    if transposed:
        tab_t_ref, step_t_ref = tabs_t
    (ln_g_ref, ln_b_ref, w_mix_ref, gn_ref, qn_g_ref, w_uq_ref, kvn_g_ref, w_uk_ref, w_uv_ref,
     h_ref, qr_ref, kr_ref, vr_ref, gate_ref, qm_ref, km_ref, vm_ref, ckv_ref, kpe_ref) = refs
    ps = []
    for r, x, y in zip(tiles, xs, ys):
        h = _ffn_norm(x, y, ln_g_ref, ln_b_ref)
        h_ref[r, :] = h
        hb = h.astype(bf16)
        p_kpe = _dot(hb, w_mix_ref[:, OFF_KPE:])
        ps.append((_dot(hb, w_mix_ref[:, :OFF_KPE]),
                   jnp.concatenate([p_kpe, jnp.zeros_like(p_kpe)], axis=1)))

    for j, (r, (p, p_kpe)) in enumerate(zip(tiles, ps)):
        half_r, half_m = RET_DK // 2, MLA_ROPE // 2
        c, s = _rotate(cr_ref[r, :], sr_ref[r, :], step_ref[0, :, :half_r], step_ref[0, :, half_r:2 * half_r])
        cm, sm = _rotate(cm_ref[r, :], sm_ref[r, :], step_ref[0, :, 2 * half_r:2 * half_r + half_m],
                         step_ref[0, :, 2 * half_r + half_m:])
        z_q, z_h = jnp.zeros_like(cm), jnp.zeros_like(c)
        c_r = jnp.concatenate([c, c], axis=1)
        s_r = jnp.concatenate([-s, s], axis=1)
        c_m = jnp.concatenate([cm, cm, z_h], axis=1)
        s_ma = jnp.concatenate([-sm, z_q, z_h], axis=1)
        s_mb = jnp.concatenate([z_q, sm, z_h], axis=1)

        def rope_ret(xh):
            return xh * c_r + pltpu.roll(xh, 64, 1) * s_r

        def rope_mla(xh):
            return xh * c_m + pltpu.roll(xh, 96, 1) * s_ma + pltpu.roll(xh, 32, 1) * s_mb

        cq = _rms_norm(p[:, OFF_CQ:OFF_CKV], qn_g_ref[...]).astype(bf16)
        ckv = _rms_norm(p[:, OFF_CKV:OFF_KPE], kvn_g_ref[...])
        ckv_ref[r, :] = ckv
        ckv_b = ckv.astype(bf16)
        k_nope = _dot(ckv_b, w_uk_ref[...])
        if transposed:
            q_t = _dot_nt(w_uq_ref[...], cq) * MLA_SCALE_LOG2E
            v_t = _dot_nt(w_uv_ref[...], ckv_b)
        else:
            q = _dot(cq, w_uq_ref[...])
            v = _dot(ckv_b, w_uv_ref[...])

        for hd in range(RET_HEADS):
            lo = hd * RET_DK
            qr_ref[r, lo:lo + RET_DK] = rope_ret(p[:, lo:lo + RET_DK]).astype(bf16)
            kr_ref[r, lo:lo + RET_DK] = (rope_ret(p[:, OFF_KR + lo:OFF_KR + lo + RET_DK])
                                         * RET_DK ** -0.5).astype(bf16)
        vr_ref[r, :] = p[:, OFF_VR:OFF_GR].astype(bf16)
        gate_ref[r, :] = gn_ref[...] * _silu(p[:, OFF_GR:OFF_CQ])
        kpe = rope_mla(p_kpe)
        kpe_ref[r, :] = kpe[:, :MLA_ROPE]
        kpe_b = kpe.astype(bf16)
        for hd in range(MLA_HEADS):
            lo = hd * MLA_QK_PAD
            km_ref[r, lo:lo + LANES] = k_nope[:, hd * MLA_NOPE:(hd + 1) * MLA_NOPE].astype(bf16)
            km_ref[r, lo + LANES:lo + 2 * LANES] = kpe_b

        if transposed:
            half = MLA_ROPE // 2
            c_t, s_t = _rotate(tab_t_ref[:half, r], tab_t_ref[half:, r], step_t_ref[0, :half, :], step_t_ref[0, half:, :])
            for hd in range(MLA_HEADS):
                lo = hd * MLA_QK_PAD
                r0 = lo + MLA_NOPE
                x1 = q_t[r0:r0 + half, :]
                x2 = q_t[r0 + half:r0 + 2 * half, :]
                qm_ref[0, lo:r0, r] = q_t[lo:r0, :].astype(bf16)
                qm_ref[0, r0:r0 + half, r] = (x1 * c_t - x2 * s_t).astype(bf16)
                qm_ref[0, r0 + half:r0 + 2 * half, r] = (x1 * s_t + x2 * c_t).astype(bf16)
                qm_ref[0, r0 + 2 * half:lo + MLA_QK_PAD, r] = q_t[r0 + 2 * half:lo + MLA_QK_PAD, :].astype(bf16)
            ones = jnp.ones((MLA_VT_ROWS - MLA_V, v_t.shape[1]), bf16)
            for hd in range(MLA_HEADS):
                vm_ref[0, hd * MLA_VT_ROWS:hd * MLA_VT_ROWS + MLA_V, r] = v_t[hd * MLA_V:(hd + 1) * MLA_V, :].astype(bf16)
                vm_ref[0, hd * MLA_VT_ROWS + MLA_V:(hd + 1) * MLA_VT_ROWS, r] = ones
        else:
            for hd in range(MLA_HEADS):
                lo = hd * MLA_QK_PAD
                qm_ref[r, lo:lo + LANES] = q[:, lo:lo + LANES].astype(bf16)
                qm_ref[r, lo + LANES:lo + 2 * LANES] = rope_mla(q[:, lo + LANES:lo + 2 * LANES]).astype(bf16)
            vm_ref[r, :] = v.astype(bf16)


def _front_small_kernel(x_ref, cr_ref, sr_ref, cm_ref, sm_ref, step_ref, wg_f32_ref, wu_f32_ref, wo_f32_ref, *refs):
    *rest, wg_ref, wu_ref, wo_ref, y_ref = refs
    c = pl.program_id(0)
    wg = wg_f32_ref[...].astype(bf16)
    wu = wu_f32_ref[...].astype(bf16)
    wo = wo_f32_ref[...].astype(bf16)
    wg_ref[...] = wg
    wu_ref[...] = wu
    wo_ref[...] = wo
    x = x_ref[...]
    part = _swiglu(x, wg, wu, wo)

    @pl.when(c == 0)
    def _():
        y_ref[...] = part

    @pl.when(c > 0)
    def _():
        y_ref[...] += part

    @pl.when(c == pl.num_programs(0) - 1)
    def _():
        _front_rest(None, [slice(0, x.shape[0])], [x], [y_ref[...]], (cr_ref, sr_ref, cm_ref, sm_ref, step_ref), rest)


def _const_spec(shape):
    nd = len(shape)
    return pl.BlockSpec(shape, lambda *_: (0,) * nd, pipeline_mode=pl.Buffered(1))


def _front_small_call(x, tabs, steps, w_in, w_out, wts):
    n = x.shape[0]
    n_c = D_FF // FF_CHUNK
    full = lambda a: pl.BlockSpec(a.shape, lambda c: (0,) * a.ndim)
    gate_cols = pl.BlockSpec((D_MODEL, FF_CHUNK), lambda c: (0, c))
    up_cols = pl.BlockSpec((D_MODEL, FF_CHUNK), lambda c: (0, n_c + c))
    out_rows = pl.BlockSpec((FF_CHUNK, D_MODEL), lambda c: (c, 0))
    widths = [(D_MODEL, f32), (RET_W, bf16), (RET_W, bf16), (RET_W, bf16), (RET_W, f32), (MLA_QW, bf16),
              (MLA_QW, bf16), (MLA_VW, bf16), (MLA_KV_LORA, f32), (MLA_ROPE, f32)]
    return pl.pallas_call(
        _front_small_kernel,
        grid=(n_c,),
        in_specs=([full(x)] + [full(t) for t in tabs] + [full(steps), gate_cols, up_cols, out_rows]
                  + [_const_spec(w.shape) for w in wts]),
        out_specs=[pl.BlockSpec((n, w), lambda c: (0, 0)) for w, _ in widths] + [gate_cols, gate_cols, out_rows],
        out_shape=([jax.ShapeDtypeStruct((n, w), dt) for w, dt in widths]
                   + [jax.ShapeDtypeStruct((D_MODEL, D_FF), bf16), jax.ShapeDtypeStruct((D_MODEL, D_FF), bf16),
                      jax.ShapeDtypeStruct((D_FF, D_MODEL), bf16)]),
        scratch_shapes=[pltpu.VMEM((n, D_MODEL), f32)],
        compiler_params=pltpu.CompilerParams(dimension_semantics=("arbitrary",),
                                             vmem_limit_bytes=VMEM_LIMIT),
        name="front",
    )(x, *tabs, steps, w_in, w_in, w_out, *wts)


def _front_call(x, tabs, steps, tabs_t, wts, tm, sub):
    n = x.shape[0]
    period = steps.shape[0]
    row = lambda w: pl.BlockSpec((tm, w), lambda i: (i, 0))
    col = lambda w: pl.BlockSpec((1, w, tm), lambda i: (i, 0, 0))
    in_specs = [row(D_MODEL)] + [pl.BlockSpec((tm, t.shape[1]), lambda i: (0, 0)) for t in tabs]
    in_specs.append(pl.BlockSpec((1,) + steps.shape[1:], lambda i: (i % period, 0, 0)))
    in_specs.append(pl.BlockSpec((MLA_ROPE, tm), lambda i: (0, 0)))
    in_specs.append(pl.BlockSpec((1,) + tabs_t[1].shape[1:], lambda i: (i % period, 0, 0)))
    in_specs += [_const_spec(w.shape) for w in wts]
    rows = lambda w, dt: (row(w), jax.ShapeDtypeStruct((n, w), dt))
    cols = lambda w, dt: (col(w), jax.ShapeDtypeStruct((n // tm, w, tm), dt))
    outs = [rows(D_MODEL, f32), rows(RET_W, bf16), rows(RET_W, bf16), rows(RET_W, bf16), rows(RET_W, f32),
            cols(MLA_QW, bf16), rows(MLA_QW, bf16), cols(MLA_HEADS * MLA_VT_ROWS, bf16),
            rows(MLA_KV_LORA, f32), rows(MLA_ROPE, f32)]
    return pl.pallas_call(
        functools.partial(_front_kernel, sub),
        grid=(n // tm,),
        in_specs=in_specs,
        out_specs=[o[0] for o in outs],
        out_shape=[o[1] for o in outs],
        compiler_params=pltpu.CompilerParams(dimension_semantics=("arbitrary",),
                                             vmem_limit_bytes=VMEM_LIMIT),
        name="front_t",
    )(x, *tabs, steps, *tabs_t, *wts)


def _decay_mask(c, lg):
    r = lax.broadcasted_iota(jnp.int32, (c, c), 0)
    k = lax.broadcasted_iota(jnp.int32, (c, c), 1)
    rel = (r - k).astype(f32)
    return jnp.where(rel >= 0.0, jnp.exp(lg * jnp.maximum(rel, 0.0)), 0.0)


def _row_pow(c, lg, offset, sign):
    i = lax.broadcasted_iota(jnp.int32, (c, LANES), 0).astype(f32)
    return jnp.exp(lg * (offset + sign * i))


def _retention_tables(c, hd):
    lg = RET_LOG_GAMMA[hd]
    return _decay_mask(c, lg), _row_pow(c, lg, 1.0, 1.0), _row_pow(c, lg, c - 1.0, -1.0)


def _retention_chunk(qr_ref, kr_ref, vr_ref, gate_ref, mix_ref, rows, states, tables):
    c = rows.stop - rows.start
    heads = range(RET_HEADS)
    cols = [slice(hd * RET_DK, (hd + 1) * RET_DK) for hd in heads]
    q = [qr_ref[rows, cols[hd]] for hd in heads]
    k = [kr_ref[rows, cols[hd]] for hd in heads]
    v = [vr_ref[rows, cols[hd]] for hd in heads]
    scores = [_dot_nt(q[hd], k[hd]) for hd in heads]
    cross = [_dot(q[hd], states[hd].astype(bf16)) for hd in heads]
    kv = [_dot_tn((k[hd].astype(f32) * tables[hd][2][...]).astype(bf16), v[hd]) for hd in heads]
    new_states = [math.exp(RET_LOG_GAMMA[hd] * c) * states[hd] + kv[hd] for hd in heads]

    def finish():
        inner = [_dot((scores[hd] * tables[hd][0][...]).astype(bf16), v[hd]) for hd in heads]
        for hd in heads:
            o = inner[hd] + cross[hd] * tables[hd][1][...]
            mu = jnp.mean(o, axis=-1, keepdims=True)
            oc = o - mu
            var = jnp.mean(oc * oc, axis=-1, keepdims=True)
            y = oc * lax.rsqrt(var + LN_EPS) * gate_ref[rows, cols[hd]]
            mix_ref[rows, cols[hd]] = y.astype(bf16)

    return new_states, finish


def _att_update(m_ref, acc_ref, hd, lane0, s, v_t, ok):
    if ok is not None:
        s = jnp.where(ok, s, -jnp.inf)
    m = m_ref[hd, :, lane0:]
    m_new = jnp.maximum(m, jnp.max(s, axis=0, keepdims=True))
    p = jnp.exp2(s - m_new)
    m_ref[hd, :, lane0:] = m_new
    acc_ref[hd, :, lane0:] = jnp.exp2(m - m_new) * acc_ref[hd, :, lane0:] + _dot(v_t, p.astype(bf16))


def _mixer_kernel(qt_ref, km_ref, vt_ref, kmeta_ref, vmeta_ref, qr_ref, kr_ref, vr_ref, gate_ref,
                  krmeta_ref, vrmeta_ref, w_f32_ref, mix_ref, state_ref, w_bf16_ref,
                  m_ref, acc_ref, s_ref, dmask_ref, qdec_ref, kdec_ref):
    qi = pl.program_id(1)

    w_bf16_ref[...] = w_f32_ref[...].astype(bf16)

    @pl.when((pl.program_id(0) == 0) & (qi == 0))
    def _():
        for hd in range(RET_HEADS):
            dmask_ref[hd], qdec_ref[hd], kdec_ref[hd] = _retention_tables(RET_C, hd)

    @pl.when(qi == 0)
    def _():
        for hd in range(RET_HEADS):
            lo = hd * RET_DK
            lg = RET_LOG_GAMMA[hd]
            kd = (krmeta_ref[:, lo:lo + RET_DK].astype(f32)
                  * _row_pow(N_META, lg, N_META - 1.0, -1.0)).astype(bf16)
            state_ref[0, hd] = _dot_tn(kd, vrmeta_ref[:, lo:lo + RET_DV])

    heads = range(MLA_HEADS)
    units = [(kb, hd) for kb in range(ATT_TK // ATT_KB) for hd in heads]

    def scores(start, kb, hd, lane0=0):
        k = km_ref[0, pl.ds(start + kb * ATT_KB, ATT_KB), hd * MLA_QK_PAD:(hd + 1) * MLA_QK_PAD]
        return _dot(k, qt_ref[0, hd * MLA_QK_PAD:(hd + 1) * MLA_QK_PAD, lane0:])

    def v_block(kt, kb, hd):
        return vt_ref[kt, hd * MLA_VT_ROWS:(hd + 1) * MLA_VT_ROWS, kb * ATT_KB:(kb + 1) * ATT_KB]

    first_scores = [functools.partial(scores, 0, *units[d]) for d in range(ATT_AHEAD)]

    def trace_first_scores(n):
        for _ in range(min(n, len(first_scores))):
            d = ATT_AHEAD - len(first_scores)
            s_ref[d] = first_scores.pop(0)()

    tables = [(dmask_ref.at[hd], qdec_ref.at[hd], kdec_ref.at[hd]) for hd in range(RET_HEADS)]
    states = [state_ref[0, hd] for hd in range(RET_HEADS)]
    finishes = []
    for c in range(ATT_TQ // RET_C):
        states, finish_chunk = _retention_chunk(qr_ref, kr_ref, vr_ref, gate_ref, mix_ref,
                                                slice(c * RET_C, (c + 1) * RET_C), states, tables)
        finishes.append(finish_chunk)
    for hd in range(RET_HEADS):
        state_ref[0, hd] = states[hd]
    per_stage = -(-ATT_AHEAD // len(finishes))
    for finish_chunk in finishes:
        trace_first_scores(per_stage)
        finish_chunk()
    trace_first_scores(ATT_AHEAD)
    s_meta = [_dot(kmeta_ref[:, hd * MLA_QK_PAD:(hd + 1) * MLA_QK_PAD],
                   qt_ref[0, hd * MLA_QK_PAD:(hd + 1) * MLA_QK_PAD, :]) for hd in heads]

    p_meta = []
    for hd in heads:
        m = jnp.max(s_meta[hd], axis=0, keepdims=True)
        m_ref[hd] = m
        p_meta.append(jnp.exp2(s_meta[hd] - m).astype(bf16))
    ones = jnp.ones((N_META, MLA_VT_ROWS - MLA_V), bf16)
    for hd in heads:
        v_ext = jnp.concatenate([vmeta_ref[:, hd * MLA_V:(hd + 1) * MLA_V], ones], axis=1)
        acc_ref[hd] = _dot_tn(v_ext, p_meta[hd])

    def full_step(kt, _):
        start = pl.multiple_of(kt * ATT_TK, ATT_TK)
        pending = [s_ref[d] for d in range(ATT_AHEAD)]
        for u, (kb, hd) in enumerate(units):
            ahead = u + ATT_AHEAD
            if ahead < len(units):
                pending.append(scores(start, *units[ahead]))
            else:
                pending.append(scores(pl.multiple_of(start + ATT_TK, ATT_TK), *units[ahead - len(units)]))
            _att_update(m_ref, acc_ref, hd, 0, pending.pop(0), v_block(kt, kb, hd), None)
        for d in range(ATT_AHEAD):
            s_ref[d] = pending[d]
        return 0

    lax.fori_loop(0, qi, full_step, 0)

    start = pl.multiple_of(qi * ATT_TQ, ATT_TQ)
    pending = [s_ref[d] for d in range(ATT_AHEAD)]
    visible = {}
    for u, (kb, hd) in enumerate(units):
        lane0 = kb * ATT_KB
        if u + ATT_AHEAD < len(units):
            nxt = units[u + ATT_AHEAD]
            pending.append(scores(start, *nxt, lane0=nxt[0] * ATT_KB))
        s_cur = pending.pop(0)
        if s_cur.shape not in visible:
            k_chunk = lax.broadcasted_iota(jnp.int32, (s_cur.shape[0], 1), 0) // CHUNK
            q_chunk = lax.broadcasted_iota(jnp.int32, (1, s_cur.shape[1]), 1) // CHUNK
            visible[s_cur.shape] = k_chunk <= q_chunk
        _att_update(m_ref, acc_ref, hd, lane0, s_cur, v_block(qi, kb, hd), visible[s_cur.shape])
    for hd in heads:
        out = (acc_ref[hd, :MLA_V, :] / acc_ref[hd, MLA_V:MLA_V + 1, :]).T
        mix_ref[:, RET_W + hd * MLA_V:RET_W + (hd + 1) * MLA_V] = out.astype(bf16)


def _mixer_call(qt, km, vt, kmeta, vmeta, qr, kr, vr, gate, krmeta, vrmeta, w_f32, batch, seq):
    nq = seq // ATT_TQ
    tile = lambda w: pl.BlockSpec((ATT_TQ, w), lambda b, i: (b * nq + i, 0))
    const = lambda a: pl.BlockSpec((N_META, a.shape[1]), lambda b, i: (a.shape[0] // N_META - 1, 0))
    w_rows, rem = divmod(w_f32.shape[0], batch * nq)
    assert rem == 0 and w_rows % BF16_SUBLANES == 0
    w_slice = pl.BlockSpec((w_rows, w_f32.shape[1]), lambda b, i: (b * nq + i, 0))
    return pl.pallas_call(
        _mixer_kernel,
        grid=(batch, nq),
        in_specs=[pl.BlockSpec((1, MLA_QW, ATT_TQ), lambda b, i: (b * nq + i, 0, 0)),
                  pl.BlockSpec((1, seq, MLA_QW), lambda b, i: (b, 0, 0)),
                  pl.BlockSpec((seq // ATT_TK, MLA_HEADS * MLA_VT_ROWS, ATT_TK), lambda b, i: (b, 0, 0)),
                  const(kmeta), const(vmeta),
                  tile(RET_W), tile(RET_W), tile(RET_W), tile(RET_W),
                  const(krmeta), const(vrmeta), w_slice],
        out_specs=[tile(D_MODEL),
                   pl.BlockSpec((1, RET_HEADS, RET_DK, RET_DV), lambda b, i: (b, 0, 0, 0)), w_slice],
        out_shape=[jax.ShapeDtypeStruct((batch * seq, D_MODEL), bf16),
                   jax.ShapeDtypeStruct((batch, RET_HEADS, RET_DK, RET_DV), f32),
                   jax.ShapeDtypeStruct(w_f32.shape, bf16)],
        scratch_shapes=[pltpu.VMEM((MLA_HEADS, 1, ATT_TQ), f32),
                        pltpu.VMEM((MLA_HEADS, MLA_VT_ROWS, ATT_TQ), f32),
                        pltpu.VMEM((ATT_AHEAD, ATT_KB, ATT_TQ), f32),
                        pltpu.VMEM((RET_HEADS, RET_C, RET_C), f32),
                        pltpu.VMEM((RET_HEADS, RET_C, LANES), f32),
                        pltpu.VMEM((RET_HEADS, RET_C, LANES), f32)],
        compiler_params=pltpu.CompilerParams(dimension_semantics=("arbitrary", "arbitrary"),
                                             vmem_limit_bytes=VMEM_LIMIT),
        name="mixer",
    )(qt, km.reshape(batch, seq, MLA_QW), vt, kmeta, vmeta, qr, kr, vr, gate, krmeta, vrmeta, w_f32)


def _smixer_kernel(qm_ref, cnew_ref, pnew_ref, cmeta_ref, pmeta_ref, ccache_ref, pcache_ref, w_qk_ref, w_ov_ref,
                   qr_ref, kr_ref, vr_ref, gate_ref, s0_ref, wa_f32_ref, wb_f32_ref,
                   mix_ref, state_ref, wa_bf16_ref, wb_bf16_ref):
    wa_bf16_ref[...] = wa_f32_ref[...].astype(bf16)
    wb_bf16_ref[...] = wb_f32_ref[...].astype(bf16)

    t = qr_ref.shape[0]
    states, finish_retention = _retention_chunk(qr_ref, kr_ref, vr_ref, gate_ref, mix_ref, slice(0, t),
                                                [s0_ref[0, hd] for hd in range(RET_HEADS)],
                                                [_retention_tables(t, hd) for hd in range(RET_HEADS)])
    for hd in range(RET_HEADS):
        state_ref[0, hd] = states[hd]

    heads = range(MLA_HEADS)
    q_nope = jnp.concatenate([qm_ref[:, hd * MLA_QK_PAD:hd * MLA_QK_PAD + MLA_NOPE] for hd in heads], axis=1)
    q_abs = _dot(q_nope, w_qk_ref[...])
    finish_retention()
    q_lat = jnp.concatenate([q_abs[:, hd * MLA_KV_LORA:(hd + 1) * MLA_KV_LORA] for hd in heads], axis=0).astype(bf16)
    q_rope = jnp.concatenate([qm_ref[:, hd * MLA_QK_PAD + MLA_NOPE:hd * MLA_QK_PAD + MLA_NOPE + MLA_ROPE]
                              for hd in heads], axis=0)
    lat = [cmeta_ref[...].astype(bf16), ccache_ref[0].astype(bf16), cnew_ref[...].astype(bf16)]
    kpe = [pmeta_ref[...].astype(bf16), pcache_ref[0].astype(bf16), pnew_ref[...].astype(bf16)]
    scores = [(_dot_nt(q_lat, c) + _dot_nt(q_rope, p)) * MLA_SCALE for c, p in zip(lat, kpe)]
    m = functools.reduce(jnp.maximum, [jnp.max(s, axis=-1, keepdims=True) for s in scores])
    probs = [jnp.exp(s - m) for s in scores]
    denom = functools.reduce(jnp.add, [jnp.sum(p, axis=-1, keepdims=True) for p in probs])
    o_lat = functools.reduce(jnp.add, [_dot(p.astype(bf16), c) for p, c in zip(probs, lat)]) / denom
    o_all = jnp.concatenate([o_lat[hd * t:(hd + 1) * t, :] for hd in heads], axis=1).astype(bf16)
    mix_ref[:, RET_W:] = _dot(o_all, w_ov_ref[...]).astype(bf16)


def _smixer_call(qm, ckv, kpe, cckv, ckpe, w_qk, w_ov, qr, kr, vr, gate, s0, wa_f32, wb_f32, t):
    db, past = cckv.shape[0], cckv.shape[1]
    tile = lambda w: pl.BlockSpec((t, w), lambda b: (b, 0))
    meta = lambda w: pl.BlockSpec((N_META, w), lambda b: (db * t // N_META, 0))
    const = lambda a: pl.BlockSpec(a.shape, lambda b: (0,) * a.ndim)
    per = lambda *tail: pl.BlockSpec((1,) + tail, lambda b: (b,) + (0,) * len(tail))

    def w_slice(w):
        rows, rem = divmod(w.shape[0], db)
        assert rem == 0 and rows % BF16_SUBLANES == 0
        return pl.BlockSpec((rows, w.shape[1]), lambda b: (b, 0))

    return pl.pallas_call(
        _smixer_kernel,
        grid=(db,),
        in_specs=[tile(MLA_QW), tile(MLA_KV_LORA), tile(MLA_ROPE), meta(MLA_KV_LORA), meta(MLA_ROPE),
                  per(past, MLA_KV_LORA), per(past, MLA_ROPE), const(w_qk), const(w_ov),
                  tile(RET_W), tile(RET_W), tile(RET_W), tile(RET_W),
                  per(RET_HEADS, RET_DK, RET_DV), w_slice(wa_f32), w_slice(wb_f32)],
        out_specs=[tile(D_MODEL), per(RET_HEADS, RET_DK, RET_DV), w_slice(wa_f32), w_slice(wb_f32)],
        out_shape=[jax.ShapeDtypeStruct((db * t, D_MODEL), bf16),
                   jax.ShapeDtypeStruct((db, RET_HEADS, RET_DK, RET_DV), f32),
                   jax.ShapeDtypeStruct(wa_f32.shape, bf16), jax.ShapeDtypeStruct(wb_f32.shape, bf16)],
        compiler_params=pltpu.CompilerParams(dimension_semantics=("arbitrary",),
                                             vmem_limit_bytes=VMEM_LIMIT),
        name="smixer",
    )(qm, ckv, kpe, ckv, kpe, cckv, ckpe, w_qk, w_ov, qr, kr, vr, gate, s0, wa_f32, wb_f32)


def _tail_kernel(sub, mix_ref, h_ref, w_mo_ref, ln2_g_ref, ln2_b_ref, w_in_ref, w_out_ref, ln3_g_ref, ln3_b_ref,
                 y_ref):
    tiles = [slice(j * sub, (j + 1) * sub) for j in range(mix_ref.shape[0] // sub)]
    mixes = [_dot(mix_ref[r, :], w_mo_ref[...]) for r in tiles]
    h2s, ys = [], []
    for r, mix in zip(tiles, mixes):
        h2 = _layer_norm(DN_ALPHA * h_ref[r, :] + mix, ln2_g_ref[...], ln2_b_ref[...])
        h2s.append(h2)
        ys.append(_swiglu(h2, w_in_ref.at[:, :D_FF], w_in_ref.at[:, D_FF:], w_out_ref))
    for r, h2, y in zip(tiles, h2s, ys):
        y_ref[r, :] = _ffn_norm(h2, y, ln3_g_ref, ln3_b_ref)


def _tail_call(mix, h, wts, tm, sub):
    n = mix.shape[0]
    row = lambda w: pl.BlockSpec((tm, w), lambda i: (i, 0))
    return pl.pallas_call(
        functools.partial(_tail_kernel, sub),
        grid=(n // tm,),
        in_specs=[row(D_MODEL), row(D_MODEL)] + [_const_spec(w.shape) for w in wts],
        out_specs=row(D_MODEL),
        out_shape=jax.ShapeDtypeStruct((n, D_MODEL), f32),
        compiler_params=pltpu.CompilerParams(dimension_semantics=("arbitrary",),
                                             vmem_limit_bytes=VMEM_LIMIT),
        name="tail",
    )(mix, h, *wts)


def _rope_tables(pos):
    posf = pos.astype(f32)[:, None]
    inv_r = ROPE_BASE ** (-jnp.arange(0, RET_DK, 2, dtype=f32) / RET_DK)
    ang = posf * inv_r[None, :]
    inv_m = ROPE_BASE ** (-jnp.arange(0, MLA_ROPE, 2, dtype=f32) / MLA_ROPE)
    angm = posf * inv_m[None, :]
    return jnp.cos(ang), jnp.sin(ang), jnp.cos(angm), jnp.sin(angm)


def kernel(x_prompt, x_sample, cache_mla_ckv, cache_mla_kpe, state_ret, meta_tokens,
           ffn1_w_in, ffn1_w_out, ln1_g, ln1_b, w_mix_in, ret_gn_g, mla_q_norm_g, mla_w_uq,
           mla_kv_norm_g, mla_w_ukv, w_mix_out, ln2_g, ln2_b, ffn2_w_in, ffn2_w_out, ln3_g, ln3_b):
    B, S, D = x_prompt.shape
    DB, T, _ = x_sample.shape
    P = cache_mla_ckv.shape[1]
    assert S % ATT_TQ == 0 and S % FRONT_TM == 0 and (B * S) % TAIL_TM == 0 and ATT_TQ % RET_C == 0
    row = lambda a: a.reshape(1, -1).astype(f32)

    w_mix = w_mix_in.astype(bf16)
    w_uq = mla_w_uq.reshape(MLA_Q_LORA, MLA_HEADS, MLA_NOPE + MLA_ROPE)
    w_uq = jnp.pad(w_uq, ((0, 0), (0, 0), (0, MLA_QK_PAD - MLA_NOPE - MLA_ROPE)))
    w_uq = w_uq.reshape(MLA_Q_LORA, MLA_QW).astype(bf16)
    w_ukv = mla_w_ukv.reshape(MLA_KV_LORA, MLA_HEADS, MLA_NOPE + MLA_V)
    w_uk = w_ukv[:, :, :MLA_NOPE].reshape(MLA_KV_LORA, -1).astype(bf16)
    w_uv = w_ukv[:, :, MLA_NOPE:].reshape(MLA_KV_LORA, -1).astype(bf16)
    shared_w = [row(ln1_g), row(ln1_b), w_mix, row(ret_gn_g), row(mla_q_norm_g)]

    n_s = DB * T
    x_small = jnp.concatenate([x_sample.reshape(n_s, D), meta_tokens.astype(x_sample.dtype)], axis=0)
    pos_small = jnp.concatenate([jnp.tile(N_META + P + jnp.arange(T), DB), jnp.arange(N_META)])
    (h_s, qr_s, kr_s, vr_s, gr_s, qm_s, km_s, vm_s, ckv_s, kpe_s, w1_gate, w1_up, w1_out) = _front_small_call(
        x_small, _rope_tables(pos_small), jnp.concatenate(_rope_tables(jnp.zeros((1,), jnp.int32)), axis=1)[:, None, :],
        ffn1_w_in, ffn1_w_out, shared_w + [w_uq, row(mla_kv_norm_g), w_uk, w_uv])

    tabs_p = _rope_tables(N_META + jnp.arange(FRONT_TM))
    steps_p = _rope_tables(FRONT_TM * jnp.arange(S // FRONT_TM))
    tabs_t_p = (jnp.concatenate([tabs_p[2].T, tabs_p[3].T], axis=0),
                jnp.concatenate([steps_p[2], steps_p[3]], axis=1)[:, :, None])
    (h_p, qr_p, kr_p, vr_p, gr_p, qt_p, km_p, vt_p, ckv_p, kpe_p) = _front_call(
        x_prompt.reshape(B * S, D), tabs_p, jnp.concatenate(steps_p, axis=1)[:, None, :], tabs_t_p,
        [w1_gate, w1_up, w1_out] + shared_w + [w_uq.T, row(mla_kv_norm_g), w_uk, w_uv.T], FRONT_TM, FRONT_SUB)

    mix_p, p_state, w2_in = _mixer_call(qt_p, km_p, vt_p, km_s, vm_s, qr_p, kr_p, vr_p, gr_p, kr_s, vr_s,
                                        ffn2_w_in, B, S)
    blocks = lambda w, n: [w[:, hd * n:(hd + 1) * n] for hd in range(MLA_HEADS)]
    w_qk = jax.scipy.linalg.block_diag(*[blk.T for blk in blocks(w_uk, MLA_NOPE)])
    w_ov = jax.scipy.linalg.block_diag(*blocks(w_uv, MLA_V))
    mix_s, s_state, w2_out, w_mo = _smixer_call(qm_s, ckv_s, kpe_s, cache_mla_ckv, cache_mla_kpe, w_qk, w_ov,
                                                qr_s, kr_s, vr_s, gr_s, state_ret, ffn2_w_out, w_mix_out, T)

    tail_w = [w_mo, row(ln2_g), row(ln2_b), w2_in, w2_out, row(ln3_g), row(ln3_b)]
    y_p = _tail_call(mix_p, h_p, tail_w, TAIL_TM, FRONT_SUB)
    y_p, mix_s = lax.optimization_barrier((y_p, mix_s))
    y_s = _tail_call(mix_s, h_s, tail_w, n_s, n_s // 2)

    meta_ckv = jnp.broadcast_to(ckv_s[n_s:][None], (B, N_META, MLA_KV_LORA))
    meta_kpe = jnp.broadcast_to(kpe_s[n_s:][None], (B, N_META, MLA_ROPE))
    p_ckv = jnp.concatenate([meta_ckv, ckv_p.reshape(B, S, MLA_KV_LORA)], axis=1)
    p_kpe = jnp.concatenate([meta_kpe, kpe_p.reshape(B, S, MLA_ROPE)], axis=1)
    return (y_p.reshape(B, S, D), y_s.reshape(DB, T, D), p_ckv, p_kpe, p_state.astype(x_prompt.dtype),
            ckv_s[:n_s].reshape(DB, T, MLA_KV_LORA), kpe_s[:n_s].reshape(DB, T, MLA_ROPE),
            s_state.astype(state_ret.dtype))
```

```python
import functools
import math

import jax
import jax.numpy as jnp
from jax import lax
from jax.experimental import pallas as pl
from jax.experimental.pallas import tpu as pltpu

D_MODEL = 1024
DEPTH = 1
CHUNK = 64
N_META = 16
RET_HEADS = 4
RET_DK = 128
RET_DV = 128
MLA_HEADS = 4
MLA_NOPE = 128
MLA_ROPE = 64
MLA_V = 128
MLA_Q_LORA = 256
MLA_KV_LORA = 128
D_FF = 2816
ROPE_BASE = 10000.0
LN_EPS = 1e-5
RMS_EPS = 1e-6
DN_ALPHA = (2 * DEPTH) ** 0.25

OFF_KR = RET_HEADS * RET_DK
OFF_VR = 2 * RET_HEADS * RET_DK
OFF_GR = OFF_VR + RET_HEADS * RET_DV
OFF_CQ = OFF_GR + RET_HEADS * RET_DV
OFF_CKV = OFF_CQ + MLA_Q_LORA
OFF_KPE = OFF_CKV + MLA_KV_LORA
D_IN = OFF_KPE + MLA_ROPE

LANES = 128
MLA_QK_PAD = 2 * LANES
RET_W = RET_HEADS * RET_DK
MLA_QW = MLA_HEADS * MLA_QK_PAD
MLA_VW = MLA_HEADS * MLA_V
BF16_SUBLANES = 16
MLA_VT_ROWS = MLA_V + BF16_SUBLANES

FRONT_SUB = 256
FRONT_TM = 2 * FRONT_SUB
TAIL_TM = 4 * FRONT_SUB
FF_CHUNK = 256
ATT_TQ = FRONT_TM
ATT_TK = FRONT_TM
ATT_KB = 512
ATT_AHEAD = 4
RET_C = 256
VMEM_LIMIT = 56 * 1024 * 1024

MLA_SCALE = (MLA_NOPE + MLA_ROPE) ** -0.5
MLA_SCALE_LOG2E = MLA_SCALE * math.log2(math.e)
RET_LOG_GAMMA = tuple(math.log(1.0 - 2.0 ** (-5.0 - h)) for h in range(RET_HEADS))

f32 = jnp.float32
bf16 = jnp.bfloat16


def _dot(a, b):
    return jnp.dot(a, b, preferred_element_type=f32)


def _dot_nt(a, b):
    return lax.dot_general(a, b, (((1,), (1,)), ((), ())), preferred_element_type=f32)


def _dot_tn(a, b):
    return lax.dot_general(a, b, (((0,), (0,)), ((), ())), preferred_element_type=f32)


def _layer_norm(x, g, b):
    mu = jnp.mean(x, axis=-1, keepdims=True)
    xc = x - mu
    var = jnp.mean(xc * xc, axis=-1, keepdims=True)
    return xc * lax.rsqrt(var + LN_EPS) * g + b


def _rms_norm(x, g):
    return x * lax.rsqrt(jnp.mean(x * x, axis=-1, keepdims=True) + RMS_EPS) * g


def _silu(x):
    return x / (1.0 + jnp.exp(-x))


def _swiglu(x, w_gate, w_up, w_out):
    xb = x.astype(bf16)
    hg = _dot(xb, w_gate[...])
    hu = _dot(xb, w_up[...])
    return _dot((_silu(hg) * hu).astype(bf16), w_out[...])


def _ffn_norm(x, y, g_ref, b_ref):
    return _layer_norm(DN_ALPHA * x + 0.5 * y, g_ref[...], b_ref[...])


def _rotate(cb, sb, ca, sa):
    return cb * ca - sb * sa, sb * ca + cb * sa


def _front_kernel(sub, x_ref, cr_ref, sr_ref, cm_ref, sm_ref, step_ref, tab_t_ref, step_t_ref,
                  w_gate_ref, w_up_ref, w_out_ref, *refs):
    n_sub = x_ref.shape[0] // sub
    tiles = [slice(j * sub, (j + 1) * sub) for j in range(n_sub)]
    xs = [x_ref[r, :] for r in tiles]
    ys = [_swiglu(x, w_gate_ref, w_up_ref, w_out_ref) for x in xs]
    _front_rest((tab_t_ref, step_t_ref), tiles, xs, ys, (cr_ref, sr_ref, cm_ref, sm_ref, step_ref), refs)


def _front_rest(tabs_t, tiles, xs, ys, tabs, refs):
    cr_ref, sr_ref, cm_ref, sm_ref, step_ref = tabs
    transposed = tabs_t is not None
    if transposed:
        tab_t_ref, step_t_ref = tabs_t
    (ln_g_ref, ln_b_ref, w_mix_ref, gn_ref, qn_g_ref, w_uq_ref, kvn_g_ref, w_uk_ref, w_uv_ref,
     h_ref, qr_ref, kr_ref, vr_ref, gate_ref, qm_ref, km_ref, vm_ref, ckv_ref, kpe_ref) = refs
    ps = []
    for r, x, y in zip(tiles, xs, ys):
        h = _ffn_norm(x, y, ln_g_ref, ln_b_ref)
        h_ref[r, :] = h
        hb = h.astype(bf16)
        p_kpe = _dot_nt(hb, w_mix_ref[OFF_KPE:, :])
        ps.append((_dot_nt(hb, w_mix_ref[:OFF_KPE, :]),
                   jnp.concatenate([p_kpe, jnp.zeros_like(p_kpe)], axis=1)))

    for j, (r, (p, p_kpe)) in enumerate(zip(tiles, ps)):
        half_r, half_m = RET_DK // 2, MLA_ROPE // 2
        c, s = _rotate(cr_ref[r, :], sr_ref[r, :], step_ref[0, :, :half_r], step_ref[0, :, half_r:2 * half_r])
        cm, sm = _rotate(cm_ref[r, :], sm_ref[r, :], step_ref[0, :, 2 * half_r:2 * half_r + half_m],
                         step_ref[0, :, 2 * half_r + half_m:])
        z_q, z_h = jnp.zeros_like(cm), jnp.zeros_like(c)
        c_r = jnp.concatenate([c, c], axis=1)
        s_r = jnp.concatenate([-s, s], axis=1)
        c_m = jnp.concatenate([cm, cm, z_h], axis=1)
        s_ma = jnp.concatenate([-sm, z_q, z_h], axis=1)
        s_mb = jnp.concatenate([z_q, sm, z_h], axis=1)

        def rope_ret(xh):
            return xh * c_r + pltpu.roll(xh, 64, 1) * s_r

        def rope_mla(xh):
            return xh * c_m + pltpu.roll(xh, 96, 1) * s_ma + pltpu.roll(xh, 32, 1) * s_mb

        cq = _rms_norm(p[:, OFF_CQ:OFF_CKV], qn_g_ref[...]).astype(bf16)
        ckv = _rms_norm(p[:, OFF_CKV:OFF_KPE], kvn_g_ref[...])
        ckv_ref[r, :] = ckv
        ckv_b = ckv.astype(bf16)
        k_nope = _dot(ckv_b, w_uk_ref[...])
        if transposed:
            q_t = _dot_nt(w_uq_ref[...], cq) * MLA_SCALE_LOG2E
            v_t = _dot_nt(w_uv_ref[...], ckv_b)
        else:
            q = _dot(cq, w_uq_ref[...])
            v = _dot(ckv_b, w_uv_ref[...])

        for hd in range(RET_HEADS):
            lo = hd * RET_DK
            qr_ref[r, lo:lo + RET_DK] = rope_ret(p[:, lo:lo + RET_DK]).astype(bf16)
            kr_ref[r, lo:lo + RET_DK] = (rope_ret(p[:, OFF_KR + lo:OFF_KR + lo + RET_DK])
                                         * RET_DK ** -0.5).astype(bf16)
        vr_ref[r, :] = p[:, OFF_VR:OFF_GR].astype(bf16)
        gate_ref[r, :] = gn_ref[...] * _silu(p[:, OFF_GR:OFF_CQ])
        kpe = rope_mla(p_kpe)
        if transposed:
            kpe_ref[0, :, r] = kpe.T[:MLA_ROPE, :]
        else:
            kpe_ref[r, :] = kpe[:, :MLA_ROPE]
        kpe_b = kpe.astype(bf16)
        for hd in range(MLA_HEADS):
            lo = hd * MLA_QK_PAD
            km_ref[r, lo:lo + LANES] = k_nope[:, hd * MLA_NOPE:(hd + 1) * MLA_NOPE].astype(bf16)
            km_ref[r, lo + LANES:lo + 2 * LANES] = kpe_b

        if transposed:
            half = MLA_ROPE // 2
            c_t, s_t = _rotate(tab_t_ref[:half, r], tab_t_ref[half:, r], step_t_ref[0, :half, :], step_t_ref[0, half:, :])
            for hd in range(MLA_HEADS):
                lo = hd * MLA_QK_PAD
                r0 = lo + MLA_NOPE
                x1 = q_t[r0:r0 + half, :]
                x2 = q_t[r0 + half:r0 + 2 * half, :]
                qm_ref[0, lo:r0, r] = q_t[lo:r0, :].astype(bf16)
                qm_ref[0, r0:r0 + half, r] = (x1 * c_t - x2 * s_t).astype(bf16)
                qm_ref[0, r0 + half:r0 + 2 * half, r] = (x1 * s_t + x2 * c_t).astype(bf16)
                qm_ref[0, r0 + 2 * half:lo + MLA_QK_PAD, r] = q_t[r0 + 2 * half:lo + MLA_QK_PAD, :].astype(bf16)
            ones = jnp.ones((MLA_VT_ROWS - MLA_V, v_t.shape[1]), bf16)
            for hd in range(MLA_HEADS):
                vm_ref[0, hd * MLA_VT_ROWS:hd * MLA_VT_ROWS + MLA_V, r] = v_t[hd * MLA_V:(hd + 1) * MLA_V, :].astype(bf16)
                vm_ref[0, hd * MLA_VT_ROWS + MLA_V:(hd + 1) * MLA_VT_ROWS, r] = ones
        else:
            for hd in range(MLA_HEADS):
                lo = hd * MLA_QK_PAD
                qm_ref[r, lo:lo + LANES] = q[:, lo:lo + LANES].astype(bf16)
                qm_ref[r, lo + LANES:lo + 2 * LANES] = rope_mla(q[:, lo + LANES:lo + 2 * LANES]).astype(bf16)
            vm_ref[r, :] = v.astype(bf16)


def _front_small_kernel(x_ref, cr_ref, sr_ref, cm_ref, sm_ref, step_ref, wg_f32_ref, wu_f32_ref, wo_f32_ref, *refs):
    *rest, wg_ref, wu_ref, wo_ref, wmix_ref, y_ref = refs
    c = pl.program_id(0)

    @pl.when(c == 0)
    def _():
        wmix_ref[...] = rest[2][...].astype(bf16)

    rest = rest[:2] + [wmix_ref] + rest[3:]
    wg = wg_f32_ref[...].astype(bf16)
    wu = wu_f32_ref[...].astype(bf16)
    wo = wo_f32_ref[...].astype(bf16)
    wg_ref[...] = wg
    wu_ref[...] = wu
    wo_ref[...] = wo
    x = x_ref[...]
    part = _swiglu(x, wg, wu, wo)

    @pl.when(c == 0)
    def _():
        y_ref[...] = part

    @pl.when(c > 0)
    def _():
        y_ref[...] += part

    @pl.when(c == pl.num_programs(0) - 1)
    def _():
        _front_rest(None, [slice(0, x.shape[0])], [x], [y_ref[...]], (cr_ref, sr_ref, cm_ref, sm_ref, step_ref), rest)


def _const_spec(shape):
    nd = len(shape)
    return pl.BlockSpec(shape, lambda *_: (0,) * nd, pipeline_mode=pl.Buffered(1))


def _front_small_call(x, tabs, steps, w_in, w_out, wts):
    n = x.shape[0]
    n_c = D_FF // FF_CHUNK
    full = lambda a: pl.BlockSpec(a.shape, lambda c: (0,) * a.ndim)
    gate_cols = pl.BlockSpec((D_MODEL, FF_CHUNK), lambda c: (0, c))
    up_cols = pl.BlockSpec((D_MODEL, FF_CHUNK), lambda c: (0, n_c + c))
    out_rows = pl.BlockSpec((FF_CHUNK, D_MODEL), lambda c: (c, 0))
    widths = [(D_MODEL, f32), (RET_W, bf16), (RET_W, bf16), (RET_W, bf16), (RET_W, f32), (MLA_QW, bf16),
              (MLA_QW, bf16), (MLA_VW, bf16), (MLA_KV_LORA, f32), (MLA_ROPE, f32)]
    return pl.pallas_call(
        _front_small_kernel,
        grid=(n_c,),
        in_specs=([full(x)] + [full(t) for t in tabs] + [full(steps), gate_cols, up_cols, out_rows]
                  + [_const_spec(w.shape) for w in wts]),
        out_specs=([pl.BlockSpec((n, w), lambda c: (0, 0)) for w, _ in widths]
                   + [gate_cols, gate_cols, out_rows, full(wts[2])]),
        out_shape=([jax.ShapeDtypeStruct((n, w), dt) for w, dt in widths]
                   + [jax.ShapeDtypeStruct((D_MODEL, D_FF), bf16), jax.ShapeDtypeStruct((D_MODEL, D_FF), bf16),
                      jax.ShapeDtypeStruct((D_FF, D_MODEL), bf16), jax.ShapeDtypeStruct(wts[2].shape, bf16)]),
        scratch_shapes=[pltpu.VMEM((n, D_MODEL), f32)],
        compiler_params=pltpu.CompilerParams(dimension_semantics=("arbitrary",),
                                             vmem_limit_bytes=VMEM_LIMIT),
        name="front",
    )(x, *tabs, steps, w_in, w_in, w_out, *wts)


def _front_call(x, tabs, steps, tabs_t, wts, tm, sub):
    n = x.shape[0]
    period = steps.shape[0]
    row = lambda w: pl.BlockSpec((tm, w), lambda i: (i, 0))
    col = lambda w: pl.BlockSpec((1, w, tm), lambda i: (i, 0, 0))
    in_specs = [row(D_MODEL)] + [pl.BlockSpec((tm, t.shape[1]), lambda i: (0, 0)) for t in tabs]
    in_specs.append(pl.BlockSpec((1,) + steps.shape[1:], lambda i: (i % period, 0, 0)))
    in_specs.append(pl.BlockSpec((MLA_ROPE, tm), lambda i: (0, 0)))
    in_specs.append(pl.BlockSpec((1,) + tabs_t[1].shape[1:], lambda i: (i % period, 0, 0)))
    in_specs += [_const_spec(w.shape) for w in wts]
    rows = lambda w, dt: (row(w), jax.ShapeDtypeStruct((n, w), dt))
    cols = lambda w, dt: (col(w), jax.ShapeDtypeStruct((n // tm, w, tm), dt))
    outs = [rows(D_MODEL, f32), rows(RET_W, bf16), rows(RET_W, bf16), rows(RET_W, bf16), rows(RET_W, f32),
            cols(MLA_QW, bf16), rows(MLA_QW, bf16), cols(MLA_HEADS * MLA_VT_ROWS, bf16),
            rows(MLA_KV_LORA, f32),
            (pl.BlockSpec((1, MLA_ROPE, tm), lambda i: (i // period, 0, i % period)),
             jax.ShapeDtypeStruct((n // (period * tm), MLA_ROPE, period * tm), f32))]
    return pl.pallas_call(
        functools.partial(_front_kernel, sub),
        grid=(n // tm,),
        in_specs=in_specs,
        out_specs=[o[0] for o in outs],
        out_shape=[o[1] for o in outs],
        compiler_params=pltpu.CompilerParams(dimension_semantics=("arbitrary",),
                                             vmem_limit_bytes=VMEM_LIMIT),
        name="front_t",
    )(x, *tabs, steps, *tabs_t, *wts)


def _decay_mask(c, lg):
    r = lax.broadcasted_iota(jnp.int32, (c, c), 0)
    k = lax.broadcasted_iota(jnp.int32, (c, c), 1)
    rel = (r - k).astype(f32)
    return jnp.where(rel >= 0.0, jnp.exp(lg * jnp.maximum(rel, 0.0)), 0.0)


def _row_pow(c, lg, offset, sign):
    i = lax.broadcasted_iota(jnp.int32, (c, LANES), 0).astype(f32)
    return jnp.exp(lg * (offset + sign * i))


def _retention_tables(c, hd):
    lg = RET_LOG_GAMMA[hd]
    return _decay_mask(c, lg), _row_pow(c, lg, 1.0, 1.0), _row_pow(c, lg, c - 1.0, -1.0)


def _retention_chunk(qr_ref, kr_ref, vr_ref, gate_ref, mix_ref, rows, states, tables):
    c = rows.stop - rows.start
    heads = range(RET_HEADS)
    cols = [slice(hd * RET_DK, (hd + 1) * RET_DK) for hd in heads]
    q = [qr_ref[rows, cols[hd]] for hd in heads]
    k = [kr_ref[rows, cols[hd]] for hd in heads]
    v = [vr_ref[rows, cols[hd]] for hd in heads]
    scores = [_dot_nt(q[hd], k[hd]) for hd in heads]
    cross = [_dot(q[hd], states[hd].astype(bf16)) for hd in heads]
    kv = [_dot_tn((k[hd].astype(f32) * tables[hd][2][...]).astype(bf16), v[hd]) for hd in heads]
    new_states = [math.exp(RET_LOG_GAMMA[hd] * c) * states[hd] + kv[hd] for hd in heads]

    def finish():
        inner = [_dot((scores[hd] * tables[hd][0][...]).astype(bf16), v[hd]) for hd in heads]
        for hd in heads:
            o = inner[hd] + cross[hd] * tables[hd][1][...]
            mu = jnp.mean(o, axis=-1, keepdims=True)
            oc = o - mu
            var = jnp.mean(oc * oc, axis=-1, keepdims=True)
            y = oc * lax.rsqrt(var + LN_EPS) * gate_ref[rows, cols[hd]]
            mix_ref[rows, cols[hd]] = y.astype(bf16)

    return new_states, finish


def _att_update(m_ref, acc_ref, hd, lane0, s, v_t, ok):
    if ok is not None:
        s = jnp.where(ok, s, -jnp.inf)
    m = m_ref[hd, :, lane0:]
    m_new = jnp.maximum(m, jnp.max(s, axis=0, keepdims=True))
    p = jnp.exp2(s - m_new)
    m_ref[hd, :, lane0:] = m_new
    acc_ref[hd, :, lane0:] = jnp.exp2(m - m_new) * acc_ref[hd, :, lane0:] + _dot(v_t, p.astype(bf16))


def _mixer_kernel(qt_ref, km_ref, vt_ref, kmeta_ref, vmeta_ref, qr_ref, kr_ref, vr_ref, gate_ref,
                  krmeta_ref, vrmeta_ref, w_f32_ref, mix_ref, state_ref, w_bf16_ref,
                  m_ref, acc_ref, s_ref, dmask_ref, qdec_ref, kdec_ref):
    qi = pl.program_id(1)

    w_bf16_ref[...] = w_f32_ref[...].astype(bf16)

    @pl.when((pl.program_id(0) == 0) & (qi == 0))
    def _():
        for hd in range(RET_HEADS):
            dmask_ref[hd], qdec_ref[hd], kdec_ref[hd] = _retention_tables(RET_C, hd)

    @pl.when(qi == 0)
    def _():
        for hd in range(RET_HEADS):
            lo = hd * RET_DK
            lg = RET_LOG_GAMMA[hd]
            kd = (krmeta_ref[:, lo:lo + RET_DK].astype(f32)
                  * _row_pow(N_META, lg, N_META - 1.0, -1.0)).astype(bf16)
            state_ref[0, hd] = _dot_tn(kd, vrmeta_ref[:, lo:lo + RET_DV])

    heads = range(MLA_HEADS)
    units = [(kb, hd) for kb in range(ATT_TK // ATT_KB) for hd in heads]

    def scores(start, kb, hd, lane0=0):
        k = km_ref[0, pl.ds(start + kb * ATT_KB, ATT_KB), hd * MLA_QK_PAD:(hd + 1) * MLA_QK_PAD]
        return _dot(k, qt_ref[0, hd * MLA_QK_PAD:(hd + 1) * MLA_QK_PAD, lane0:])

    def v_block(kt, kb, hd):
        return vt_ref[kt, hd * MLA_VT_ROWS:(hd + 1) * MLA_VT_ROWS, kb * ATT_KB:(kb + 1) * ATT_KB]

    first_scores = [functools.partial(scores, 0, *units[d]) for d in range(ATT_AHEAD)]

    def trace_first_scores(n):
        for _ in range(min(n, len(first_scores))):
            d = ATT_AHEAD - len(first_scores)
            s_ref[d] = first_scores.pop(0)()

    tables = [(dmask_ref.at[hd], qdec_ref.at[hd], kdec_ref.at[hd]) for hd in range(RET_HEADS)]
    states = [state_ref[0, hd] for hd in range(RET_HEADS)]
    finishes = []
    for c in range(ATT_TQ // RET_C):
        states, finish_chunk = _retention_chunk(qr_ref, kr_ref, vr_ref, gate_ref, mix_ref,
                                                slice(c * RET_C, (c + 1) * RET_C), states, tables)
        finishes.append(finish_chunk)
    for hd in range(RET_HEADS):
        state_ref[0, hd] = states[hd]
    per_stage = -(-ATT_AHEAD // len(finishes))
    for finish_chunk in finishes:
        trace_first_scores(per_stage)
        finish_chunk()
    trace_first_scores(ATT_AHEAD)
    s_meta = [_dot(kmeta_ref[:, hd * MLA_QK_PAD:(hd + 1) * MLA_QK_PAD],
                   qt_ref[0, hd * MLA_QK_PAD:(hd + 1) * MLA_QK_PAD, :]) for hd in heads]

    p_meta = []
    for hd in heads:
        m = jnp.max(s_meta[hd], axis=0, keepdims=True)
        m_ref[hd] = m
        p_meta.append(jnp.exp2(s_meta[hd] - m).astype(bf16))
    ones = jnp.ones((N_META, MLA_VT_ROWS - MLA_V), bf16)
    for hd in heads:
        v_ext = jnp.concatenate([vmeta_ref[:, hd * MLA_V:(hd + 1) * MLA_V], ones], axis=1)
        acc_ref[hd] = _dot_tn(v_ext, p_meta[hd])

    def full_step(kt, _):
        start = pl.multiple_of(kt * ATT_TK, ATT_TK)
        pending = [s_ref[d] for d in range(ATT_AHEAD)]
        for u, (kb, hd) in enumerate(units):
            ahead = u + ATT_AHEAD
            if ahead < len(units):
                pending.append(scores(start, *units[ahead]))
            else:
                pending.append(scores(pl.multiple_of(start + ATT_TK, ATT_TK), *units[ahead - len(units)]))
            _att_update(m_ref, acc_ref, hd, 0, pending.pop(0), v_block(kt, kb, hd), None)
        for d in range(ATT_AHEAD):
            s_ref[d] = pending[d]
        return 0

    lax.fori_loop(0, qi, full_step, 0)

    start = pl.multiple_of(qi * ATT_TQ, ATT_TQ)
    pending = [s_ref[d] for d in range(ATT_AHEAD)]
    visible = {}
    for u, (kb, hd) in enumerate(units):
        lane0 = kb * ATT_KB
        if u + ATT_AHEAD < len(units):
            nxt = units[u + ATT_AHEAD]
            pending.append(scores(start, *nxt, lane0=nxt[0] * ATT_KB))
        s_cur = pending.pop(0)
        if s_cur.shape not in visible:
            k_chunk = lax.broadcasted_iota(jnp.int32, (s_cur.shape[0], 1), 0) // CHUNK
            q_chunk = lax.broadcasted_iota(jnp.int32, (1, s_cur.shape[1]), 1) // CHUNK
            visible[s_cur.shape] = k_chunk <= q_chunk
        _att_update(m_ref, acc_ref, hd, lane0, s_cur, v_block(qi, kb, hd), visible[s_cur.shape])
    for hd in heads:
        out = (acc_ref[hd, :MLA_V, :] / acc_ref[hd, MLA_V:MLA_V + 1, :]).T
        mix_ref[:, RET_W + hd * MLA_V:RET_W + (hd + 1) * MLA_V] = out.astype(bf16)


def _mixer_call(qt, km, vt, kmeta, vmeta, qr, kr, vr, gate, krmeta, vrmeta, w_f32, batch, seq):
    nq = seq // ATT_TQ
    tile = lambda w: pl.BlockSpec((ATT_TQ, w), lambda b, i: (b * nq + i, 0))
    const = lambda a: pl.BlockSpec((N_META, a.shape[1]), lambda b, i: (a.shape[0] // N_META - 1, 0))
    w_rows, rem = divmod(w_f32.shape[0], batch * nq)
    assert rem == 0 and w_rows % BF16_SUBLANES == 0
    w_slice = pl.BlockSpec((w_rows, w_f32.shape[1]), lambda b, i: (b * nq + i, 0))
    return pl.pallas_call(
        _mixer_kernel,
        grid=(batch, nq),
        in_specs=[pl.BlockSpec((1, MLA_QW, ATT_TQ), lambda b, i: (b * nq + i, 0, 0)),
                  pl.BlockSpec((1, seq, MLA_QW), lambda b, i: (b, 0, 0)),
                  pl.BlockSpec((seq // ATT_TK, MLA_HEADS * MLA_VT_ROWS, ATT_TK), lambda b, i: (b, 0, 0)),
                  const(kmeta), const(vmeta),
                  tile(RET_W), tile(RET_W), tile(RET_W), tile(RET_W),
                  const(krmeta), const(vrmeta), w_slice],
        out_specs=[tile(D_MODEL),
                   pl.BlockSpec((1, RET_HEADS, RET_DK, RET_DV), lambda b, i: (b, 0, 0, 0)), w_slice],
        out_shape=[jax.ShapeDtypeStruct((batch * seq, D_MODEL), bf16),
                   jax.ShapeDtypeStruct((batch, RET_HEADS, RET_DK, RET_DV), f32),
                   jax.ShapeDtypeStruct(w_f32.shape, bf16)],
        scratch_shapes=[pltpu.VMEM((MLA_HEADS, 1, ATT_TQ), f32),
                        pltpu.VMEM((MLA_HEADS, MLA_VT_ROWS, ATT_TQ), f32),
                        pltpu.VMEM((ATT_AHEAD, ATT_KB, ATT_TQ), f32),
                        pltpu.VMEM((RET_HEADS, RET_C, RET_C), f32),
                        pltpu.VMEM((RET_HEADS, RET_C, LANES), f32),
                        pltpu.VMEM((RET_HEADS, RET_C, LANES), f32)],
        compiler_params=pltpu.CompilerParams(dimension_semantics=("arbitrary", "arbitrary"),
                                             vmem_limit_bytes=VMEM_LIMIT),
        name="mixer",
    )(qt, km.reshape(batch, seq, MLA_QW), vt, kmeta, vmeta, qr, kr, vr, gate, krmeta, vrmeta, w_f32)


def _smixer_kernel(qm_ref, cnew_ref, pnew_ref, cmeta_ref, pmeta_ref, ccache_ref, pcache_ref, w_qk_ref, w_ov_ref,
                   qr_ref, kr_ref, vr_ref, gate_ref, s0_ref, wa_f32_ref, wb_f32_ref,
                   mix_ref, state_ref, wa_bf16_ref, wb_bf16_ref):
    wa_bf16_ref[...] = wa_f32_ref[...].astype(bf16)
    wb_bf16_ref[...] = wb_f32_ref[...].astype(bf16)

    t = qr_ref.shape[0]
    states, finish_retention = _retention_chunk(qr_ref, kr_ref, vr_ref, gate_ref, mix_ref, slice(0, t),
                                                [s0_ref[0, hd] for hd in range(RET_HEADS)],
                                                [_retention_tables(t, hd) for hd in range(RET_HEADS)])
    for hd in range(RET_HEADS):
        state_ref[0, hd] = states[hd]

    heads = range(MLA_HEADS)
    q_nope = jnp.concatenate([qm_ref[:, hd * MLA_QK_PAD:hd * MLA_QK_PAD + MLA_NOPE] for hd in heads], axis=1)
    q_abs = _dot(q_nope, w_qk_ref[...])
    finish_retention()
    q_lat = jnp.concatenate([q_abs[:, hd * MLA_KV_LORA:(hd + 1) * MLA_KV_LORA] for hd in heads], axis=0).astype(bf16)
    q_rope = jnp.concatenate([qm_ref[:, hd * MLA_QK_PAD + MLA_NOPE:hd * MLA_QK_PAD + MLA_NOPE + MLA_ROPE]
                              for hd in heads], axis=0)
    lat = [cmeta_ref[...].astype(bf16), ccache_ref[0].astype(bf16), cnew_ref[...].astype(bf16)]
    s_rope = [_dot_nt(q_rope, pmeta_ref[...].astype(bf16)), _dot(q_rope, pcache_ref[0].astype(bf16)),
              _dot_nt(q_rope, pnew_ref[...].astype(bf16))]
    scores = [(_dot_nt(q_lat, c) + sr) * MLA_SCALE for c, sr in zip(lat, s_rope)]
    m = functools.reduce(jnp.maximum, [jnp.max(s, axis=-1, keepdims=True) for s in scores])
    probs = [jnp.exp(s - m) for s in scores]
    denom = functools.reduce(jnp.add, [jnp.sum(p, axis=-1, keepdims=True) for p in probs])
    o_lat = functools.reduce(jnp.add, [_dot(p.astype(bf16), c) for p, c in zip(probs, lat)]) / denom
    o_all = jnp.concatenate([o_lat[hd * t:(hd + 1) * t, :] for hd in heads], axis=1).astype(bf16)
    mix_ref[:, RET_W:] = _dot(o_all, w_ov_ref[...]).astype(bf16)


def _smixer_call(qm, ckv, kpe, cckv, ckpe, w_qk, w_ov, qr, kr, vr, gate, s0, wa_f32, wb_f32, t):
    db, past = cckv.shape[0], cckv.shape[1]
    tile = lambda w: pl.BlockSpec((t, w), lambda b: (b, 0))
    meta = lambda w: pl.BlockSpec((N_META, w), lambda b: (db * t // N_META, 0))
    const = lambda a: pl.BlockSpec(a.shape, lambda b: (0,) * a.ndim)
    per = lambda *tail: pl.BlockSpec((1,) + tail, lambda b: (b,) + (0,) * len(tail))

    def w_slice(w):
        rows, rem = divmod(w.shape[0], db)
        assert rem == 0 and rows % BF16_SUBLANES == 0
        return pl.BlockSpec((rows, w.shape[1]), lambda b: (b, 0))

    return pl.pallas_call(
        _smixer_kernel,
        grid=(db,),
        in_specs=[tile(MLA_QW), tile(MLA_KV_LORA), tile(MLA_ROPE), meta(MLA_KV_LORA), meta(MLA_ROPE),
                  per(past, MLA_KV_LORA), per(MLA_ROPE, past), const(w_qk), const(w_ov),
                  tile(RET_W), tile(RET_W), tile(RET_W), tile(RET_W),
                  per(RET_HEADS, RET_DK, RET_DV), w_slice(wa_f32), w_slice(wb_f32)],
        out_specs=[tile(D_MODEL), per(RET_HEADS, RET_DK, RET_DV), w_slice(wa_f32), w_slice(wb_f32)],
        out_shape=[jax.ShapeDtypeStruct((db * t, D_MODEL), bf16),
                   jax.ShapeDtypeStruct((db, RET_HEADS, RET_DK, RET_DV), f32),
                   jax.ShapeDtypeStruct(wa_f32.shape, bf16), jax.ShapeDtypeStruct(wb_f32.shape, bf16)],
        compiler_params=pltpu.CompilerParams(dimension_semantics=("arbitrary",),
                                             vmem_limit_bytes=VMEM_LIMIT),
        name="smixer",
    )(qm, ckv, kpe, ckv, kpe, cckv, ckpe, w_qk, w_ov, qr, kr, vr, gate, s0, wa_f32, wb_f32)


def _tail_kernel(sub, mix_ref, h_ref, w_mo_ref, ln2_g_ref, ln2_b_ref, w_in_ref, w_out_ref, ln3_g_ref, ln3_b_ref,
                 y_ref):
    tiles = [slice(j * sub, (j + 1) * sub) for j in range(mix_ref.shape[0] // sub)]
    mixes = [_dot(mix_ref[r, :], w_mo_ref[...]) for r in tiles]
    h2s, ys = [], []
    for r, mix in zip(tiles, mixes):
        h2 = _layer_norm(DN_ALPHA * h_ref[r, :] + mix, ln2_g_ref[...], ln2_b_ref[...])
        h2s.append(h2)
        ys.append(_swiglu(h2, w_in_ref.at[:, :D_FF], w_in_ref.at[:, D_FF:], w_out_ref))
    for r, h2, y in zip(tiles, h2s, ys):
        y_ref[r, :] = _ffn_norm(h2, y, ln3_g_ref, ln3_b_ref)


def _tail_call(mix, h, wts, tm, sub):
    n = mix.shape[0]
    row = lambda w: pl.BlockSpec((tm, w), lambda i: (i, 0))
    return pl.pallas_call(
        functools.partial(_tail_kernel, sub),
        grid=(n // tm,),
        in_specs=[row(D_MODEL), row(D_MODEL)] + [_const_spec(w.shape) for w in wts],
        out_specs=row(D_MODEL),
        out_shape=jax.ShapeDtypeStruct((n, D_MODEL), f32),
        compiler_params=pltpu.CompilerParams(dimension_semantics=("arbitrary",),
                                             vmem_limit_bytes=VMEM_LIMIT),
        name="tail",
    )(mix, h, *wts)


def _rope_tables(pos):
    posf = pos.astype(f32)[:, None]
    inv_r = ROPE_BASE ** (-jnp.arange(0, RET_DK, 2, dtype=f32) / RET_DK)
    ang = posf * inv_r[None, :]
    inv_m = ROPE_BASE ** (-jnp.arange(0, MLA_ROPE, 2, dtype=f32) / MLA_ROPE)
    angm = posf * inv_m[None, :]
    return jnp.cos(ang), jnp.sin(ang), jnp.cos(angm), jnp.sin(angm)


def kernel(x_prompt, x_sample, cache_mla_ckv, cache_mla_kpe, state_ret, meta_tokens,
           ffn1_w_in, ffn1_w_out, ln1_g, ln1_b, w_mix_in, ret_gn_g, mla_q_norm_g, mla_w_uq,
           mla_kv_norm_g, mla_w_ukv, w_mix_out, ln2_g, ln2_b, ffn2_w_in, ffn2_w_out, ln3_g, ln3_b):
    B, S, D = x_prompt.shape
    DB, T, _ = x_sample.shape
    P = cache_mla_ckv.shape[1]
    assert S % ATT_TQ == 0 and S % FRONT_TM == 0 and (B * S) % TAIL_TM == 0 and ATT_TQ % RET_C == 0
    row = lambda a: a.reshape(1, -1).astype(f32)

    w_uq = mla_w_uq.reshape(MLA_Q_LORA, MLA_HEADS, MLA_NOPE + MLA_ROPE)
    w_uq = jnp.pad(w_uq, ((0, 0), (0, 0), (0, MLA_QK_PAD - MLA_NOPE - MLA_ROPE)))
    w_uq = w_uq.reshape(MLA_Q_LORA, MLA_QW).astype(bf16)
    w_ukv = mla_w_ukv.reshape(MLA_KV_LORA, MLA_HEADS, MLA_NOPE + MLA_V)
    w_uk = w_ukv[:, :, :MLA_NOPE].reshape(MLA_KV_LORA, -1).astype(bf16)
    w_uv = w_ukv[:, :, MLA_NOPE:].reshape(MLA_KV_LORA, -1).astype(bf16)
    shared_w = lambda w_mix: [row(ln1_g), row(ln1_b), w_mix, row(ret_gn_g), row(mla_q_norm_g)]

    n_s = DB * T
    x_small = jnp.concatenate([x_sample.reshape(n_s, D), meta_tokens.astype(x_sample.dtype)], axis=0)
    pos_small = jnp.concatenate([jnp.tile(N_META + P + jnp.arange(T), DB), jnp.arange(N_META)])
    (h_s, qr_s, kr_s, vr_s, gr_s, qm_s, km_s, vm_s, ckv_s, kpe_s, w1_gate, w1_up, w1_out, w_mix) = _front_small_call(
        x_small, _rope_tables(pos_small), jnp.concatenate(_rope_tables(jnp.zeros((1,), jnp.int32)), axis=1)[:, None, :],
        ffn1_w_in, ffn1_w_out, shared_w(w_mix_in.T) + [w_uq, row(mla_kv_norm_g), w_uk, w_uv])

    tabs_p = _rope_tables(N_META + jnp.arange(FRONT_TM))
    steps_p = _rope_tables(FRONT_TM * jnp.arange(S // FRONT_TM))
    tabs_t_p = (jnp.concatenate([tabs_p[2].T, tabs_p[3].T], axis=0),
                jnp.concatenate([steps_p[2], steps_p[3]], axis=1)[:, :, None])
    (h_p, qr_p, kr_p, vr_p, gr_p, qt_p, km_p, vt_p, ckv_p, kpe_p) = _front_call(
        x_prompt.reshape(B * S, D), tabs_p, jnp.concatenate(steps_p, axis=1)[:, None, :], tabs_t_p,
        [w1_gate, w1_up, w1_out] + shared_w(w_mix) + [w_uq.T, row(mla_kv_norm_g), w_uk, w_uv.T], FRONT_TM, FRONT_SUB)

    mix_p, p_state, w2_in = _mixer_call(qt_p, km_p, vt_p, km_s, vm_s, qr_p, kr_p, vr_p, gr_p, kr_s, vr_s,
                                        ffn2_w_in, B, S)
    blocks = lambda w, n: [w[:, hd * n:(hd + 1) * n] for hd in range(MLA_HEADS)]
    w_qk = jax.scipy.linalg.block_diag(*[blk.T for blk in blocks(w_uk, MLA_NOPE)])
    w_ov = jax.scipy.linalg.block_diag(*blocks(w_uv, MLA_V))
    mix_s, s_state, w2_out, w_mo = _smixer_call(qm_s, ckv_s, kpe_s, cache_mla_ckv,
                                                jnp.transpose(cache_mla_kpe, (0, 2, 1)), w_qk, w_ov,
                                                qr_s, kr_s, vr_s, gr_s, state_ret, ffn2_w_out, w_mix_out, T)

    tail_w = [w_mo, row(ln2_g), row(ln2_b), w2_in, w2_out, row(ln3_g), row(ln3_b)]
    y_p = _tail_call(mix_p, h_p, tail_w, TAIL_TM, FRONT_SUB)
    y_p, mix_s = lax.optimization_barrier((y_p, mix_s))
    y_s = _tail_call(mix_s, h_s, tail_w, n_s, n_s // 2)

    meta_ckv = jnp.broadcast_to(ckv_s[n_s:][None], (B, N_META, MLA_KV_LORA))
    meta_kpe_t = jnp.broadcast_to(kpe_s[n_s:].T[None], (B, MLA_ROPE, N_META))
    p_ckv = jnp.concatenate([meta_ckv, ckv_p.reshape(B, S, MLA_KV_LORA)], axis=1)
    p_kpe = jnp.transpose(jnp.concatenate([meta_kpe_t, kpe_p], axis=2), (0, 2, 1))
    return (y_p.reshape(B, S, D), y_s.reshape(DB, T, D), p_ckv, p_kpe, p_state.astype(x_prompt.dtype),
            ckv_s[:n_s].reshape(DB, T, MLA_KV_LORA), kpe_s[:n_s].reshape(DB, T, MLA_ROPE),
            s_state.astype(state_ret.dtype))
```

```python
import functools
import math

import jax
import jax.numpy as jnp
from jax import lax
from jax.experimental import pallas as pl
from jax.experimental.pallas import tpu as pltpu

D_MODEL = 1024
DEPTH = 1
CHUNK = 64
N_META = 16
RET_HEADS = 4
RET_DK = 128
RET_DV = 128
MLA_HEADS = 4
MLA_NOPE = 128
MLA_ROPE = 64
MLA_V = 128
MLA_Q_LORA = 256
MLA_KV_LORA = 128
D_FF = 2816
ROPE_BASE = 10000.0
LN_EPS = 1e-5
RMS_EPS = 1e-6
DN_ALPHA = (2 * DEPTH) ** 0.25

OFF_KR = RET_HEADS * RET_DK
OFF_VR = 2 * RET_HEADS * RET_DK
OFF_GR = OFF_VR + RET_HEADS * RET_DV
OFF_CQ = OFF_GR + RET_HEADS * RET_DV
OFF_CKV = OFF_CQ + MLA_Q_LORA
OFF_KPE = OFF_CKV + MLA_KV_LORA
D_IN = OFF_KPE + MLA_ROPE

LANES = 128
MLA_QK_PAD = 2 * LANES
RET_W = RET_HEADS * RET_DK
MLA_QW = MLA_HEADS * MLA_QK_PAD
MLA_VW = MLA_HEADS * MLA_V
BF16_SUBLANES = 16
MLA_VT_ROWS = MLA_V + BF16_SUBLANES

FRONT_SUB = 256
FRONT_TM = 2 * FRONT_SUB
TAIL_TM = 4 * FRONT_SUB
FF_CHUNK = 256
W_IN_ROWS = 128
ATT_TQ = FRONT_TM
ATT_TK = FRONT_TM
ATT_KB = 512
ATT_AHEAD = 4
RET_C = 256
VMEM_LIMIT = 56 * 1024 * 1024

MLA_SCALE = (MLA_NOPE + MLA_ROPE) ** -0.5
MLA_SCALE_LOG2E = MLA_SCALE * math.log2(math.e)
RET_LOG_GAMMA = tuple(math.log(1.0 - 2.0 ** (-5.0 - h)) for h in range(RET_HEADS))

f32 = jnp.float32
bf16 = jnp.bfloat16


def _dot(a, b):
    return jnp.dot(a, b, preferred_element_type=f32)


def _dot_nt(a, b):
    return lax.dot_general(a, b, (((1,), (1,)), ((), ())), preferred_element_type=f32)


def _dot_tn(a, b):
    return lax.dot_general(a, b, (((0,), (0,)), ((), ())), preferred_element_type=f32)


def _layer_norm(x, g, b):
    mu = jnp.mean(x, axis=-1, keepdims=True)
    xc = x - mu
    var = jnp.mean(xc * xc, axis=-1, keepdims=True)
    return xc * lax.rsqrt(var + LN_EPS) * g + b


def _rms_norm(x, g):
    return x * lax.rsqrt(jnp.mean(x * x, axis=-1, keepdims=True) + RMS_EPS) * g


def _silu(x):
    return x / (1.0 + jnp.exp(-x))


def _swiglu(x, w_gate, w_up, w_out):
    xb = x.astype(bf16)
    hg = _dot(xb, w_gate[...])
    hu = _dot(xb, w_up[...])
    return _dot((_silu(hg) * hu).astype(bf16), w_out[...])


def _ffn_norm(x, y, g_ref, b_ref):
    return _layer_norm(DN_ALPHA * x + 0.5 * y, g_ref[...], b_ref[...])


def _rotate(cb, sb, ca, sa):
    return cb * ca - sb * sa, sb * ca + cb * sa


def _front_kernel(sub, x_ref, cr_ref, sr_ref, cm_ref, sm_ref, step_ref, tab_t_ref, step_t_ref,
                  w_gate_ref, w_up_ref, w_out_ref, *refs):
    n_sub = x_ref.shape[0] // sub
    tiles = [slice(j * sub, (j + 1) * sub) for j in range(n_sub)]
    xs = [x_ref[r, :] for r in tiles]
    ys = [_swiglu(x, w_gate_ref, w_up_ref, w_out_ref) for x in xs]
    _front_rest((tab_t_ref, step_t_ref), tiles, xs, ys, (cr_ref, sr_ref, cm_ref, sm_ref, step_ref), refs)


def _front_rest(tabs_t, tiles, xs, ys, tabs, refs):
    cr_ref, sr_ref, cm_ref, sm_ref, step_ref = tabs
    transposed = tabs_t is not None
    if transposed:
        tab_t_ref, step_t_ref = tabs_t
    (ln_g_ref, ln_b_ref, w_mix_ref, gn_ref, qn_g_ref, w_uq_ref, kvn_g_ref, w_uk_ref, w_uv_ref,
     h_ref, qr_ref, kr_ref, vr_ref, gate_ref, qm_ref, km_ref, vm_ref, ckv_ref, kpe_ref) = refs
    ps = []
    for r, x, y in zip(tiles, xs, ys):
        h = _ffn_norm(x, y, ln_g_ref, ln_b_ref)
        h_ref[r, :] = h
        hb = h.astype(bf16)
        p_kpe = _dot_nt(hb, w_mix_ref[OFF_KPE:, :])
        ps.append((_dot_nt(hb, w_mix_ref[:OFF_KPE, :]),
                   jnp.concatenate([p_kpe, jnp.zeros_like(p_kpe)], axis=1)))

    for j, (r, (p, p_kpe)) in enumerate(zip(tiles, ps)):
        half_r, half_m = RET_DK // 2, MLA_ROPE // 2
        c, s = _rotate(cr_ref[r, :], sr_ref[r, :], step_ref[0, :, :half_r], step_ref[0, :, half_r:2 * half_r])
        cm, sm = _rotate(cm_ref[r, :], sm_ref[r, :], step_ref[0, :, 2 * half_r:2 * half_r + half_m],
                         step_ref[0, :, 2 * half_r + half_m:])
        z_q, z_h = jnp.zeros_like(cm), jnp.zeros_like(c)
        c_r = jnp.concatenate([c, c], axis=1)
        s_r = jnp.concatenate([-s, s], axis=1)
        c_m = jnp.concatenate([cm, cm, z_h], axis=1)
        s_ma = jnp.concatenate([-sm, z_q, z_h], axis=1)
        s_mb = jnp.concatenate([z_q, sm, z_h], axis=1)

        def rope_ret(xh):
            return xh * c_r + pltpu.roll(xh, 64, 1) * s_r

        def rope_mla(xh):
            return xh * c_m + pltpu.roll(xh, 96, 1) * s_ma + pltpu.roll(xh, 32, 1) * s_mb

        cq = _rms_norm(p[:, OFF_CQ:OFF_CKV], qn_g_ref[...]).astype(bf16)
        ckv = _rms_norm(p[:, OFF_CKV:OFF_KPE], kvn_g_ref[...])
        ckv_ref[r, :] = ckv
        ckv_b = ckv.astype(bf16)
        k_nope = _dot(ckv_b, w_uk_ref[...])
        if transposed:
            q_t = _dot_nt(w_uq_ref[...], cq) * MLA_SCALE_LOG2E
            v_t = _dot_nt(w_uv_ref[...], ckv_b)
        else:
            q = _dot(cq, w_uq_ref[...])
            v = _dot(ckv_b, w_uv_ref[...])

        for hd in range(RET_HEADS):
            lo = hd * RET_DK
            qr_ref[r, lo:lo + RET_DK] = rope_ret(p[:, lo:lo + RET_DK]).astype(bf16)
            kr_ref[r, lo:lo + RET_DK] = (rope_ret(p[:, OFF_KR + lo:OFF_KR + lo + RET_DK])
                                         * RET_DK ** -0.5).astype(bf16)
        vr_ref[r, :] = p[:, OFF_VR:OFF_GR].astype(bf16)
        gate_ref[r, :] = gn_ref[...] * _silu(p[:, OFF_GR:OFF_CQ])
        kpe = rope_mla(p_kpe)
        if transposed:
            kpe_ref[0, :, r] = kpe.T[:MLA_ROPE, :]
        else:
            kpe_ref[r, :] = kpe[:, :MLA_ROPE]
        kpe_b = kpe.astype(bf16)
        for hd in range(MLA_HEADS):
            lo = hd * MLA_QK_PAD
            km_ref[r, lo:lo + LANES] = k_nope[:, hd * MLA_NOPE:(hd + 1) * MLA_NOPE].astype(bf16)
            km_ref[r, lo + LANES:lo + 2 * LANES] = kpe_b

        if transposed:
            half = MLA_ROPE // 2
            c_t, s_t = _rotate(tab_t_ref[:half, r], tab_t_ref[half:, r], step_t_ref[0, :half, :], step_t_ref[0, half:, :])
            for hd in range(MLA_HEADS):
                lo = hd * MLA_QK_PAD
                r0 = lo + MLA_NOPE
                x1 = q_t[r0:r0 + half, :]
                x2 = q_t[r0 + half:r0 + 2 * half, :]
                qm_ref[0, lo:r0, r] = q_t[lo:r0, :].astype(bf16)
                qm_ref[0, r0:r0 + half, r] = (x1 * c_t - x2 * s_t).astype(bf16)
                qm_ref[0, r0 + half:r0 + 2 * half, r] = (x1 * s_t + x2 * c_t).astype(bf16)
                qm_ref[0, r0 + 2 * half:lo + MLA_QK_PAD, r] = q_t[r0 + 2 * half:lo + MLA_QK_PAD, :].astype(bf16)
            ones = jnp.ones((MLA_VT_ROWS - MLA_V, v_t.shape[1]), bf16)
            for hd in range(MLA_HEADS):
                vm_ref[0, hd * MLA_VT_ROWS:hd * MLA_VT_ROWS + MLA_V, r] = v_t[hd * MLA_V:(hd + 1) * MLA_V, :].astype(bf16)
                vm_ref[0, hd * MLA_VT_ROWS + MLA_V:(hd + 1) * MLA_VT_ROWS, r] = ones
        else:
            for hd in range(MLA_HEADS):
                lo = hd * MLA_QK_PAD
                qm_ref[r, lo:lo + LANES] = q[:, lo:lo + LANES].astype(bf16)
                qm_ref[r, lo + LANES:lo + 2 * LANES] = rope_mla(q[:, lo + LANES:lo + 2 * LANES]).astype(bf16)
            vm_ref[r, :] = v.astype(bf16)


def _front_small_kernel(n_r, x_ref, cr_ref, sr_ref, cm_ref, sm_ref, step_ref, win_f32_ref, wo_f32_ref, *refs):
    *rest, wg_ref, wu_ref, wo_ref, wmix_ref, y_ref, wg_vmem, wu_vmem = refs
    i = pl.program_id(0)
    n_c = wg_vmem.shape[0]
    blk = win_f32_ref.shape[0]

    @pl.when(i == 0)
    def _():
        wmix_ref[...] = rest[2][...].astype(bf16)

    @pl.when(i < n_r)
    def _():
        g = win_f32_ref[:, :D_FF].astype(bf16)
        u = win_f32_ref[:, D_FF:].astype(bf16)
        wg_ref[...] = g
        wu_ref[...] = u
        row0 = pl.multiple_of(i * blk, blk)
        for c in range(n_c):
            wg_vmem[c, pl.ds(row0, blk), :] = g[:, c * FF_CHUNK:(c + 1) * FF_CHUNK]
            wu_vmem[c, pl.ds(row0, blk), :] = u[:, c * FF_CHUNK:(c + 1) * FF_CHUNK]

    @pl.when(i >= n_r)
    def _():
        c = i - n_r
        wo = wo_f32_ref[...].astype(bf16)
        wo_ref[...] = wo
        part = _swiglu(x_ref[...], wg_vmem[c], wu_vmem[c], wo)

        @pl.when(c == 0)
        def _():
            y_ref[...] = part

        @pl.when(c > 0)
        def _():
            y_ref[...] += part

    @pl.when(i == pl.num_programs(0) - 1)
    def _():
        x = x_ref[...]
        _front_rest(None, [slice(0, x.shape[0])], [x], [y_ref[...]], (cr_ref, sr_ref, cm_ref, sm_ref, step_ref),
                    rest[:2] + [wmix_ref] + rest[3:])


def _const_spec(shape):
    nd = len(shape)
    return pl.BlockSpec(shape, lambda *_: (0,) * nd, pipeline_mode=pl.Buffered(1))


def _front_small_call(x, tabs, steps, w_in, w_out, wts):
    n = x.shape[0]
    n_c = D_FF // FF_CHUNK
    n_r = D_MODEL // W_IN_ROWS
    full = lambda a: pl.BlockSpec(a.shape, lambda i: (0,) * a.ndim)
    in_rows = lambda w: pl.BlockSpec((W_IN_ROWS, w), lambda i: (jnp.minimum(i, n_r - 1), 0))
    out_rows = pl.BlockSpec((FF_CHUNK, D_MODEL), lambda i: (jnp.maximum(i - n_r, 0), 0))
    widths = [(D_MODEL, f32), (RET_W, bf16), (RET_W, bf16), (RET_W, bf16), (RET_W, f32), (MLA_QW, bf16),
              (MLA_QW, bf16), (MLA_VW, bf16), (MLA_KV_LORA, f32), (MLA_ROPE, f32)]
    return pl.pallas_call(
        functools.partial(_front_small_kernel, n_r),
        grid=(n_r + n_c,),
        in_specs=([full(x)] + [full(t) for t in tabs] + [full(steps), in_rows(2 * D_FF), out_rows]
                  + [_const_spec(w.shape) for w in wts]),
        out_specs=([pl.BlockSpec((n, w), lambda i: (0, 0)) for w, _ in widths]
                   + [in_rows(D_FF), in_rows(D_FF), out_rows, full(wts[2])]),
        out_shape=([jax.ShapeDtypeStruct((n, w), dt) for w, dt in widths]
                   + [jax.ShapeDtypeStruct((D_MODEL, D_FF), bf16), jax.ShapeDtypeStruct((D_MODEL, D_FF), bf16),
                      jax.ShapeDtypeStruct((D_FF, D_MODEL), bf16), jax.ShapeDtypeStruct(wts[2].shape, bf16)]),
        scratch_shapes=[pltpu.VMEM((n, D_MODEL), f32),
                        pltpu.VMEM((n_c, D_MODEL, FF_CHUNK), bf16),
                        pltpu.VMEM((n_c, D_MODEL, FF_CHUNK), bf16)],
        compiler_params=pltpu.CompilerParams(dimension_semantics=("arbitrary",),
                                             vmem_limit_bytes=VMEM_LIMIT),
        name="front",
    )(x, *tabs, steps, w_in, w_out, *wts)


def _front_call(x, tabs, steps, tabs_t, wts, tm, sub):
    n = x.shape[0]
    period = steps.shape[0]
    row = lambda w: pl.BlockSpec((tm, w), lambda i: (i, 0))
    col = lambda w: pl.BlockSpec((1, w, tm), lambda i: (i, 0, 0))
    in_specs = [row(D_MODEL)] + [pl.BlockSpec((tm, t.shape[1]), lambda i: (0, 0)) for t in tabs]
    in_specs.append(pl.BlockSpec((1,) + steps.shape[1:], lambda i: (i % period, 0, 0)))
    in_specs.append(pl.BlockSpec((MLA_ROPE, tm), lambda i: (0, 0)))
    in_specs.append(pl.BlockSpec((1,) + tabs_t[1].shape[1:], lambda i: (i % period, 0, 0)))
    in_specs += [_const_spec(w.shape) for w in wts]
    rows = lambda w, dt: (row(w), jax.ShapeDtypeStruct((n, w), dt))
    cols = lambda w, dt: (col(w), jax.ShapeDtypeStruct((n // tm, w, tm), dt))
    outs = [rows(D_MODEL, f32), rows(RET_W, bf16), rows(RET_W, bf16), rows(RET_W, bf16), rows(RET_W, f32),
            cols(MLA_QW, bf16), rows(MLA_QW, bf16), cols(MLA_HEADS * MLA_VT_ROWS, bf16),
            rows(MLA_KV_LORA, f32),
            (pl.BlockSpec((1, MLA_ROPE, tm), lambda i: (i // period, 0, i % period)),
             jax.ShapeDtypeStruct((n // (period * tm), MLA_ROPE, period * tm), f32))]
    return pl.pallas_call(
        functools.partial(_front_kernel, sub),
        grid=(n // tm,),
        in_specs=in_specs,
        out_specs=[o[0] for o in outs],
        out_shape=[o[1] for o in outs],
        compiler_params=pltpu.CompilerParams(dimension_semantics=("arbitrary",),
                                             vmem_limit_bytes=VMEM_LIMIT),
        name="front_t",
    )(x, *tabs, steps, *tabs_t, *wts)


def _decay_mask(c, lg):
    r = lax.broadcasted_iota(jnp.int32, (c, c), 0)
    k = lax.broadcasted_iota(jnp.int32, (c, c), 1)
    rel = (r - k).astype(f32)
    return jnp.where(rel >= 0.0, jnp.exp(lg * jnp.maximum(rel, 0.0)), 0.0)


def _row_pow(c, lg, offset, sign):
    i = lax.broadcasted_iota(jnp.int32, (c, LANES), 0).astype(f32)
    return jnp.exp(lg * (offset + sign * i))


def _retention_tables(c, hd):
    lg = RET_LOG_GAMMA[hd]
    return _decay_mask(c, lg), _row_pow(c, lg, 1.0, 1.0), _row_pow(c, lg, c - 1.0, -1.0)


def _retention_chunk(qr_ref, kr_ref, vr_ref, gate_ref, mix_ref, rows, states, tables):
    c = rows.stop - rows.start
    heads = range(RET_HEADS)
    cols = [slice(hd * RET_DK, (hd + 1) * RET_DK) for hd in heads]
    q = [qr_ref[rows, cols[hd]] for hd in heads]
    k = [kr_ref[rows, cols[hd]] for hd in heads]
    v = [vr_ref[rows, cols[hd]] for hd in heads]
    scores = [_dot_nt(q[hd], k[hd]) for hd in heads]
    cross = [_dot(q[hd], states[hd].astype(bf16)) for hd in heads]
    kv = [_dot_tn((k[hd].astype(f32) * tables[hd][2][...]).astype(bf16), v[hd]) for hd in heads]
    new_states = [math.exp(RET_LOG_GAMMA[hd] * c) * states[hd] + kv[hd] for hd in heads]

    def finish():
        inner = [_dot((scores[hd] * tables[hd][0][...]).astype(bf16), v[hd]) for hd in heads]
        for hd in heads:
            o = inner[hd] + cross[hd] * tables[hd][1][...]
            mu = jnp.mean(o, axis=-1, keepdims=True)
            oc = o - mu
            var = jnp.mean(oc * oc, axis=-1, keepdims=True)
            y = oc * lax.rsqrt(var + LN_EPS) * gate_ref[rows, cols[hd]]
            mix_ref[rows, cols[hd]] = y.astype(bf16)

    return new_states, finish


def _att_update(m_ref, acc_ref, hd, lane0, s, v_t, ok):
    if ok is not None:
        s = jnp.where(ok, s, -jnp.inf)
    m = m_ref[hd, :, lane0:]
    m_new = jnp.maximum(m, jnp.max(s, axis=0, keepdims=True))
    p = jnp.exp2(s - m_new)
    m_ref[hd, :, lane0:] = m_new
    acc_ref[hd, :, lane0:] = jnp.exp2(m - m_new) * acc_ref[hd, :, lane0:] + _dot(v_t, p.astype(bf16))


def _mixer_kernel(qt_ref, km_ref, vt_ref, kmeta_ref, vmeta_ref, qr_ref, kr_ref, vr_ref, gate_ref,
                  krmeta_ref, vrmeta_ref, w_f32_ref, mix_ref, state_ref, w_bf16_ref,
                  m_ref, acc_ref, s_ref, dmask_ref, qdec_ref, kdec_ref):
    qi = pl.program_id(1)

    w_bf16_ref[...] = w_f32_ref[...].astype(bf16)

    @pl.when((pl.program_id(0) == 0) & (qi == 0))
    def _():
        for hd in range(RET_HEADS):
            dmask_ref[hd], qdec_ref[hd], kdec_ref[hd] = _retention_tables(RET_C, hd)

    @pl.when(qi == 0)
    def _():
        for hd in range(RET_HEADS):
            lo = hd * RET_DK
            lg = RET_LOG_GAMMA[hd]
            kd = (krmeta_ref[:, lo:lo + RET_DK].astype(f32)
                  * _row_pow(N_META, lg, N_META - 1.0, -1.0)).astype(bf16)
            state_ref[0, hd] = _dot_tn(kd, vrmeta_ref[:, lo:lo + RET_DV])

    heads = range(MLA_HEADS)
    units = [(kb, hd) for kb in range(ATT_TK // ATT_KB) for hd in heads]

    def scores(start, kb, hd, lane0=0):
        k = km_ref[0, pl.ds(start + kb * ATT_KB, ATT_KB), hd * MLA_QK_PAD:(hd + 1) * MLA_QK_PAD]
        return _dot(k, qt_ref[0, hd * MLA_QK_PAD:(hd + 1) * MLA_QK_PAD, lane0:])

    def v_block(kt, kb, hd):
        return vt_ref[kt, hd * MLA_VT_ROWS:(hd + 1) * MLA_VT_ROWS, kb * ATT_KB:(kb + 1) * ATT_KB]

    first_scores = [functools.partial(scores, 0, *units[d]) for d in range(ATT_AHEAD)]

    def trace_first_scores(n):
        for _ in range(min(n, len(first_scores))):
            d = ATT_AHEAD - len(first_scores)
            s_ref[d] = first_scores.pop(0)()

    tables = [(dmask_ref.at[hd], qdec_ref.at[hd], kdec_ref.at[hd]) for hd in range(RET_HEADS)]
    states = [state_ref[0, hd] for hd in range(RET_HEADS)]
    finishes = []
    for c in range(ATT_TQ // RET_C):
        states, finish_chunk = _retention_chunk(qr_ref, kr_ref, vr_ref, gate_ref, mix_ref,
                                                slice(c * RET_C, (c + 1) * RET_C), states, tables)
        finishes.append(finish_chunk)
    for hd in range(RET_HEADS):
        state_ref[0, hd] = states[hd]
    per_stage = -(-ATT_AHEAD // len(finishes))
    for finish_chunk in finishes:
        trace_first_scores(per_stage)
        finish_chunk()
    trace_first_scores(ATT_AHEAD)
    s_meta = [_dot(kmeta_ref[:, hd * MLA_QK_PAD:(hd + 1) * MLA_QK_PAD],
                   qt_ref[0, hd * MLA_QK_PAD:(hd + 1) * MLA_QK_PAD, :]) for hd in heads]

    p_meta = []
    for hd in heads:
        m = jnp.max(s_meta[hd], axis=0, keepdims=True)
        m_ref[hd] = m
        p_meta.append(jnp.exp2(s_meta[hd] - m).astype(bf16))
    ones = jnp.ones((N_META, MLA_VT_ROWS - MLA_V), bf16)
    for hd in heads:
        v_ext = jnp.concatenate([vmeta_ref[:, hd * MLA_V:(hd + 1) * MLA_V], ones], axis=1)
        acc_ref[hd] = _dot_tn(v_ext, p_meta[hd])

    def full_step(kt, _):
        start = pl.multiple_of(kt * ATT_TK, ATT_TK)
        pending = [s_ref[d] for d in range(ATT_AHEAD)]
        for u, (kb, hd) in enumerate(units):
            ahead = u + ATT_AHEAD
            if ahead < len(units):
                pending.append(scores(start, *units[ahead]))
            else:
                pending.append(scores(pl.multiple_of(start + ATT_TK, ATT_TK), *units[ahead - len(units)]))
            _att_update(m_ref, acc_ref, hd, 0, pending.pop(0), v_block(kt, kb, hd), None)
        for d in range(ATT_AHEAD):
            s_ref[d] = pending[d]
        return 0

    lax.fori_loop(0, qi, full_step, 0)

    start = pl.multiple_of(qi * ATT_TQ, ATT_TQ)
    pending = [s_ref[d] for d in range(ATT_AHEAD)]
    visible = {}
    for u, (kb, hd) in enumerate(units):
        lane0 = kb * ATT_KB
        if u + ATT_AHEAD < len(units):
            nxt = units[u + ATT_AHEAD]
            pending.append(scores(start, *nxt, lane0=nxt[0] * ATT_KB))
        s_cur = pending.pop(0)
        if s_cur.shape not in visible:
            k_chunk = lax.broadcasted_iota(jnp.int32, (s_cur.shape[0], 1), 0) // CHUNK
            q_chunk = lax.broadcasted_iota(jnp.int32, (1, s_cur.shape[1]), 1) // CHUNK
            visible[s_cur.shape] = k_chunk <= q_chunk
        _att_update(m_ref, acc_ref, hd, lane0, s_cur, v_block(qi, kb, hd), visible[s_cur.shape])
    for hd in heads:
        out = (acc_ref[hd, :MLA_V, :] / acc_ref[hd, MLA_V:MLA_V + 1, :]).T
        mix_ref[:, RET_W + hd * MLA_V:RET_W + (hd + 1) * MLA_V] = out.astype(bf16)


def _mixer_call(qt, km, vt, kmeta, vmeta, qr, kr, vr, gate, krmeta, vrmeta, w_f32, batch, seq):
    nq = seq // ATT_TQ
    tile = lambda w: pl.BlockSpec((ATT_TQ, w), lambda b, i: (b * nq + i, 0))
    const = lambda a: pl.BlockSpec((N_META, a.shape[1]), lambda b, i: (a.shape[0] // N_META - 1, 0))
    w_rows, rem = divmod(w_f32.shape[0], batch * nq)
    assert rem == 0 and w_rows % BF16_SUBLANES == 0
    w_slice = pl.BlockSpec((w_rows, w_f32.shape[1]), lambda b, i: (b * nq + i, 0))
    return pl.pallas_call(
        _mixer_kernel,
        grid=(batch, nq),
        in_specs=[pl.BlockSpec((1, MLA_QW, ATT_TQ), lambda b, i: (b * nq + i, 0, 0)),
                  pl.BlockSpec((1, seq, MLA_QW), lambda b, i: (b, 0, 0)),
                  pl.BlockSpec((seq // ATT_TK, MLA_HEADS * MLA_VT_ROWS, ATT_TK), lambda b, i: (b, 0, 0)),
                  const(kmeta), const(vmeta),
                  tile(RET_W), tile(RET_W), tile(RET_W), tile(RET_W),
                  const(krmeta), const(vrmeta), w_slice],
        out_specs=[tile(D_MODEL),
                   pl.BlockSpec((1, RET_HEADS, RET_DK, RET_DV), lambda b, i: (b, 0, 0, 0)), w_slice],
        out_shape=[jax.ShapeDtypeStruct((batch * seq, D_MODEL), bf16),
                   jax.ShapeDtypeStruct((batch, RET_HEADS, RET_DK, RET_DV), f32),
                   jax.ShapeDtypeStruct(w_f32.shape, bf16)],
        scratch_shapes=[pltpu.VMEM((MLA_HEADS, 1, ATT_TQ), f32),
                        pltpu.VMEM((MLA_HEADS, MLA_VT_ROWS, ATT_TQ), f32),
                        pltpu.VMEM((ATT_AHEAD, ATT_KB, ATT_TQ), f32),
                        pltpu.VMEM((RET_HEADS, RET_C, RET_C), f32),
                        pltpu.VMEM((RET_HEADS, RET_C, LANES), f32),
                        pltpu.VMEM((RET_HEADS, RET_C, LANES), f32)],
        compiler_params=pltpu.CompilerParams(dimension_semantics=("arbitrary", "arbitrary"),
                                             vmem_limit_bytes=VMEM_LIMIT),
        name="mixer",
    )(qt, km.reshape(batch, seq, MLA_QW), vt, kmeta, vmeta, qr, kr, vr, gate, krmeta, vrmeta, w_f32)


def _smixer_kernel(qm_ref, cnew_ref, pnew_ref, cmeta_ref, pmeta_ref, ccache_ref, pcache_ref, w_qk_ref, w_ov_ref,
                   qr_ref, kr_ref, vr_ref, gate_ref, s0_ref, wa_f32_ref, wb_f32_ref,
                   mix_ref, state_ref, wa_bf16_ref, wb_bf16_ref):
    wa_bf16_ref[...] = wa_f32_ref[...].astype(bf16)
    wb_bf16_ref[...] = wb_f32_ref[...].astype(bf16)

    t = qr_ref.shape[0]
    states, finish_retention = _retention_chunk(qr_ref, kr_ref, vr_ref, gate_ref, mix_ref, slice(0, t),
                                                [s0_ref[0, hd] for hd in range(RET_HEADS)],
                                                [_retention_tables(t, hd) for hd in range(RET_HEADS)])
    for hd in range(RET_HEADS):
        state_ref[0, hd] = states[hd]

    heads = range(MLA_HEADS)
    q_nope = jnp.concatenate([qm_ref[:, hd * MLA_QK_PAD:hd * MLA_QK_PAD + MLA_NOPE] for hd in heads], axis=1)
    q_abs = _dot(q_nope, w_qk_ref[...])
    finish_retention()
    q_lat = jnp.concatenate([q_abs[:, hd * MLA_KV_LORA:(hd + 1) * MLA_KV_LORA] for hd in heads], axis=0).astype(bf16)
    q_rope = jnp.concatenate([qm_ref[:, hd * MLA_QK_PAD + MLA_NOPE:hd * MLA_QK_PAD + MLA_NOPE + MLA_ROPE]
                              for hd in heads], axis=0)
    lat = [cmeta_ref[...].astype(bf16), ccache_ref[0].astype(bf16), cnew_ref[...].astype(bf16)]
    s_rope = [_dot_nt(q_rope, pmeta_ref[...].astype(bf16)), _dot(q_rope, pcache_ref[0].astype(bf16)),
              _dot_nt(q_rope, pnew_ref[...].astype(bf16))]
    scores = [(_dot_nt(q_lat, c) + sr) * MLA_SCALE for c, sr in zip(lat, s_rope)]
    m = functools.reduce(jnp.maximum, [jnp.max(s, axis=-1, keepdims=True) for s in scores])
    probs = [jnp.exp(s - m) for s in scores]
    denom = functools.reduce(jnp.add, [jnp.sum(p, axis=-1, keepdims=True) for p in probs])
    o_lat = functools.reduce(jnp.add, [_dot(p.astype(bf16), c) for p, c in zip(probs, lat)]) / denom
    o_all = jnp.concatenate([o_lat[hd * t:(hd + 1) * t, :] for hd in heads], axis=1).astype(bf16)
    mix_ref[:, RET_W:] = _dot(o_all, w_ov_ref[...]).astype(bf16)


def _smixer_call(qm, ckv, kpe, cckv, ckpe, w_qk, w_ov, qr, kr, vr, gate, s0, wa_f32, wb_f32, t):
    db, past = cckv.shape[0], cckv.shape[1]
    tile = lambda w: pl.BlockSpec((t, w), lambda b: (b, 0))
    meta = lambda w: pl.BlockSpec((N_META, w), lambda b: (db * t // N_META, 0))
    const = lambda a: pl.BlockSpec(a.shape, lambda b: (0,) * a.ndim)
    per = lambda *tail: pl.BlockSpec((1,) + tail, lambda b: (b,) + (0,) * len(tail))

    def w_slice(w):
        rows, rem = divmod(w.shape[0], db)
        assert rem == 0 and rows % BF16_SUBLANES == 0
        return pl.BlockSpec((rows, w.shape[1]), lambda b: (b, 0))

    return pl.pallas_call(
        _smixer_kernel,
        grid=(db,),
        in_specs=[tile(MLA_QW), tile(MLA_KV_LORA), tile(MLA_ROPE), meta(MLA_KV_LORA), meta(MLA_ROPE),
                  per(past, MLA_KV_LORA), per(MLA_ROPE, past), const(w_qk), const(w_ov),
                  tile(RET_W), tile(RET_W), tile(RET_W), tile(RET_W),
                  per(RET_HEADS, RET_DK, RET_DV), w_slice(wa_f32), w_slice(wb_f32)],
        out_specs=[tile(D_MODEL), per(RET_HEADS, RET_DK, RET_DV), w_slice(wa_f32), w_slice(wb_f32)],
        out_shape=[jax.ShapeDtypeStruct((db * t, D_MODEL), bf16),
                   jax.ShapeDtypeStruct((db, RET_HEADS, RET_DK, RET_DV), f32),
                   jax.ShapeDtypeStruct(wa_f32.shape, bf16), jax.ShapeDtypeStruct(wb_f32.shape, bf16)],
        compiler_params=pltpu.CompilerParams(dimension_semantics=("arbitrary",),
                                             vmem_limit_bytes=VMEM_LIMIT),
        name="smixer",
    )(qm, ckv, kpe, ckv, kpe, cckv, ckpe, w_qk, w_ov, qr, kr, vr, gate, s0, wa_f32, wb_f32)


def _tail_kernel(sub, mix_ref, h_ref, w_mo_ref, ln2_g_ref, ln2_b_ref, w_in_ref, w_out_ref, ln3_g_ref, ln3_b_ref,
                 y_ref):
    tiles = [slice(j * sub, (j + 1) * sub) for j in range(mix_ref.shape[0] // sub)]
    mixes = [_dot(mix_ref[r, :], w_mo_ref[...]) for r in tiles]
    h2s, ys = [], []
    for r, mix in zip(tiles, mixes):
        h2 = _layer_norm(DN_ALPHA * h_ref[r, :] + mix, ln2_g_ref[...], ln2_b_ref[...])
        h2s.append(h2)
        ys.append(_swiglu(h2, w_in_ref.at[:, :D_FF], w_in_ref.at[:, D_FF:], w_out_ref))
    for r, h2, y in zip(tiles, h2s, ys):
        y_ref[r, :] = _ffn_norm(h2, y, ln3_g_ref, ln3_b_ref)


def _tail_call(mix, h, wts, tm, sub):
    n = mix.shape[0]
    row = lambda w: pl.BlockSpec((tm, w), lambda i: (i, 0))
    return pl.pallas_call(
        functools.partial(_tail_kernel, sub),
        grid=(n // tm,),
        in_specs=[row(D_MODEL), row(D_MODEL)] + [_const_spec(w.shape) for w in wts],
        out_specs=row(D_MODEL),
        out_shape=jax.ShapeDtypeStruct((n, D_MODEL), f32),
        compiler_params=pltpu.CompilerParams(dimension_semantics=("arbitrary",),
                                             vmem_limit_bytes=VMEM_LIMIT),
        name="tail",
    )(mix, h, *wts)


def _rope_tables(pos):
    posf = pos.astype(f32)[:, None]
    inv_r = ROPE_BASE ** (-jnp.arange(0, RET_DK, 2, dtype=f32) / RET_DK)
    ang = posf * inv_r[None, :]
    inv_m = ROPE_BASE ** (-jnp.arange(0, MLA_ROPE, 2, dtype=f32) / MLA_ROPE)
    angm = posf * inv_m[None, :]
    return jnp.cos(ang), jnp.sin(ang), jnp.cos(angm), jnp.sin(angm)


def kernel(x_prompt, x_sample, cache_mla_ckv, cache_mla_kpe, state_ret, meta_tokens,
           ffn1_w_in, ffn1_w_out, ln1_g, ln1_b, w_mix_in, ret_gn_g, mla_q_norm_g, mla_w_uq,
           mla_kv_norm_g, mla_w_ukv, w_mix_out, ln2_g, ln2_b, ffn2_w_in, ffn2_w_out, ln3_g, ln3_b):
    B, S, D = x_prompt.shape
    DB, T, _ = x_sample.shape
    P = cache_mla_ckv.shape[1]
    assert S % ATT_TQ == 0 and S % FRONT_TM == 0 and (B * S) % TAIL_TM == 0 and ATT_TQ % RET_C == 0
    row = lambda a: a.reshape(1, -1).astype(f32)

    w_uq = mla_w_uq.reshape(MLA_Q_LORA, MLA_HEADS, MLA_NOPE + MLA_ROPE)
    w_uq = jnp.pad(w_uq, ((0, 0), (0, 0), (0, MLA_QK_PAD - MLA_NOPE - MLA_ROPE)))
    w_uq = w_uq.reshape(MLA_Q_LORA, MLA_QW).astype(bf16)
    w_ukv = mla_w_ukv.reshape(MLA_KV_LORA, MLA_HEADS, MLA_NOPE + MLA_V)
    w_uk = w_ukv[:, :, :MLA_NOPE].reshape(MLA_KV_LORA, -1).astype(bf16)
    w_uv = w_ukv[:, :, MLA_NOPE:].reshape(MLA_KV_LORA, -1).astype(bf16)
    shared_w = lambda w_mix: [row(ln1_g), row(ln1_b), w_mix, row(ret_gn_g), row(mla_q_norm_g)]

    n_s = DB * T
    x_small = jnp.concatenate([x_sample.reshape(n_s, D), meta_tokens.astype(x_sample.dtype)], axis=0)
    pos_small = jnp.concatenate([jnp.tile(N_META + P + jnp.arange(T), DB), jnp.arange(N_META)])
    (h_s, qr_s, kr_s, vr_s, gr_s, qm_s, km_s, vm_s, ckv_s, kpe_s, w1_gate, w1_up, w1_out, w_mix) = _front_small_call(
        x_small, _rope_tables(pos_small), jnp.concatenate(_rope_tables(jnp.zeros((1,), jnp.int32)), axis=1)[:, None, :],
        ffn1_w_in, ffn1_w_out, shared_w(w_mix_in.T) + [w_uq, row(mla_kv_norm_g), w_uk, w_uv])

    tabs_p = _rope_tables(N_META + jnp.arange(FRONT_TM))
    steps_p = _rope_tables(FRONT_TM * jnp.arange(S // FRONT_TM))
    tabs_t_p = (jnp.concatenate([tabs_p[2].T, tabs_p[3].T], axis=0),
                jnp.concatenate([steps_p[2], steps_p[3]], axis=1)[:, :, None])
    (h_p, qr_p, kr_p, vr_p, gr_p, qt_p, km_p, vt_p, ckv_p, kpe_p) = _front_call(
        x_prompt.reshape(B * S, D), tabs_p, jnp.concatenate(steps_p, axis=1)[:, None, :], tabs_t_p,
        [w1_gate, w1_up, w1_out] + shared_w(w_mix) + [w_uq.T, row(mla_kv_norm_g), w_uk, w_uv.T], FRONT_TM, FRONT_SUB)

    mix_p, p_state, w2_in = _mixer_call(qt_p, km_p, vt_p, km_s, vm_s, qr_p, kr_p, vr_p, gr_p, kr_s, vr_s,
                                        ffn2_w_in, B, S)
    blocks = lambda w, n: [w[:, hd * n:(hd + 1) * n] for hd in range(MLA_HEADS)]
    w_qk = jax.scipy.linalg.block_diag(*[blk.T for blk in blocks(w_uk, MLA_NOPE)])
    w_ov = jax.scipy.linalg.block_diag(*blocks(w_uv, MLA_V))
    mix_s, s_state, w2_out, w_mo = _smixer_call(qm_s, ckv_s, kpe_s, cache_mla_ckv,
                                                jnp.transpose(cache_mla_kpe, (0, 2, 1)), w_qk, w_ov,
                                                qr_s, kr_s, vr_s, gr_s, state_ret, ffn2_w_out, w_mix_out, T)

    tail_w = [w_mo, row(ln2_g), row(ln2_b), w2_in, w2_out, row(ln3_g), row(ln3_b)]
    y_p = _tail_call(mix_p, h_p, tail_w, TAIL_TM, FRONT_SUB)
    y_p, mix_s = lax.optimization_barrier((y_p, mix_s))
    y_s = _tail_call(mix_s, h_s, tail_w, n_s, n_s // 2)

    meta_ckv = jnp.broadcast_to(ckv_s[n_s:][None], (B, N_META, MLA_KV_LORA))
    meta_kpe_t = jnp.broadcast_to(kpe_s[n_s:].T[None], (B, MLA_ROPE, N_META))
    p_ckv = jnp.concatenate([meta_ckv, ckv_p.reshape(B, S, MLA_KV_LORA)], axis=1)
    p_kpe = jnp.transpose(jnp.concatenate([meta_kpe_t, kpe_p], axis=2), (0, 2, 1))
    return (y_p.reshape(B, S, D), y_s.reshape(DB, T, D), p_ckv, p_kpe, p_state.astype(x_prompt.dtype),
            ckv_s[:n_s].reshape(DB, T, MLA_KV_LORA), kpe_s[:n_s].reshape(DB, T, MLA_ROPE),
            s_state.astype(state_ret.dtype))
```

```python
import functools
import math

import jax
import jax.numpy as jnp
from jax import lax
from jax.experimental import pallas as pl
from jax.experimental.pallas import tpu as pltpu

D_MODEL = 1024
DEPTH = 1
CHUNK = 64
N_META = 16
RET_HEADS = 4
RET_DK = 128
RET_DV = 128
MLA_HEADS = 4
MLA_NOPE = 128
MLA_ROPE = 64
MLA_V = 128
MLA_Q_LORA = 256
MLA_KV_LORA = 128
D_FF = 2816
ROPE_BASE = 10000.0
LN_EPS = 1e-5
RMS_EPS = 1e-6
DN_ALPHA = (2 * DEPTH) ** 0.25

OFF_KR = RET_HEADS * RET_DK
OFF_VR = 2 * RET_HEADS * RET_DK
OFF_GR = OFF_VR + RET_HEADS * RET_DV
OFF_CQ = OFF_GR + RET_HEADS * RET_DV
OFF_CKV = OFF_CQ + MLA_Q_LORA
OFF_KPE = OFF_CKV + MLA_KV_LORA
D_IN = OFF_KPE + MLA_ROPE

LANES = 128
MLA_QK_PAD = 2 * LANES
RET_W = RET_HEADS * RET_DK
MLA_QW = MLA_HEADS * MLA_QK_PAD
MLA_VW = MLA_HEADS * MLA_V
BF16_SUBLANES = 16
MLA_VT_ROWS = MLA_V + BF16_SUBLANES

FRONT_SUB = 256
FRONT_TM = 2 * FRONT_SUB
TAIL_SUB = 128
TAIL_TM = 8 * TAIL_SUB
FF_CHUNK = 256
ATT_TQ = FRONT_TM
ATT_TK = FRONT_TM
ATT_KB = 512
ATT_AHEAD = 4
RET_C = 256
VMEM_LIMIT = 56 * 1024 * 1024

MLA_SCALE = (MLA_NOPE + MLA_ROPE) ** -0.5
MLA_SCALE_LOG2E = MLA_SCALE * math.log2(math.e)
RET_LOG_GAMMA = tuple(math.log(1.0 - 2.0 ** (-5.0 - h)) for h in range(RET_HEADS))

f32 = jnp.float32
bf16 = jnp.bfloat16


def _dot(a, b):
    return jnp.dot(a, b, preferred_element_type=f32)


def _dot_nt(a, b):
    return lax.dot_general(a, b, (((1,), (1,)), ((), ())), preferred_element_type=f32)


def _dot_tn(a, b):
    return lax.dot_general(a, b, (((0,), (0,)), ((), ())), preferred_element_type=f32)


def _layer_norm(x, g, b):
    mu = jnp.mean(x, axis=-1, keepdims=True)
    xc = x - mu
    var = jnp.mean(xc * xc, axis=-1, keepdims=True)
    return xc * lax.rsqrt(var + LN_EPS) * g + b


def _rms_norm(x, g):
    return x * lax.rsqrt(jnp.mean(x * x, axis=-1, keepdims=True) + RMS_EPS) * g


def _silu(x):
    return x / (1.0 + jnp.exp(-x))


def _swiglu(x, w_gate, w_up, w_out):
    xb = x.astype(bf16)
    hg = _dot(xb, w_gate[...])
    hu = _dot(xb, w_up[...])
    return _dot((_silu(hg) * hu).astype(bf16), w_out[...])


def _ffn_norm(x, y, g_ref, b_ref):
    return _layer_norm(DN_ALPHA * x + 0.5 * y, g_ref[...], b_ref[...])


def _rotate(cb, sb, ca, sa):
    return cb * ca - sb * sa, sb * ca + cb * sa


def _front_kernel(sub, x_ref, cr_ref, sr_ref, cm_ref, sm_ref, step_ref, tab_t_ref, step_t_ref,
                  w_gate_ref, w_up_ref, w_out_ref, *refs):
    n_sub = x_ref.shape[0] // sub
    tiles = [slice(j * sub, (j + 1) * sub) for j in range(n_sub)]
    xs = [x_ref[r, :] for r in tiles]
    ys = [_swiglu(x, w_gate_ref, w_up_ref, w_out_ref) for x in xs]
    _front_rest((tab_t_ref, step_t_ref), tiles, xs, ys, (cr_ref, sr_ref, cm_ref, sm_ref, step_ref), refs)


def _front_rest(tabs_t, tiles, xs, ys, tabs, refs):
    cr_ref, sr_ref, cm_ref, sm_ref, step_ref = tabs
    transposed = tabs_t is not None
    if transposed:
        tab_t_ref, step_t_ref = tabs_t
    (ln_g_ref, ln_b_ref, w_mix_ref, gn_ref, qn_g_ref, w_uq_ref, kvn_g_ref, w_uk_ref, w_uv_ref,
     h_ref, qr_ref, kr_ref, vr_ref, gate_ref, qm_ref, km_ref, vm_ref, ckv_ref, kpe_ref) = refs
    ps = []
    for r, x, y in zip(tiles, xs, ys):
        h = _ffn_norm(x, y, ln_g_ref, ln_b_ref)
        h_ref[r, :] = h
        hb = h.astype(bf16)
        p_kpe = _dot_nt(hb, w_mix_ref[OFF_KPE:, :])
        ps.append((_dot_nt(hb, w_mix_ref[:OFF_KPE, :]),
                   jnp.concatenate([p_kpe, jnp.zeros_like(p_kpe)], axis=1)))

    for j, (r, (p, p_kpe)) in enumerate(zip(tiles, ps)):
        half_r, half_m = RET_DK // 2, MLA_ROPE // 2
        c, s = _rotate(cr_ref[r, :], sr_ref[r, :], step_ref[0, :, :half_r], step_ref[0, :, half_r:2 * half_r])
        cm, sm = _rotate(cm_ref[r, :], sm_ref[r, :], step_ref[0, :, 2 * half_r:2 * half_r + half_m],
                         step_ref[0, :, 2 * half_r + half_m:])
        z_q, z_h = jnp.zeros_like(cm), jnp.zeros_like(c)
        c_r = jnp.concatenate([c, c], axis=1)
        s_r = jnp.concatenate([-s, s], axis=1)
        c_m = jnp.concatenate([cm, cm, z_h], axis=1)
        s_ma = jnp.concatenate([-sm, z_q, z_h], axis=1)
        s_mb = jnp.concatenate([z_q, sm, z_h], axis=1)

        def rope_ret(xh):
            return xh * c_r + pltpu.roll(xh, 64, 1) * s_r

        def rope_mla(xh):
            return xh * c_m + pltpu.roll(xh, 96, 1) * s_ma + pltpu.roll(xh, 32, 1) * s_mb

        cq = _rms_norm(p[:, OFF_CQ:OFF_CKV], qn_g_ref[...]).astype(bf16)
        ckv = _rms_norm(p[:, OFF_CKV:OFF_KPE], kvn_g_ref[...])
        ckv_ref[r, :] = ckv
        ckv_b = ckv.astype(bf16)
        k_nope = _dot(ckv_b, w_uk_ref[...])
        if transposed:
            q_t = _dot_nt(w_uq_ref[...], cq) * MLA_SCALE_LOG2E
            v_t = _dot_nt(w_uv_ref[...], ckv_b)
        else:
            q = _dot(cq, w_uq_ref[...])
            v = _dot(ckv_b, w_uv_ref[...])

        for hd in range(RET_HEADS):
            lo = hd * RET_DK
            qr_ref[r, lo:lo + RET_DK] = rope_ret(p[:, lo:lo + RET_DK]).astype(bf16)
            kr_ref[r, lo:lo + RET_DK] = (rope_ret(p[:, OFF_KR + lo:OFF_KR + lo + RET_DK])
                                         * RET_DK ** -0.5).astype(bf16)
        vr_ref[r, :] = p[:, OFF_VR:OFF_GR].astype(bf16)
        gate_ref[r, :] = gn_ref[...] * _silu(p[:, OFF_GR:OFF_CQ])
        kpe = rope_mla(p_kpe)
        if transposed:
            kpe_ref[0, :, r] = kpe.T[:MLA_ROPE, :]
        else:
            kpe_ref[r, :] = kpe[:, :MLA_ROPE]
        kpe_b = kpe.astype(bf16)
        for hd in range(MLA_HEADS):
            lo = hd * MLA_QK_PAD
            km_ref[r, lo:lo + LANES] = k_nope[:, hd * MLA_NOPE:(hd + 1) * MLA_NOPE].astype(bf16)
            km_ref[r, lo + LANES:lo + 2 * LANES] = kpe_b

        if transposed:
            half = MLA_ROPE // 2
            c_t, s_t = _rotate(tab_t_ref[:half, r], tab_t_ref[half:, r], step_t_ref[0, :half, :], step_t_ref[0, half:, :])
            for hd in range(MLA_HEADS):
                lo = hd * MLA_QK_PAD
                r0 = lo + MLA_NOPE
                x1 = q_t[r0:r0 + half, :]
                x2 = q_t[r0 + half:r0 + 2 * half, :]
                qm_ref[0, lo:r0, r] = q_t[lo:r0, :].astype(bf16)
                qm_ref[0, r0:r0 + half, r] = (x1 * c_t - x2 * s_t).astype(bf16)
                qm_ref[0, r0 + half:r0 + 2 * half, r] = (x1 * s_t + x2 * c_t).astype(bf16)
                qm_ref[0, r0 + 2 * half:lo + MLA_QK_PAD, r] = q_t[r0 + 2 * half:lo + MLA_QK_PAD, :].astype(bf16)
            ones = jnp.ones((MLA_VT_ROWS - MLA_V, v_t.shape[1]), bf16)
            for hd in range(MLA_HEADS):
                vm_ref[0, hd * MLA_VT_ROWS:hd * MLA_VT_ROWS + MLA_V, r] = v_t[hd * MLA_V:(hd + 1) * MLA_V, :].astype(bf16)
                vm_ref[0, hd * MLA_VT_ROWS + MLA_V:(hd + 1) * MLA_VT_ROWS, r] = ones
        else:
            for hd in range(MLA_HEADS):
                lo = hd * MLA_QK_PAD
                qm_ref[r, lo:lo + LANES] = q[:, lo:lo + LANES].astype(bf16)
                qm_ref[r, lo + LANES:lo + 2 * LANES] = rope_mla(q[:, lo + LANES:lo + 2 * LANES]).astype(bf16)
            vm_ref[r, :] = v.astype(bf16)


def _front_small_kernel(x_ref, cr_ref, sr_ref, cm_ref, sm_ref, step_ref, wg_f32_ref, wu_f32_ref, wo_f32_ref, *refs):
    *rest, wg_ref, wu_ref, wo_ref, wmix_ref, y_ref = refs
    c = pl.program_id(0)

    @pl.when(c == 0)
    def _():
        wmix_ref[...] = rest[2][...].astype(bf16)

    rest = rest[:2] + [wmix_ref] + rest[3:]
    wg = wg_f32_ref[...].astype(bf16)
    wu = wu_f32_ref[...].astype(bf16)
    wo = wo_f32_ref[...].astype(bf16)
    wg_ref[...] = wg
    wu_ref[...] = wu
    wo_ref[...] = wo
    x = x_ref[...]
    part = _swiglu(x, wg, wu, wo)

    @pl.when(c == 0)
    def _():
        y_ref[...] = part

    @pl.when(c > 0)
    def _():
        y_ref[...] += part

    @pl.when(c == pl.num_programs(0) - 1)
    def _():
        _front_rest(None, [slice(0, x.shape[0])], [x], [y_ref[...]], (cr_ref, sr_ref, cm_ref, sm_ref, step_ref), rest)


def _const_spec(shape):
    nd = len(shape)
    return pl.BlockSpec(shape, lambda *_: (0,) * nd, pipeline_mode=pl.Buffered(1))


def _front_small_call(x, tabs, steps, w_in, w_out, wts):
    n = x.shape[0]
    n_c = D_FF // FF_CHUNK
    full = lambda a: pl.BlockSpec(a.shape, lambda c: (0,) * a.ndim)
    gate_cols = pl.BlockSpec((D_MODEL, FF_CHUNK), lambda c: (0, c))
    up_cols = pl.BlockSpec((D_MODEL, FF_CHUNK), lambda c: (0, n_c + c))
    out_rows = pl.BlockSpec((FF_CHUNK, D_MODEL), lambda c: (c, 0))
    widths = [(D_MODEL, f32), (RET_W, bf16), (RET_W, bf16), (RET_W, bf16), (RET_W, f32), (MLA_QW, bf16),
              (MLA_QW, bf16), (MLA_VW, bf16), (MLA_KV_LORA, f32), (MLA_ROPE, f32)]
    return pl.pallas_call(
        _front_small_kernel,
        grid=(n_c,),
        in_specs=([full(x)] + [full(t) for t in tabs] + [full(steps), gate_cols, up_cols, out_rows]
                  + [_const_spec(w.shape) for w in wts]),
        out_specs=([pl.BlockSpec((n, w), lambda c: (0, 0)) for w, _ in widths]
                   + [gate_cols, gate_cols, out_rows, full(wts[2])]),
        out_shape=([jax.ShapeDtypeStruct((n, w), dt) for w, dt in widths]
                   + [jax.ShapeDtypeStruct((D_MODEL, D_FF), bf16), jax.ShapeDtypeStruct((D_MODEL, D_FF), bf16),
                      jax.ShapeDtypeStruct((D_FF, D_MODEL), bf16), jax.ShapeDtypeStruct(wts[2].shape, bf16)]),
        scratch_shapes=[pltpu.VMEM((n, D_MODEL), f32)],
        compiler_params=pltpu.CompilerParams(dimension_semantics=("arbitrary",),
                                             vmem_limit_bytes=VMEM_LIMIT),
        name="front",
    )(x, *tabs, steps, w_in, w_in, w_out, *wts)


def _front_call(x, tabs, steps, tabs_t, wts, tm, sub):
    n = x.shape[0]
    period = steps.shape[0]
    row = lambda w: pl.BlockSpec((tm, w), lambda i: (i, 0))
    col = lambda w: pl.BlockSpec((1, w, tm), lambda i: (i, 0, 0))
    in_specs = [row(D_MODEL)] + [pl.BlockSpec((tm, t.shape[1]), lambda i: (0, 0)) for t in tabs]
    in_specs.append(pl.BlockSpec((1,) + steps.shape[1:], lambda i: (i % period, 0, 0)))
    in_specs.append(pl.BlockSpec((MLA_ROPE, tm), lambda i: (0, 0)))
    in_specs.append(pl.BlockSpec((1,) + tabs_t[1].shape[1:], lambda i: (i % period, 0, 0)))
    in_specs += [_const_spec(w.shape) for w in wts]
    rows = lambda w, dt: (row(w), jax.ShapeDtypeStruct((n, w), dt))
    cols = lambda w, dt: (col(w), jax.ShapeDtypeStruct((n // tm, w, tm), dt))
    outs = [rows(D_MODEL, f32), rows(RET_W, bf16), rows(RET_W, bf16), rows(RET_W, bf16), rows(RET_W, f32),
            cols(MLA_QW, bf16), rows(MLA_QW, bf16), cols(MLA_HEADS * MLA_VT_ROWS, bf16),
            rows(MLA_KV_LORA, f32),
            (pl.BlockSpec((1, MLA_ROPE, tm), lambda i: (i // period, 0, i % period)),
             jax.ShapeDtypeStruct((n // (period * tm), MLA_ROPE, period * tm), f32))]
    return pl.pallas_call(
        functools.partial(_front_kernel, sub),
        grid=(n // tm,),
        in_specs=in_specs,
        out_specs=[o[0] for o in outs],
        out_shape=[o[1] for o in outs],
        compiler_params=pltpu.CompilerParams(dimension_semantics=("arbitrary",),
                                             vmem_limit_bytes=VMEM_LIMIT),
        name="front_t",
    )(x, *tabs, steps, *tabs_t, *wts)


def _decay_mask(c, lg):
    r = lax.broadcasted_iota(jnp.int32, (c, c), 0)
    k = lax.broadcasted_iota(jnp.int32, (c, c), 1)
    rel = (r - k).astype(f32)
    return jnp.where(rel >= 0.0, jnp.exp(lg * jnp.maximum(rel, 0.0)), 0.0)


def _row_pow(c, lg, offset, sign):
    i = lax.broadcasted_iota(jnp.int32, (c, LANES), 0).astype(f32)
    return jnp.exp(lg * (offset + sign * i))


def _retention_tables(c, hd):
    lg = RET_LOG_GAMMA[hd]
    return _decay_mask(c, lg), _row_pow(c, lg, 1.0, 1.0), _row_pow(c, lg, c - 1.0, -1.0)


def _retention_chunk(qr_ref, kr_ref, vr_ref, gate_ref, mix_ref, rows, states, tables):
    c = rows.stop - rows.start
    heads = range(RET_HEADS)
    cols = [slice(hd * RET_DK, (hd + 1) * RET_DK) for hd in heads]
    q = [qr_ref[rows, cols[hd]] for hd in heads]
    k = [kr_ref[rows, cols[hd]] for hd in heads]
    v = [vr_ref[rows, cols[hd]] for hd in heads]
    scores = [_dot_nt(q[hd], k[hd]) for hd in heads]
    cross = [_dot(q[hd], states[hd].astype(bf16)) for hd in heads]
    kv = [_dot_tn((k[hd].astype(f32) * tables[hd][2][...]).astype(bf16), v[hd]) for hd in heads]
    new_states = [math.exp(RET_LOG_GAMMA[hd] * c) * states[hd] + kv[hd] for hd in heads]

    def finish():
        inner = [_dot((scores[hd] * tables[hd][0][...]).astype(bf16), v[hd]) for hd in heads]
        for hd in heads:
            o = inner[hd] + cross[hd] * tables[hd][1][...]
            mu = jnp.mean(o, axis=-1, keepdims=True)
            oc = o - mu
            var = jnp.mean(oc * oc, axis=-1, keepdims=True)
            y = oc * lax.rsqrt(var + LN_EPS) * gate_ref[rows, cols[hd]]
            mix_ref[rows, cols[hd]] = y.astype(bf16)

    return new_states, finish


def _att_update(m_ref, acc_ref, hd, lane0, s, v_t, ok):
    if ok is not None:
        s = jnp.where(ok, s, -jnp.inf)
    m = m_ref[hd, :, lane0:]
    m_new = jnp.maximum(m, jnp.max(s, axis=0, keepdims=True))
    p = jnp.exp2(s - m_new)
    m_ref[hd, :, lane0:] = m_new
    acc_ref[hd, :, lane0:] = jnp.exp2(m - m_new) * acc_ref[hd, :, lane0:] + _dot(v_t, p.astype(bf16))


def _mixer_kernel(qt_ref, km_ref, vt_ref, kmeta_ref, vmeta_ref, qr_ref, kr_ref, vr_ref, gate_ref,
                  krmeta_ref, vrmeta_ref, w_f32_ref, mix_ref, state_ref, w_bf16_ref,
                  m_ref, acc_ref, s_ref, dmask_ref, qdec_ref, kdec_ref):
    qi = pl.program_id(1)

    w_bf16_ref[...] = w_f32_ref[...].astype(bf16)

    @pl.when((pl.program_id(0) == 0) & (qi == 0))
    def _():
        for hd in range(RET_HEADS):
            dmask_ref[hd], qdec_ref[hd], kdec_ref[hd] = _retention_tables(RET_C, hd)

    @pl.when(qi == 0)
    def _():
        for hd in range(RET_HEADS):
            lo = hd * RET_DK
            lg = RET_LOG_GAMMA[hd]
            kd = (krmeta_ref[:, lo:lo + RET_DK].astype(f32)
                  * _row_pow(N_META, lg, N_META - 1.0, -1.0)).astype(bf16)
            state_ref[0, hd] = _dot_tn(kd, vrmeta_ref[:, lo:lo + RET_DV])

    heads = range(MLA_HEADS)
    units = [(kb, hd) for kb in range(ATT_TK // ATT_KB) for hd in heads]

    def scores(start, kb, hd, lane0=0):
        k = km_ref[0, pl.ds(start + kb * ATT_KB, ATT_KB), hd * MLA_QK_PAD:(hd + 1) * MLA_QK_PAD]
        return _dot(k, qt_ref[0, hd * MLA_QK_PAD:(hd + 1) * MLA_QK_PAD, lane0:])

    def v_block(kt, kb, hd):
        return vt_ref[kt, hd * MLA_VT_ROWS:(hd + 1) * MLA_VT_ROWS, kb * ATT_KB:(kb + 1) * ATT_KB]

    first_scores = [functools.partial(scores, 0, *units[d]) for d in range(ATT_AHEAD)]

    def trace_first_scores(n):
        for _ in range(min(n, len(first_scores))):
            d = ATT_AHEAD - len(first_scores)
            s_ref[d] = first_scores.pop(0)()

    tables = [(dmask_ref.at[hd], qdec_ref.at[hd], kdec_ref.at[hd]) for hd in range(RET_HEADS)]
    states = [state_ref[0, hd] for hd in range(RET_HEADS)]
    finishes = []
    for c in range(ATT_TQ // RET_C):
        states, finish_chunk = _retention_chunk(qr_ref, kr_ref, vr_ref, gate_ref, mix_ref,
                                                slice(c * RET_C, (c + 1) * RET_C), states, tables)
        finishes.append(finish_chunk)
    for hd in range(RET_HEADS):
        state_ref[0, hd] = states[hd]
    per_stage = -(-ATT_AHEAD // len(finishes))
    for finish_chunk in finishes:
        trace_first_scores(per_stage)
        finish_chunk()
    trace_first_scores(ATT_AHEAD)
    s_meta = [_dot(kmeta_ref[:, hd * MLA_QK_PAD:(hd + 1) * MLA_QK_PAD],
                   qt_ref[0, hd * MLA_QK_PAD:(hd + 1) * MLA_QK_PAD, :]) for hd in heads]

    p_meta = []
    for hd in heads:
        m = jnp.max(s_meta[hd], axis=0, keepdims=True)
        m_ref[hd] = m
        p_meta.append(jnp.exp2(s_meta[hd] - m).astype(bf16))
    ones = jnp.ones((N_META, MLA_VT_ROWS - MLA_V), bf16)
    for hd in heads:
        v_ext = jnp.concatenate([vmeta_ref[:, hd * MLA_V:(hd + 1) * MLA_V], ones], axis=1)
        acc_ref[hd] = _dot_tn(v_ext, p_meta[hd])

    def full_step(kt, _):
        start = pl.multiple_of(kt * ATT_TK, ATT_TK)
        pending = [s_ref[d] for d in range(ATT_AHEAD)]
        for u, (kb, hd) in enumerate(units):
            ahead = u + ATT_AHEAD
            if ahead < len(units):
                pending.append(scores(start, *units[ahead]))
            else:
                pending.append(scores(pl.multiple_of(start + ATT_TK, ATT_TK), *units[ahead - len(units)]))
            _att_update(m_ref, acc_ref, hd, 0, pending.pop(0), v_block(kt, kb, hd), None)
        for d in range(ATT_AHEAD):
            s_ref[d] = pending[d]
        return 0

    lax.fori_loop(0, qi, full_step, 0)

    start = pl.multiple_of(qi * ATT_TQ, ATT_TQ)
    pending = [s_ref[d] for d in range(ATT_AHEAD)]
    visible = {}
    for u, (kb, hd) in enumerate(units):
        lane0 = kb * ATT_KB
        if u + ATT_AHEAD < len(units):
            nxt = units[u + ATT_AHEAD]
            pending.append(scores(start, *nxt, lane0=nxt[0] * ATT_KB))
        s_cur = pending.pop(0)
        if s_cur.shape not in visible:
            k_chunk = lax.broadcasted_iota(jnp.int32, (s_cur.shape[0], 1), 0) // CHUNK
            q_chunk = lax.broadcasted_iota(jnp.int32, (1, s_cur.shape[1]), 1) // CHUNK
            visible[s_cur.shape] = k_chunk <= q_chunk
        _att_update(m_ref, acc_ref, hd, lane0, s_cur, v_block(qi, kb, hd), visible[s_cur.shape])
    for hd in heads:
        out = (acc_ref[hd, :MLA_V, :] / acc_ref[hd, MLA_V:MLA_V + 1, :]).T
        mix_ref[:, RET_W + hd * MLA_V:RET_W + (hd + 1) * MLA_V] = out.astype(bf16)


def _mixer_call(qt, km, vt, kmeta, vmeta, qr, kr, vr, gate, krmeta, vrmeta, w_f32, batch, seq):
    nq = seq // ATT_TQ
    tile = lambda w: pl.BlockSpec((ATT_TQ, w), lambda b, i: (b * nq + i, 0))
    const = lambda a: pl.BlockSpec((N_META, a.shape[1]), lambda b, i: (a.shape[0] // N_META - 1, 0))
    w_rows, rem = divmod(w_f32.shape[0], batch * nq)
    assert rem == 0 and w_rows % BF16_SUBLANES == 0
    w_slice = pl.BlockSpec((w_rows, w_f32.shape[1]), lambda b, i: (b * nq + i, 0))
    return pl.pallas_call(
        _mixer_kernel,
        grid=(batch, nq),
        in_specs=[pl.BlockSpec((1, MLA_QW, ATT_TQ), lambda b, i: (b * nq + i, 0, 0)),
                  pl.BlockSpec((1, seq, MLA_QW), lambda b, i: (b, 0, 0)),
                  pl.BlockSpec((seq // ATT_TK, MLA_HEADS * MLA_VT_ROWS, ATT_TK), lambda b, i: (b, 0, 0)),
                  const(kmeta), const(vmeta),
                  tile(RET_W), tile(RET_W), tile(RET_W), tile(RET_W),
                  const(krmeta), const(vrmeta), w_slice],
        out_specs=[tile(D_MODEL),
                   pl.BlockSpec((1, RET_HEADS, RET_DK, RET_DV), lambda b, i: (b, 0, 0, 0)), w_slice],
        out_shape=[jax.ShapeDtypeStruct((batch * seq, D_MODEL), bf16),
                   jax.ShapeDtypeStruct((batch, RET_HEADS, RET_DK, RET_DV), f32),
                   jax.ShapeDtypeStruct(w_f32.shape, bf16)],
        scratch_shapes=[pltpu.VMEM((MLA_HEADS, 1, ATT_TQ), f32),
                        pltpu.VMEM((MLA_HEADS, MLA_VT_ROWS, ATT_TQ), f32),
                        pltpu.VMEM((ATT_AHEAD, ATT_KB, ATT_TQ), f32),
                        pltpu.VMEM((RET_HEADS, RET_C, RET_C), f32),
                        pltpu.VMEM((RET_HEADS, RET_C, LANES), f32),
                        pltpu.VMEM((RET_HEADS, RET_C, LANES), f32)],
        compiler_params=pltpu.CompilerParams(dimension_semantics=("arbitrary", "arbitrary"),
                                             vmem_limit_bytes=VMEM_LIMIT),
        name="mixer",
    )(qt, km.reshape(batch, seq, MLA_QW), vt, kmeta, vmeta, qr, kr, vr, gate, krmeta, vrmeta, w_f32)


def _smixer_kernel(qm_ref, cnew_ref, pnew_ref, cmeta_ref, pmeta_ref, ccache_ref, pcache_ref, w_qk_ref, w_ov_ref,
                   qr_ref, kr_ref, vr_ref, gate_ref, s0_ref, wa_f32_ref, wb_f32_ref,
                   mix_ref, state_ref, wa_bf16_ref, wb_bf16_ref):
    wa_bf16_ref[...] = wa_f32_ref[...].astype(bf16)
    wb_bf16_ref[...] = wb_f32_ref[...].astype(bf16)

    t = qr_ref.shape[0]
    states, finish_retention = _retention_chunk(qr_ref, kr_ref, vr_ref, gate_ref, mix_ref, slice(0, t),
                                                [s0_ref[0, hd] for hd in range(RET_HEADS)],
                                                [_retention_tables(t, hd) for hd in range(RET_HEADS)])
    for hd in range(RET_HEADS):
        state_ref[0, hd] = states[hd]

    heads = range(MLA_HEADS)
    q_nope = jnp.concatenate([qm_ref[:, hd * MLA_QK_PAD:hd * MLA_QK_PAD + MLA_NOPE] for hd in heads], axis=1)
    q_abs = _dot(q_nope, w_qk_ref[...])
    finish_retention()
    q_lat = jnp.concatenate([q_abs[:, hd * MLA_KV_LORA:(hd + 1) * MLA_KV_LORA] for hd in heads], axis=0).astype(bf16)
    q_rope = jnp.concatenate([qm_ref[:, hd * MLA_QK_PAD + MLA_NOPE:hd * MLA_QK_PAD + MLA_NOPE + MLA_ROPE]
                              for hd in heads], axis=0)
    lat = [cmeta_ref[...].astype(bf16), ccache_ref[0].astype(bf16), cnew_ref[...].astype(bf16)]
    s_rope = [_dot_nt(q_rope, pmeta_ref[...].astype(bf16)), _dot(q_rope, pcache_ref[0].astype(bf16)),
              _dot_nt(q_rope, pnew_ref[...].astype(bf16))]
    scores = [(_dot_nt(q_lat, c) + sr) * MLA_SCALE for c, sr in zip(lat, s_rope)]
    m = functools.reduce(jnp.maximum, [jnp.max(s, axis=-1, keepdims=True) for s in scores])
    probs = [jnp.exp(s - m) for s in scores]
    denom = functools.reduce(jnp.add, [jnp.sum(p, axis=-1, keepdims=True) for p in probs])
    o_lat = functools.reduce(jnp.add, [_dot(p.astype(bf16), c) for p, c in zip(probs, lat)]) / denom
    o_all = jnp.concatenate([o_lat[hd * t:(hd + 1) * t, :] for hd in heads], axis=1).astype(bf16)
    mix_ref[:, RET_W:] = _dot(o_all, w_ov_ref[...]).astype(bf16)


def _smixer_call(qm, ckv, kpe, cckv, ckpe, w_qk, w_ov, qr, kr, vr, gate, s0, wa_f32, wb_f32, t):
    db, past = cckv.shape[0], cckv.shape[1]
    tile = lambda w: pl.BlockSpec((t, w), lambda b: (b, 0))
    meta = lambda w: pl.BlockSpec((N_META, w), lambda b: (db * t // N_META, 0))
    const = lambda a: pl.BlockSpec(a.shape, lambda b: (0,) * a.ndim)
    per = lambda *tail: pl.BlockSpec((1,) + tail, lambda b: (b,) + (0,) * len(tail))

    def w_slice(w):
        rows, rem = divmod(w.shape[0], db)
        assert rem == 0 and rows % BF16_SUBLANES == 0
        return pl.BlockSpec((rows, w.shape[1]), lambda b: (b, 0))

    return pl.pallas_call(
        _smixer_kernel,
        grid=(db,),
        in_specs=[tile(MLA_QW), tile(MLA_KV_LORA), tile(MLA_ROPE), meta(MLA_KV_LORA), meta(MLA_ROPE),
                  per(past, MLA_KV_LORA), per(MLA_ROPE, past), const(w_qk), const(w_ov),
                  tile(RET_W), tile(RET_W), tile(RET_W), tile(RET_W),
                  per(RET_HEADS, RET_DK, RET_DV), w_slice(wa_f32), w_slice(wb_f32)],
        out_specs=[tile(D_MODEL), per(RET_HEADS, RET_DK, RET_DV), w_slice(wa_f32), w_slice(wb_f32)],
        out_shape=[jax.ShapeDtypeStruct((db * t, D_MODEL), bf16),
                   jax.ShapeDtypeStruct((db, RET_HEADS, RET_DK, RET_DV), f32),
                   jax.ShapeDtypeStruct(wa_f32.shape, bf16), jax.ShapeDtypeStruct(wb_f32.shape, bf16)],
        compiler_params=pltpu.CompilerParams(dimension_semantics=("arbitrary",),
                                             vmem_limit_bytes=VMEM_LIMIT),
        name="smixer",
    )(qm, ckv, kpe, ckv, kpe, cckv, ckpe, w_qk, w_ov, qr, kr, vr, gate, s0, wa_f32, wb_f32)


def _tail_kernel(sub, mix_ref, h_ref, w_mo_ref, ln2_g_ref, ln2_b_ref, w_in_ref, w_out_ref, ln3_g_ref, ln3_b_ref,
                 y_ref):
    tiles = [slice(j * sub, (j + 1) * sub) for j in range(mix_ref.shape[0] // sub)]
    mixes = [_dot(mix_ref[r, :], w_mo_ref[...]) for r in tiles]
    h2s, ys = [], []
    for r, mix in zip(tiles, mixes):
        h2 = _layer_norm(DN_ALPHA * h_ref[r, :] + mix, ln2_g_ref[...], ln2_b_ref[...])
        h2s.append(h2)
        ys.append(_swiglu(h2, w_in_ref.at[:, :D_FF], w_in_ref.at[:, D_FF:], w_out_ref))
    for r, h2, y in zip(tiles, h2s, ys):
        y_ref[r, :] = _ffn_norm(h2, y, ln3_g_ref, ln3_b_ref)


def _tail_call(mix, h, wts, tm, sub):
    n = mix.shape[0]
    row = lambda w: pl.BlockSpec((tm, w), lambda i: (i, 0))
    return pl.pallas_call(
        functools.partial(_tail_kernel, sub),
        grid=(n // tm,),
        in_specs=[row(D_MODEL), row(D_MODEL)] + [_const_spec(w.shape) for w in wts],
        out_specs=row(D_MODEL),
        out_shape=jax.ShapeDtypeStruct((n, D_MODEL), f32),
        compiler_params=pltpu.CompilerParams(dimension_semantics=("arbitrary",),
                                             vmem_limit_bytes=VMEM_LIMIT),
        name="tail",
    )(mix, h, *wts)


def _rope_tables(pos):
    posf = pos.astype(f32)[:, None]
    inv_r = ROPE_BASE ** (-jnp.arange(0, RET_DK, 2, dtype=f32) / RET_DK)
    ang = posf * inv_r[None, :]
    inv_m = ROPE_BASE ** (-jnp.arange(0, MLA_ROPE, 2, dtype=f32) / MLA_ROPE)
    angm = posf * inv_m[None, :]
    return jnp.cos(ang), jnp.sin(ang), jnp.cos(angm), jnp.sin(angm)


def kernel(x_prompt, x_sample, cache_mla_ckv, cache_mla_kpe, state_ret, meta_tokens,
           ffn1_w_in, ffn1_w_out, ln1_g, ln1_b, w_mix_in, ret_gn_g, mla_q_norm_g, mla_w_uq,
           mla_kv_norm_g, mla_w_ukv, w_mix_out, ln2_g, ln2_b, ffn2_w_in, ffn2_w_out, ln3_g, ln3_b):
    B, S, D = x_prompt.shape
    DB, T, _ = x_sample.shape
    P = cache_mla_ckv.shape[1]
    assert S % ATT_TQ == 0 and S % FRONT_TM == 0 and (B * S) % TAIL_TM == 0 and ATT_TQ % RET_C == 0
    row = lambda a: a.reshape(1, -1).astype(f32)

    w_uq = mla_w_uq.reshape(MLA_Q_LORA, MLA_HEADS, MLA_NOPE + MLA_ROPE)
    w_uq = jnp.pad(w_uq, ((0, 0), (0, 0), (0, MLA_QK_PAD - MLA_NOPE - MLA_ROPE)))
    w_uq = w_uq.reshape(MLA_Q_LORA, MLA_QW).astype(bf16)
    w_ukv = mla_w_ukv.reshape(MLA_KV_LORA, MLA_HEADS, MLA_NOPE + MLA_V)
    w_uk = w_ukv[:, :, :MLA_NOPE].reshape(MLA_KV_LORA, -1).astype(bf16)
    w_uv = w_ukv[:, :, MLA_NOPE:].reshape(MLA_KV_LORA, -1).astype(bf16)
    shared_w = lambda w_mix: [row(ln1_g), row(ln1_b), w_mix, row(ret_gn_g), row(mla_q_norm_g)]

    n_s = DB * T
    x_small = jnp.concatenate([x_sample.reshape(n_s, D), meta_tokens.astype(x_sample.dtype)], axis=0)
    pos_small = jnp.concatenate([jnp.tile(N_META + P + jnp.arange(T), DB), jnp.arange(N_META)])
    (h_s, qr_s, kr_s, vr_s, gr_s, qm_s, km_s, vm_s, ckv_s, kpe_s, w1_gate, w1_up, w1_out, w_mix) = _front_small_call(
        x_small, _rope_tables(pos_small), jnp.concatenate(_rope_tables(jnp.zeros((1,), jnp.int32)), axis=1)[:, None, :],
        ffn1_w_in, ffn1_w_out, shared_w(w_mix_in.T) + [w_uq, row(mla_kv_norm_g), w_uk, w_uv])

    tabs_p = _rope_tables(N_META + jnp.arange(FRONT_TM))
    steps_p = _rope_tables(FRONT_TM * jnp.arange(S // FRONT_TM))
    tabs_t_p = (jnp.concatenate([tabs_p[2].T, tabs_p[3].T], axis=0),
                jnp.concatenate([steps_p[2], steps_p[3]], axis=1)[:, :, None])
    (h_p, qr_p, kr_p, vr_p, gr_p, qt_p, km_p, vt_p, ckv_p, kpe_p) = _front_call(
        x_prompt.reshape(B * S, D), tabs_p, jnp.concatenate(steps_p, axis=1)[:, None, :], tabs_t_p,
        [w1_gate, w1_up, w1_out] + shared_w(w_mix) + [w_uq.T, row(mla_kv_norm_g), w_uk, w_uv.T], FRONT_TM, FRONT_SUB)

    mix_p, p_state, w2_in = _mixer_call(qt_p, km_p, vt_p, km_s, vm_s, qr_p, kr_p, vr_p, gr_p, kr_s, vr_s,
                                        ffn2_w_in, B, S)
    blocks = lambda w, n: [w[:, hd * n:(hd + 1) * n] for hd in range(MLA_HEADS)]
    w_qk = jax.scipy.linalg.block_diag(*[blk.T for blk in blocks(w_uk, MLA_NOPE)])
    w_ov = jax.scipy.linalg.block_diag(*blocks(w_uv, MLA_V))
    mix_s, s_state, w2_out, w_mo = _smixer_call(qm_s, ckv_s, kpe_s, cache_mla_ckv,
                                                jnp.transpose(cache_mla_kpe, (0, 2, 1)), w_qk, w_ov,
                                                qr_s, kr_s, vr_s, gr_s, state_ret, ffn2_w_out, w_mix_out, T)

    tail_w = [w_mo, row(ln2_g), row(ln2_b), w2_in, w2_out, row(ln3_g), row(ln3_b)]
    y_p = _tail_call(mix_p, h_p, tail_w, TAIL_TM, TAIL_SUB)
    y_p, mix_s = lax.optimization_barrier((y_p, mix_s))
    y_s = _tail_call(mix_s, h_s, tail_w, n_s, n_s // 2)

    meta_ckv = jnp.broadcast_to(ckv_s[n_s:][None], (B, N_META, MLA_KV_LORA))
    meta_kpe_t = jnp.broadcast_to(kpe_s[n_s:].T[None], (B, MLA_ROPE, N_META))
    p_ckv = jnp.concatenate([meta_ckv, ckv_p.reshape(B, S, MLA_KV_LORA)], axis=1)
    p_kpe = jnp.transpose(jnp.concatenate([meta_kpe_t, kpe_p], axis=2), (0, 2, 1))
    return (y_p.reshape(B, S, D), y_s.reshape(DB, T, D), p_ckv, p_kpe, p_state.astype(x_prompt.dtype),
            ckv_s[:n_s].reshape(DB, T, MLA_KV_LORA), kpe_s[:n_s].reshape(DB, T, MLA_ROPE),
            s_state.astype(state_ret.dtype))
```
